```python
import jax, jax.numpy as jnp
from jax import lax
import numpy as np

D_MODEL = 2048
BATCH = 32
SEQ = 256
DEPTH = 2
DEC_BATCH = 2
DEC_SEQ = 2048
PAST_LEN = 512

GRID_W = 64
HEAD_DIM = 128
CONV_WIDTH = D_MODEL // 2
CONV_TAPS = 3
GQA_HEADS = (D_MODEL // 2) // HEAD_DIM
GQA_KV_HEADS = 2
NA_HEADS = D_MODEL // HEAD_DIM
NA_WIN_ROWS = 8
NA_WIN_COLS = 16
D_FF = 4 * D_MODEL
N_EVEN = (DEPTH + 1) // 2
N_ODD = DEPTH // 2
QUERY_BLOCK = 128
ROPE_THETA = 10000.0
ROPE_AXIS_DIM = HEAD_DIM // 2
NORM_EPS = 1e-6
MIX_WIDTH = CONV_WIDTH + GQA_HEADS * HEAD_DIM
AB_IN = 3 * CONV_WIDTH + (GQA_HEADS + 2 * GQA_KV_HEADS) * HEAD_DIM

kernel_name = "hybrid_dit_prefix_conv_gqa_natten"


def _rms_norm(x, g):
    xf = x.astype(jnp.float32)
    y = xf * lax.rsqrt(jnp.mean(xf * xf, axis=-1, keepdims=True) + NORM_EPS)
    return (y * g.astype(jnp.float32)).astype(x.dtype)


def _modulation(cond, w, b):
    m = jax.nn.silu(cond) @ w + b
    return [t[:, None, :] for t in jnp.split(m, 6, axis=-1)]


def _ada(x, g, shift, scale):
    return _rms_norm(x, g) * (1 + scale) + shift


def _short_conv(u, w):
    up = jnp.pad(u, ((0, 0), (1, 1), (0, 0)))
    return up[:, :-2] * w[0] + up[:, 1:-1] * w[1] + up[:, 2:] * w[2]


def _axial_rope(x):
    T = x.shape[1]
    t = jnp.arange(T)
    half = ROPE_AXIS_DIM // 2
    inv = ROPE_THETA ** (-jnp.arange(half, dtype=jnp.float32) / half)

    def rot(xa, pos):
        ang = pos.astype(jnp.float32)[:, None] * inv
        cos = jnp.cos(ang)[None, :, None, :]
        sin = jnp.sin(ang)[None, :, None, :]
        x1, x2 = xa[..., :half], xa[..., half:]
        return jnp.concatenate([x1 * cos - x2 * sin, x1 * sin + x2 * cos], axis=-1)

    xf = x.astype(jnp.float32)
    out = jnp.concatenate([rot(xf[..., :ROPE_AXIS_DIM], t // GRID_W),
                           rot(xf[..., ROPE_AXIS_DIM:], t % GRID_W)], axis=-1)
    return out.astype(x.dtype)


def _block_attention(q, k, v):
    B, S, Hq, D = q.shape
    Hkv = k.shape[2]
    G = Hq // Hkv
    nb = S // QUERY_BLOCK
    qb = q.reshape(B, nb, QUERY_BLOCK, Hkv, G, D).transpose(1, 0, 2, 3, 4, 5)
    scale = D ** -0.5

    def one(qblk):
        s = jnp.einsum('bqhgd,bkhd->bhgqk', qblk, k).astype(jnp.float32) * scale
        p = jax.nn.softmax(s, axis=-1).astype(v.dtype)
        return jnp.einsum('bhgqk,bkhd->bqhgd', p, v)

    o = lax.map(one, qb)
    return o.transpose(1, 0, 2, 3, 4, 5).reshape(B, S, Hq * D)


def _neighbourhood_attention(q, k, v, ck, cv, rel_bias):
    B, T, H, D = q.shape
    rows = T // GRID_W
    wr = min(NA_WIN_ROWS, rows)
    wc = NA_WIN_COLS
    nk = wr * wc
    r = jnp.arange(rows)
    col = jnp.arange(GRID_W)
    krow = jnp.clip(r - wr // 2, 0, rows - wr)[:, None] + jnp.arange(wr)
    kcol = jnp.clip(col - wc // 2, 0, GRID_W - wc)[:, None] + jnp.arange(wc)
    idx = (krow[:, None, :, None] * GRID_W + kcol[None, :, None, :]).reshape(rows, GRID_W, nk)
    drow = krow - r[:, None] + (NA_WIN_ROWS - 1)
    dcol = kcol - col[:, None] + (NA_WIN_COLS - 1)
    qr = q.reshape(B, rows, GRID_W, H, D).transpose(1, 0, 2, 3, 4)
    scale = D ** -0.5

    def one(args):
        q_blk, idx_blk, drow_blk = args
        kg = k[:, idx_blk]
        vg = v[:, idx_blk]
        bias = rel_bias[:, drow_blk[None, :, None], dcol[:, None, :]].reshape(H, GRID_W, nk)
        s_loc = jnp.einsum('bqhd,bqkhd->bhqk', q_blk, kg).astype(jnp.float32) * scale \
            + bias.astype(jnp.float32)[None]
        s_ctx = jnp.einsum('bqhd,bkhd->bhqk', q_blk, ck).astype(jnp.float32) * scale
        p = jax.nn.softmax(jnp.concatenate([s_loc, s_ctx], axis=-1), axis=-1).astype(v.dtype)
        return (jnp.einsum('bhqk,bqkhd->bqhd', p[..., :nk], vg)
                + jnp.einsum('bhqk,bkhd->bqhd', p[..., nk:], cv))

    o = lax.map(one, (qr, idx, drow))
    return o.transpose(1, 0, 2, 3, 4).reshape(B, T, H * D)


def _ab_project(h, w_in, q_norm, k_norm):
    B, T, _ = h.shape
    z = h @ w_in
    c1 = CONV_WIDTH
    c2 = 2 * c1
    c3 = 3 * c1
    c4 = c3 + GQA_HEADS * HEAD_DIM
    c5 = c4 + GQA_KV_HEADS * HEAD_DIM
    gate_b, gate_c, xa = z[..., :c1], z[..., c1:c2], z[..., c2:c3]
    q = _rms_norm(z[..., c3:c4].reshape(B, T, GQA_HEADS, HEAD_DIM), q_norm)
    k = _rms_norm(z[..., c4:c5].reshape(B, T, GQA_KV_HEADS, HEAD_DIM), k_norm)
    v = z[..., c5:].reshape(B, T, GQA_KV_HEADS, HEAD_DIM)
    return gate_b, gate_c, xa, q, k, v


def _ab_context(h, w_in, conv_w, q_norm, k_norm, w_out):
    gb, gc, xa, q, k, v = _ab_project(h, w_in, q_norm, k_norm)
    a = gb * _short_conv(gc * xa, conv_w)
    b = _block_attention(q, k, v)
    return jnp.concatenate([a, b], axis=-1) @ w_out, k, v


def _ab_latent(h, ctx_k, ctx_v, w_in, conv_w, q_norm, k_norm, w_out):
    gb, gc, xa, q, k, v = _ab_project(h, w_in, q_norm, k_norm)
    a = gb * _short_conv(gc * xa, conv_w)
    q = _axial_rope(q)
    k = _axial_rope(k)
    b = _block_attention(q, jnp.concatenate([ctx_k, k], axis=1),
                         jnp.concatenate([ctx_v, v], axis=1))
    return jnp.concatenate([a, b], axis=-1) @ w_out


def _na_project(h, w_qkv):
    B, T, _ = h.shape
    q, k, v = jnp.split(h @ w_qkv, 3, axis=-1)
    shp = (B, T, NA_HEADS, HEAD_DIM)
    return q.reshape(shp), k.reshape(shp), v.reshape(shp)


def _na_context(h, w_qkv, w_out):
    q, k, v = _na_project(h, w_qkv)
    return _block_attention(q, k, v) @ w_out, k, v


def _na_latent(h, ctx_k, ctx_v, w_qkv, rel_bias, w_out):
    q, k, v = _na_project(h, w_qkv)
    return _neighbourhood_attention(q, k, v, ctx_k, ctx_v, rel_bias) @ w_out


def _mlp(h, w1, w2):
    return jnp.square(jax.nn.relu(h @ w1)) @ w2


def setup_inputs(seed: int = 0) -> dict:
    key = jax.random.key(seed)
    ks = jax.random.split(key, 23)

    def nrm(k, shape, scale=1.0):
        return jax.random.normal(k, shape, jnp.float32) * scale

    return {
        "x_prompt": nrm(ks[0], (BATCH, SEQ, D_MODEL)),
        "x_sample": nrm(ks[1], (DEC_BATCH, DEC_SEQ, D_MODEL)),
        "cache_attn_k": nrm(ks[2], (DEC_BATCH, N_EVEN, PAST_LEN, GQA_KV_HEADS, HEAD_DIM)),
        "cache_attn_v": nrm(ks[3], (DEC_BATCH, N_EVEN, PAST_LEN, GQA_KV_HEADS, HEAD_DIM)),
        "cache_na_k": nrm(ks[4], (DEC_BATCH, N_ODD, PAST_LEN, NA_HEADS, HEAD_DIM)),
        "cache_na_v": nrm(ks[5], (DEC_BATCH, N_ODD, PAST_LEN, NA_HEADS, HEAD_DIM)),
        "c": nrm(ks[6], (DEC_BATCH, D_MODEL)),
        "c_ctx": nrm(ks[7], (D_MODEL,)),
        "mod_w": nrm(ks[8], (DEPTH, D_MODEL, 6 * D_MODEL), D_MODEL ** -0.5),
        "mod_b": nrm(ks[9], (DEPTH, 6 * D_MODEL), 0.01),
        "norm1_g": 1.0 + nrm(ks[10], (DEPTH, D_MODEL), 0.02),
        "norm2_g": 1.0 + nrm(ks[11], (DEPTH, D_MODEL), 0.02),
        "ab_w_in": nrm(ks[12], (N_EVEN, D_MODEL, AB_IN), D_MODEL ** -0.5),
        "ab_conv_w": nrm(ks[13], (N_EVEN, CONV_TAPS, CONV_WIDTH), CONV_TAPS ** -0.5),
        "ab_q_norm": 1.0 + nrm(ks[14], (N_EVEN, HEAD_DIM), 0.02),
        "ab_k_norm": 1.0 + nrm(ks[15], (N_EVEN, HEAD_DIM), 0.02),
        "ab_w_out": nrm(ks[16], (N_EVEN, MIX_WIDTH, D_MODEL), MIX_WIDTH ** -0.5),
        "na_w_qkv": nrm(ks[17], (N_ODD, D_MODEL, 3 * NA_HEADS * HEAD_DIM), D_MODEL ** -0.5),
        "na_rel_bias": nrm(ks[18], (N_ODD, NA_HEADS, 2 * NA_WIN_ROWS - 1, 2 * NA_WIN_COLS - 1), 0.1),
        "na_w_out": nrm(ks[19], (N_ODD, NA_HEADS * HEAD_DIM, D_MODEL), (NA_HEADS * HEAD_DIM) ** -0.5),
        "mlp_w1": nrm(ks[20], (DEPTH, D_MODEL, D_FF), D_MODEL ** -0.5),
        "mlp_w2": nrm(ks[21], (DEPTH, D_FF, D_MODEL), D_FF ** -0.5),
        "final_norm_g": 1.0 + nrm(ks[22], (D_MODEL,), 0.02),
    }


def reference(x_prompt, x_sample, cache_attn_k, cache_attn_v, cache_na_k, cache_na_v, c, c_ctx,
              mod_w, mod_b, norm1_g, norm2_g, ab_w_in, ab_conv_w, ab_q_norm, ab_k_norm, ab_w_out,
              na_w_qkv, na_rel_bias, na_w_out, mlp_w1, mlp_w2, final_norm_g):
    xp = x_prompt
    xs = x_sample
    cond_ctx = c_ctx[None, :]
    new_ak, new_av, new_nk, new_nv = [], [], [], []
    for i in range(DEPTH):
        j = i // 2
        sp1, cp1, gp1, sp2, cp2, gp2 = _modulation(cond_ctx, mod_w[i], mod_b[i])
        ss1, cs1, gs1, ss2, cs2, gs2 = _modulation(c, mod_w[i], mod_b[i])
        hp = _ada(xp, norm1_g[i], sp1, cp1)
        hs = _ada(xs, norm1_g[i], ss1, cs1)
        if i % 2 == 0:
            op, kp, vp = _ab_context(hp, ab_w_in[j], ab_conv_w[j], ab_q_norm[j], ab_k_norm[j],
                                     ab_w_out[j])
            os_ = _ab_latent(hs, cache_attn_k[:, j], cache_attn_v[:, j], ab_w_in[j], ab_conv_w[j],
                             ab_q_norm[j], ab_k_norm[j], ab_w_out[j])
            new_ak.append(kp)
            new_av.append(vp)
        else:
            op, kp, vp = _na_context(hp, na_w_qkv[j], na_w_out[j])
            os_ = _na_latent(hs, cache_na_k[:, j], cache_na_v[:, j], na_w_qkv[j], na_rel_bias[j],
                             na_w_out[j])
            new_nk.append(kp)
            new_nv.append(vp)
        xp = xp + gp1 * op
        xs = xs + gs1 * os_
        xp = xp + gp2 * _mlp(_ada(xp, norm2_g[i], sp2, cp2), mlp_w1[i], mlp_w2[i])
        xs = xs + gs2 * _mlp(_ada(xs, norm2_g[i], ss2, cs2), mlp_w1[i], mlp_w2[i])
    y_prompt = _rms_norm(xp, final_norm_g)
    y_sample = _rms_norm(xs, final_norm_g)
    new_attn_k = jnp.stack(new_ak, axis=1)
    new_attn_v = jnp.stack(new_av, axis=1)
    new_na_k = jnp.stack(new_nk, axis=1)
    new_na_v = jnp.stack(new_nv, axis=1)
    return (y_prompt, y_sample, new_attn_k, new_attn_v, new_na_k, new_na_v)
```

```python
import functools

import numpy as np
import jax
import jax.numpy as jnp
from jax import lax
from jax.experimental import pallas as pl
from jax.experimental.pallas import tpu as pltpu

D_MODEL = 2048
BATCH = 32
SEQ = 256
DEC_BATCH = 2
DEC_SEQ = 2048
PAST_LEN = 512
GRID_W = 64
GRID_H = DEC_SEQ // GRID_W
HEAD_DIM = 128
CONV_WIDTH = D_MODEL // 2
GQA_HEADS = 8
GQA_KV_HEADS = 2
GQA_GROUP = GQA_HEADS // GQA_KV_HEADS
NA_HEADS = 16
NA_WIN_ROWS = 8
NA_WIN_COLS = 16
D_FF = 4 * D_MODEL
ROPE_THETA = 10000.0
NORM_EPS = 1e-6
AB_IN = 3 * CONV_WIDTH + (GQA_HEADS + 2 * GQA_KV_HEADS) * HEAD_DIM
KV_W = GQA_KV_HEADS * HEAD_DIM
ATTN_SCALE = HEAD_DIM ** -0.5

NA_QROWS = 4
NA_SLAB_ROWS = NA_QROWS + NA_WIN_ROWS
NA_QBLK = NA_QROWS * GRID_W
NA_SLAB = NA_SLAB_ROWS * GRID_W
NA_NBLK = GRID_H // NA_QROWS
MASK_NEG = -1e30

V7X_VMEM_LIMIT = 56 * 1024 * 1024

BF16 = jnp.bfloat16
F32 = jnp.float32


def _cparams(n_axes):
    return pltpu.CompilerParams(
        dimension_semantics=("parallel",) + ("arbitrary",) * (n_axes - 1),
        vmem_limit_bytes=V7X_VMEM_LIMIT)


def _nt_dot(a, b):
    return lax.dot_general(a, b, (((1,), (1,)), ((), ())), preferred_element_type=F32)


def _ada_norm(x, g, shift, scale):
    ms = jnp.mean(x * x, axis=-1, keepdims=True)
    y = x * lax.rsqrt(ms + NORM_EPS)
    return (y * g) * (1.0 + scale) + shift


def _head_norm(x, g):
    ms = jnp.mean(x * x, axis=-1, keepdims=True)
    return x * lax.rsqrt(ms + NORM_EPS) * g


def _rope(x, cos, sin_lo, sin_hi):
    return x * cos + pltpu.roll(x, HEAD_DIM - 32, 1) * sin_lo + pltpu.roll(x, 32, 1) * sin_hi


def _softmax_pv(scores, values):
    m = scores[0].max(axis=-1, keepdims=True)
    for s in scores[1:]:
        m = jnp.maximum(m, s.max(axis=-1, keepdims=True))
    l = None
    o = None
    for s, v in zip(scores, values):
        p = jnp.exp(s - m)
        ls = jnp.sum(p, axis=-1, keepdims=True)
        os_ = jnp.dot(p.astype(BF16), v, preferred_element_type=F32)
        l = ls if l is None else l + ls
        o = os_ if o is None else o + os_
    return o / l


def _mod_kernel(c_ref, w_ref, b_ref, o_ref):
    c = c_ref[...]
    s = (c * jax.nn.sigmoid(c)).astype(BF16)
    o_ref[...] = jnp.dot(s, w_ref[...].astype(BF16), preferred_element_type=F32) + b_ref[...]


def _modulation(cond, mod_w, mod_b):
    depth, d, n = mod_w.shape
    tn = 1024
    rows = cond.shape[0]
    return pl.pallas_call(
        _mod_kernel,
        grid=(depth, n // tn),
        in_specs=[
            pl.BlockSpec((rows, d), lambda l, j: (0, 0)),
            pl.BlockSpec((None, d, tn), lambda l, j: (l, 0, j)),
            pl.BlockSpec((None, 1, tn), lambda l, j: (l, 0, j)),
        ],
        out_specs=pl.BlockSpec((None, rows, tn), lambda l, j: (l, 0, j)),
        out_shape=jax.ShapeDtypeStruct((depth, rows, n), F32),
        compiler_params=_cparams(2),
        name="modulation",
    )(cond, mod_w, mod_b.reshape(depth, 1, n))


def _mod_spec(which, row_of_tile):
    return pl.BlockSpec((None, None, 1, D_MODEL), lambda i, j: (row_of_tile(i), which, 0, 0))


def _in0_kernel(x_ref, g_ref, shift_ref, scale_ref, w_ref, qn_ref, kn_ref, *rest, rope, tn):
    if rope:
        cos_ref, slo_ref, shi_ref, zc_ref, q_ref, kv_ref, h_scr = rest
    else:
        zc_ref, q_ref, kv_ref, h_scr = rest
    j = pl.program_id(1)
    n_zc = 3 * CONV_WIDTH // tn
    n_q = GQA_HEADS * HEAD_DIM // tn

    @pl.when(j == 0)
    def _():
        h_scr[...] = _ada_norm(x_ref[...], g_ref[...], shift_ref[...], scale_ref[...]).astype(BF16)

    acc = jnp.dot(h_scr[...], w_ref[...], preferred_element_type=F32)

    def normed(blk, gain):
        y = _head_norm(blk, gain)
        if rope:
            y = _rope(y, cos_ref[...], slo_ref[...], shi_ref[...])
        return y

    @pl.when(j < n_zc)
    def _():
        zc_ref[...] = acc.astype(zc_ref.dtype)

    @pl.when((j >= n_zc) & (j < n_zc + n_q))
    def _():
        for hh in range(tn // HEAD_DIM):
            sl = slice(hh * HEAD_DIM, (hh + 1) * HEAD_DIM)
            q_ref[:, sl] = (normed(acc[:, sl], qn_ref[...]) * ATTN_SCALE).astype(q_ref.dtype)

    @pl.when(j == n_zc + n_q)
    def _():
        for hh in range(GQA_KV_HEADS):
            sl = slice(hh * HEAD_DIM, (hh + 1) * HEAD_DIM)
            kv_ref[:, sl] = normed(acc[:, sl], kn_ref[...]).astype(kv_ref.dtype)
        kv_ref[:, KV_W:] = acc[:, KV_W:].astype(kv_ref.dtype)


def _in_proj0(x, norm_g, mods, row_of_tile, w, qn, kn, rope_tabs, kv_dtype, tm):
    t = x.shape[0]
    tn = 2 * KV_W
    n_zc = 3 * CONV_WIDTH // tn
    n_q = GQA_HEADS * HEAD_DIM // tn
    rope = rope_tabs is not None
    vec = pl.BlockSpec((1, HEAD_DIM), lambda i, j: (0, 0))
    in_specs = [
        pl.BlockSpec((tm, D_MODEL), lambda i, j: (i, 0)),
        pl.BlockSpec((1, D_MODEL), lambda i, j: (0, 0)),
        _mod_spec(0, row_of_tile),
        _mod_spec(1, row_of_tile),
        pl.BlockSpec((D_MODEL, tn), lambda i, j: (0, j)),
        vec, vec,
    ]
    args = [x, norm_g, mods, mods, w, qn, kn]
    if rope:
        nt = DEC_SEQ // tm
        tab = pl.BlockSpec((tm, HEAD_DIM), lambda i, j: (i % nt, 0))
        in_specs += [tab, tab, tab]
        args += list(rope_tabs)
    return pl.pallas_call(
        functools.partial(_in0_kernel, rope=rope, tn=tn),
        grid=(t // tm, AB_IN // tn),
        in_specs=in_specs,
        out_specs=[
            pl.BlockSpec((tm, tn), lambda i, j: (i, jnp.minimum(j, n_zc - 1))),
            pl.BlockSpec((tm, tn), lambda i, j: (i, jnp.clip(j - n_zc, 0, n_q - 1))),
            pl.BlockSpec((tm, tn), lambda i, j: (i, 0)),
        ],
        out_shape=[
            jax.ShapeDtypeStruct((t, 3 * CONV_WIDTH), BF16),
            jax.ShapeDtypeStruct((t, GQA_HEADS * HEAD_DIM), BF16),
            jax.ShapeDtypeStruct((t, 2 * KV_W), kv_dtype),
        ],
        scratch_shapes=[pltpu.VMEM((tm, D_MODEL), BF16)],
        compiler_params=_cparams(2),
        name="in_proj0",
    )(*args)


def _in1_kernel(x_ref, g_ref, shift_ref, scale_ref, w_ref, q_ref, k_ref, v_ref, h_scr, *, nb):
    j = pl.program_id(1)

    @pl.when(j == 0)
    def _():
        h_scr[...] = _ada_norm(x_ref[...], g_ref[...], shift_ref[...], scale_ref[...]).astype(BF16)

    acc = jnp.dot(h_scr[...], w_ref[...], preferred_element_type=F32)

    @pl.when(j < nb)
    def _():
        q_ref[...] = (acc * ATTN_SCALE).astype(q_ref.dtype)

    @pl.when((j >= nb) & (j < 2 * nb))
    def _():
        k_ref[...] = acc.astype(k_ref.dtype)

    @pl.when(j >= 2 * nb)
    def _():
        v_ref[...] = acc.astype(v_ref.dtype)


def _in_proj1(x, norm_g, mods, row_of_tile, w, kv_dtype, tm, tn):
    t = x.shape[0]
    nb = D_MODEL // tn
    return pl.pallas_call(
        functools.partial(_in1_kernel, nb=nb),
        grid=(t // tm, 3 * nb),
        in_specs=[
            pl.BlockSpec((tm, D_MODEL), lambda i, j: (i, 0)),
            pl.BlockSpec((1, D_MODEL), lambda i, j: (0, 0)),
            _mod_spec(0, row_of_tile),
            _mod_spec(1, row_of_tile),
            pl.BlockSpec((D_MODEL, tn), lambda i, j: (0, j)),
        ],
        out_specs=[
            pl.BlockSpec((tm, tn), lambda i, j: (i, jnp.minimum(j, nb - 1))),
            pl.BlockSpec((tm, tn), lambda i, j: (i, jnp.clip(j - nb, 0, nb - 1))),
            pl.BlockSpec((tm, tn), lambda i, j: (i, jnp.maximum(j - 2 * nb, 0))),
        ],
        out_shape=[
            jax.ShapeDtypeStruct((t, D_MODEL), BF16),
            jax.ShapeDtypeStruct((t, D_MODEL), kv_dtype),
            jax.ShapeDtypeStruct((t, D_MODEL), kv_dtype),
        ],
        scratch_shapes=[pltpu.VMEM((tm, D_MODEL), BF16)],
        compiler_params=_cparams(2),
        name="in_proj1",
    )(x, norm_g, mods, mods, w)


def _gated_conv(zc, prev_row, next_row, cw):
    c = CONV_WIDTH
    s = zc.shape[0]
    gb = zc[:, 0:c].astype(F32)
    u = zc[:, c:2 * c].astype(F32) * zc[:, 2 * c:3 * c].astype(F32)
    row = lax.broadcasted_iota(jnp.int32, u.shape, 0)
    prev = jnp.where(row == 0, prev_row, pltpu.roll(u, 1, 0))
    nxt = jnp.where(row == s - 1, next_row, pltpu.roll(u, s - 1, 0))
    return gb * (prev * cw[0:1] + u * cw[1:2] + nxt * cw[2:3])


def _attn0_ctx_kernel(zc_ref, q_ref, kv_ref, cw_ref, o_ref):
    zero = jnp.zeros((1, CONV_WIDTH), F32)
    a = _gated_conv(zc_ref[...], zero, zero, cw_ref[...])
    o_ref[:, 0:CONV_WIDTH] = a.astype(o_ref.dtype)
    for g in range(GQA_KV_HEADS):
        k = kv_ref[:, g * HEAD_DIM:(g + 1) * HEAD_DIM].astype(BF16)
        v = kv_ref[:, KV_W + g * HEAD_DIM:KV_W + (g + 1) * HEAD_DIM].astype(BF16)
        for hh in range(GQA_GROUP):
            h = g * GQA_GROUP + hh
            q = q_ref[:, h * HEAD_DIM:(h + 1) * HEAD_DIM]
            o = _softmax_pv([_nt_dot(q, k)], [v])
            o_ref[:, CONV_WIDTH + h * HEAD_DIM:CONV_WIDTH + (h + 1) * HEAD_DIM] = o.astype(o_ref.dtype)


def _attn0_ctx(zc, q, kv, cw):
    t = zc.shape[0]
    return pl.pallas_call(
        _attn0_ctx_kernel,
        grid=(t // SEQ,),
        in_specs=[
            pl.BlockSpec((SEQ, 3 * CONV_WIDTH), lambda b: (b, 0)),
            pl.BlockSpec((SEQ, GQA_HEADS * HEAD_DIM), lambda b: (b, 0)),
            pl.BlockSpec((SEQ, 2 * KV_W), lambda b: (b, 0)),
            pl.BlockSpec((3, CONV_WIDTH), lambda b: (0, 0)),
        ],
        out_specs=pl.BlockSpec((SEQ, D_MODEL), lambda b: (b, 0)),
        out_shape=jax.ShapeDtypeStruct((t, D_MODEL), BF16),
        compiler_params=_cparams(1),
        name="mixer0_ctx",
    )(zc, q, kv, cw)


_HALO = 16


def _attn0_lat_kernel(zc_ref, zp_ref, zn_ref, q_ref, kv_ref, ck_ref, cv_ref, cw_ref, o_ref, *, nqb):
    qb = pl.program_id(1)
    c = CONV_WIDTH
    up = zp_ref[:, c:2 * c].astype(F32) * zp_ref[:, 2 * c:3 * c].astype(F32)
    un = zn_ref[:, c:2 * c].astype(F32) * zn_ref[:, 2 * c:3 * c].astype(F32)
    up = jnp.where(qb == 0, 0.0, up[_HALO - 1:_HALO, :])
    un = jnp.where(qb == nqb - 1, 0.0, un[0:1, :])
    a = _gated_conv(zc_ref[...], up, un, cw_ref[...])
    o_ref[:, 0:c] = a.astype(o_ref.dtype)
    for g in range(GQA_KV_HEADS):
        ks = slice(g * HEAD_DIM, (g + 1) * HEAD_DIM)
        vs = slice(KV_W + g * HEAD_DIM, KV_W + (g + 1) * HEAD_DIM)
        k = kv_ref[:, ks]
        v = kv_ref[:, vs]
        ck = ck_ref[:, ks].astype(BF16)
        cv = cv_ref[:, ks].astype(BF16)
        for hh in range(GQA_GROUP):
            h = g * GQA_GROUP + hh
            q = q_ref[:, h * HEAD_DIM:(h + 1) * HEAD_DIM]
            o = _softmax_pv([_nt_dot(q, ck), _nt_dot(q, k)], [cv, v])
            o_ref[:, c + h * HEAD_DIM:c + (h + 1) * HEAD_DIM] = o.astype(o_ref.dtype)


def _attn0_lat(zc, q, kv, ck, cv, cw, tq):
    nb, t, _ = zc.shape
    nqb = t // tq
    hb = tq // _HALO
    return pl.pallas_call(
        functools.partial(_attn0_lat_kernel, nqb=nqb),
        grid=(nb, nqb),
        in_specs=[
            pl.BlockSpec((None, tq, 3 * CONV_WIDTH), lambda b, i: (b, i, 0)),
            pl.BlockSpec((None, _HALO, 3 * CONV_WIDTH), lambda b, i: (b, jnp.maximum(i * hb - 1, 0), 0)),
            pl.BlockSpec((None, _HALO, 3 * CONV_WIDTH),
                         lambda b, i: (b, jnp.minimum((i + 1) * hb, t // _HALO - 1), 0)),
            pl.BlockSpec((None, tq, GQA_HEADS * HEAD_DIM), lambda b, i: (b, i, 0)),
            pl.BlockSpec((None, t, 2 * KV_W), lambda b, i: (b, 0, 0)),
            pl.BlockSpec((None, PAST_LEN, KV_W), lambda b, i: (b, 0, 0)),
            pl.BlockSpec((None, PAST_LEN, KV_W), lambda b, i: (b, 0, 0)),
            pl.BlockSpec((3, CONV_WIDTH), lambda b, i: (0, 0)),
        ],
        out_specs=pl.BlockSpec((None, tq, D_MODEL), lambda b, i: (b, i, 0)),
        out_shape=jax.ShapeDtypeStruct((nb, t, D_MODEL), BF16),
        compiler_params=_cparams(2),
        name="mixer0_lat",
    )(zc, zc, zc, q, kv, ck, cv, cw)


def _attn1_ctx_kernel(q_ref, k_ref, v_ref, o_ref):
    for h in range(NA_HEADS):
        sl = slice(h * HEAD_DIM, (h + 1) * HEAD_DIM)
        k = k_ref[:, sl].astype(BF16)
        v = v_ref[:, sl].astype(BF16)
        o = _softmax_pv([_nt_dot(q_ref[:, sl], k)], [v])
        o_ref[:, sl] = o.astype(o_ref.dtype)


def _attn1_ctx(q, k, v):
    t = q.shape[0]
    blk = pl.BlockSpec((SEQ, D_MODEL), lambda b: (b, 0))
    return pl.pallas_call(
        _attn1_ctx_kernel,
        grid=(t // SEQ,),
        in_specs=[blk, blk, blk],
        out_specs=blk,
        out_shape=jax.ShapeDtypeStruct((t, D_MODEL), BF16),
        compiler_params=_cparams(1),
        name="mixer1_ctx",
    )(q, k, v)


def _na_slab_row(blk):
    return min(max(NA_QROWS * blk - NA_WIN_ROWS // 2, 0), GRID_H - NA_SLAB_ROWS)


def _na_pattern(blk):
    return 0 if blk == 0 else (2 if blk == NA_NBLK - 1 else 1)


def _na_bias_index():
    idx = np.zeros((3, NA_QBLK, NA_SLAB), np.int32)
    valid = np.zeros((3, NA_QBLK, NA_SLAB), bool)
    dr, c, j, kc = np.meshgrid(np.arange(NA_QROWS), np.arange(GRID_W), np.arange(NA_SLAB_ROWS),
                               np.arange(GRID_W), indexing="ij")
    for p, blk in enumerate((0, 1, NA_NBLK - 1)):
        r = NA_QROWS * blk + dr
        krow = _na_slab_row(blk) + j
        ks = np.clip(r - NA_WIN_ROWS // 2, 0, GRID_H - NA_WIN_ROWS)
        cs = np.clip(c - NA_WIN_COLS // 2, 0, GRID_W - NA_WIN_COLS)
        ok = (krow >= ks) & (krow < ks + NA_WIN_ROWS) & (kc >= cs) & (kc < cs + NA_WIN_COLS)
        drow = krow - r + (NA_WIN_ROWS - 1)
        dcol = kc - c + (NA_WIN_COLS - 1)
        flat = np.where(ok, drow * (2 * NA_WIN_COLS - 1) + dcol, 0)
        idx[p] = flat.reshape(NA_QBLK, NA_SLAB)
        valid[p] = ok.reshape(NA_QBLK, NA_SLAB)
    return idx, valid


def _na_bias(rel_bias):
    idx, valid = _na_bias_index()
    flat = rel_bias.reshape(NA_HEADS, -1)
    return jnp.where(valid[None], flat[:, idx], MASK_NEG).astype(F32)


def _attn1_lat_kernel(q_ref, k_ref, v_ref, ck_ref, cv_ref, bias_ref, o_ref):
    ck = ck_ref[...].astype(BF16)
    cv = cv_ref[...].astype(BF16)
    for blk in range(NA_NBLK):
        rows = slice(blk * NA_QBLK, (blk + 1) * NA_QBLK)
        s0 = _na_slab_row(blk) * GRID_W
        slab = slice(s0, s0 + NA_SLAB)
        q = q_ref[rows, :]
        s_loc = _nt_dot(q, k_ref[slab, :]) + bias_ref[_na_pattern(blk)]
        o = _softmax_pv([s_loc, _nt_dot(q, ck)], [v_ref[slab, :], cv])
        o_ref[rows, :] = o.astype(o_ref.dtype)


def _attn1_lat(q, k, v, ck, cv, bias):
    nb, t, _ = q.shape
    head = pl.BlockSpec((None, t, HEAD_DIM), lambda b, h: (b, 0, h))
    ctx = pl.BlockSpec((None, PAST_LEN, HEAD_DIM), lambda b, h: (b, 0, h))
    return pl.pallas_call(
        _attn1_lat_kernel,
        grid=(nb, NA_HEADS),
        in_specs=[head, head, head, ctx, ctx,
                  pl.BlockSpec((None, 3, NA_QBLK, NA_SLAB), lambda b, h: (h, 0, 0, 0))],
        out_specs=head,
        out_shape=jax.ShapeDtypeStruct((nb, t, D_MODEL), BF16),
        compiler_params=_cparams(2),
        name="mixer1_lat",
    )(q, k, v, ck, cv, bias)


def _out_proj_kernel(m_ref, w_ref, x_ref, gate_ref, o_ref):
    acc = jnp.dot(m_ref[...], w_ref[...], preferred_element_type=F32)
    o_ref[...] = x_ref[...] + gate_ref[...] * acc


def _out_proj(mix, w, x, mods, row_of_tile, tm, tn):
    t, k = mix.shape
    return pl.pallas_call(
        _out_proj_kernel,
        grid=(t // tm, D_MODEL // tn),
        in_specs=[
            pl.BlockSpec((tm, k), lambda i, j: (i, 0)),
            pl.BlockSpec((k, tn), lambda i, j: (0, j)),
            pl.BlockSpec((tm, tn), lambda i, j: (i, j)),
            pl.BlockSpec((None, None, 1, tn), lambda i, j: (row_of_tile(i), 2, 0, j)),
        ],
        out_specs=pl.BlockSpec((tm, tn), lambda i, j: (i, j)),
        out_shape=jax.ShapeDtypeStruct((t, D_MODEL), F32),
        compiler_params=_cparams(2),
        name="out_proj",
    )(mix, w, x, mods)


def _mlp_kernel(x_ref, g_ref, shift_ref, scale_ref, gate_ref, w1_ref, w2_ref, fg_ref, o_ref, h_scr,
                *, nf, final):
    f = pl.program_id(1)

    @pl.when(f == 0)
    def _():
        h_scr[...] = _ada_norm(x_ref[...], g_ref[...], shift_ref[...], scale_ref[...]).astype(BF16)

    u = jnp.dot(h_scr[...], w1_ref[...], preferred_element_type=F32)
    u = jnp.square(jnp.maximum(u, 0.0)).astype(BF16)
    part = jnp.dot(u, w2_ref[...], preferred_element_type=F32)

    @pl.when(f == 0)
    def _():
        o_ref[...] = part

    @pl.when(f > 0)
    def _():
        o_ref[...] += part

    @pl.when(f == nf - 1)
    def _():
        y = x_ref[...] + gate_ref[...] * o_ref[...]
        if final:
            ms = jnp.mean(y * y, axis=-1, keepdims=True)
            y = y * lax.rsqrt(ms + NORM_EPS) * fg_ref[...]
        o_ref[...] = y


def _mlp(x, norm_g, mods, row_of_tile, w1, w2, final_g, final, tm, tf):
    t = x.shape[0]
    nf = D_FF // tf

    def mod(which):
        return pl.BlockSpec((None, None, 1, D_MODEL), lambda i, f: (row_of_tile(i), which, 0, 0))

    return pl.pallas_call(
        functools.partial(_mlp_kernel, nf=nf, final=final),
        grid=(t // tm, nf),
        in_specs=[
            pl.BlockSpec((tm, D_MODEL), lambda i, f: (i, 0)),
            pl.BlockSpec((1, D_MODEL), lambda i, f: (0, 0)),
            mod(3), mod(4), mod(5),
            pl.BlockSpec((D_MODEL, tf), lambda i, f: (0, f)),
            pl.BlockSpec((tf, D_MODEL), lambda i, f: (f, 0)),
            pl.BlockSpec((1, D_MODEL), lambda i, f: (0, 0)),
        ],
        out_specs=pl.BlockSpec((tm, D_MODEL), lambda i, f: (i, 0)),
        out_shape=jax.ShapeDtypeStruct((t, D_MODEL), F32),
        scratch_shapes=[pltpu.VMEM((tm, D_MODEL), BF16)],
        compiler_params=_cparams(2),
        name="mlp",
    )(x, norm_g, mods, mods, mods, w1, w2, final_g)


def _rope_tables():
    t = np.arange(DEC_SEQ)
    half = HEAD_DIM // 4
    inv = ROPE_THETA ** (-np.arange(half, dtype=np.float32) / half)
    ang_r = (t // GRID_W).astype(np.float32)[:, None] * inv
    ang_c = (t % GRID_W).astype(np.float32)[:, None] * inv
    zero = np.zeros_like(ang_r)
    cos = np.concatenate([np.cos(ang_r)] * 2 + [np.cos(ang_c)] * 2, axis=-1)
    sin_lo = np.concatenate([-np.sin(ang_r), zero, -np.sin(ang_c), zero], axis=-1)
    sin_hi = np.concatenate([zero, np.sin(ang_r), zero, np.sin(ang_c)], axis=-1)
    return tuple(jnp.asarray(a, F32) for a in (cos, sin_lo, sin_hi))


def kernel(x_prompt, x_sample, cache_attn_k, cache_attn_v, cache_na_k, cache_na_v, c, c_ctx, mod_w, mod_b,
           norm1_g, norm2_g, ab_w_in, ab_conv_w, ab_q_norm, ab_k_norm, ab_w_out, na_w_qkv, na_rel_bias,
           na_w_out, mlp_w1, mlp_w2, final_norm_g):
    n_ctx = BATCH * SEQ
    xp = x_prompt.reshape(n_ctx, D_MODEL)
    xs = x_sample.reshape(DEC_BATCH * DEC_SEQ, D_MODEL)

    cond = jnp.concatenate([c_ctx[None, :], c, jnp.zeros((8 - 1 - DEC_BATCH, D_MODEL), F32)], axis=0)
    mods = _modulation(cond, mod_w, mod_b).reshape(2, 8, 6, 1, D_MODEL)

    tm_in, tm_out, tm_mlp, tf = 512, 1024, 512, 1024

    def ctx_row(i):
        return 0

    def lat_row(tm):
        return lambda i: 1 + i // (DEC_SEQ // tm)

    rope_tabs = _rope_tables()
    fg = final_norm_g.reshape(1, D_MODEL)

    m0 = mods[0]
    g1 = norm1_g[0].reshape(1, D_MODEL)
    w_in = ab_w_in[0].astype(BF16)
    qn = ab_q_norm[0].reshape(1, HEAD_DIM)
    kn = ab_k_norm[0].reshape(1, HEAD_DIM)
    cw = ab_conv_w[0]
    w_out = ab_w_out[0].astype(BF16)

    zc_p, q_p, kv_p = _in_proj0(xp, g1, m0, ctx_row, w_in, qn, kn, None, F32, tm_in)
    zc_s, q_s, kv_s = _in_proj0(xs, g1, m0, lat_row(tm_in), w_in, qn, kn, rope_tabs, BF16, tm_in)
    new_attn_k = kv_p[:, :KV_W].reshape(BATCH, 1, SEQ, GQA_KV_HEADS, HEAD_DIM)
    new_attn_v = kv_p[:, KV_W:].reshape(BATCH, 1, SEQ, GQA_KV_HEADS, HEAD_DIM)

    mix_p = _attn0_ctx(zc_p, q_p, kv_p, cw)
    lat3 = lambda a: a.reshape(DEC_BATCH, DEC_SEQ, a.shape[-1])
    mix_s = _attn0_lat(lat3(zc_s), lat3(q_s), lat3(kv_s),
                       cache_attn_k[:, 0].reshape(DEC_BATCH, PAST_LEN, KV_W),
                       cache_attn_v[:, 0].reshape(DEC_BATCH, PAST_LEN, KV_W), cw, 256)
    mix_s = mix_s.reshape(DEC_BATCH * DEC_SEQ, D_MODEL)

    xp = _out_proj(mix_p, w_out, xp, m0, ctx_row, tm_out, 1024)
    xs = _out_proj(mix_s, w_out, xs, m0, lat_row(tm_out), tm_out, 1024)

    g2 = norm2_g[0].reshape(1, D_MODEL)
    w1 = mlp_w1[0].astype(BF16)
    w2 = mlp_w2[0].astype(BF16)
    xp = _mlp(xp, g2, m0, ctx_row, w1, w2, fg, False, tm_mlp, tf)
    xs = _mlp(xs, g2, m0, lat_row(tm_mlp), w1, w2, fg, False, tm_mlp, tf)

    m1 = mods[1]
    g1 = norm1_g[1].reshape(1, D_MODEL)
    w_qkv = na_w_qkv[0].astype(BF16)
    w_out = na_w_out[0].astype(BF16)

    q_p, k_p, v_p = _in_proj1(xp, g1, m1, ctx_row, w_qkv, F32, tm_in, 512)
    q_s, k_s, v_s = _in_proj1(xs, g1, m1, lat_row(tm_in), w_qkv, BF16, tm_in, 512)
    new_na_k = k_p.reshape(BATCH, 1, SEQ, NA_HEADS, HEAD_DIM)
    new_na_v = v_p.reshape(BATCH, 1, SEQ, NA_HEADS, HEAD_DIM)

    mix_p = _attn1_ctx(q_p, k_p, v_p)
    mix_s = _attn1_lat(lat3(q_s), lat3(k_s), lat3(v_s),
                       cache_na_k[:, 0].reshape(DEC_BATCH, PAST_LEN, D_MODEL),
                       cache_na_v[:, 0].reshape(DEC_BATCH, PAST_LEN, D_MODEL),
                       _na_bias(na_rel_bias[0]))
    mix_s = mix_s.reshape(DEC_BATCH * DEC_SEQ, D_MODEL)

    xp = _out_proj(mix_p, w_out, xp, m1, ctx_row, tm_out, 1024)
    xs = _out_proj(mix_s, w_out, xs, m1, lat_row(tm_out), tm_out, 1024)

    g2 = norm2_g[1].reshape(1, D_MODEL)
    w1 = mlp_w1[1].astype(BF16)
    w2 = mlp_w2[1].astype(BF16)
    yp = _mlp(xp, g2, m1, ctx_row, w1, w2, fg, True, tm_mlp, tf)
    ys = _mlp(xs, g2, m1, lat_row(tm_mlp), w1, w2, fg, True, tm_mlp, tf)

    return (yp.reshape(BATCH, SEQ, D_MODEL), ys.reshape(DEC_BATCH, DEC_SEQ, D_MODEL),
            new_attn_k, new_attn_v, new_na_k, new_na_v)
```

```python
import functools

import numpy as np
import jax
import jax.numpy as jnp
from jax import lax
from jax.experimental import pallas as pl
from jax.experimental.pallas import tpu as pltpu

D_MODEL = 2048
BATCH = 32
SEQ = 256
DEC_BATCH = 2
DEC_SEQ = 2048
PAST_LEN = 512
GRID_W = 64
GRID_H = DEC_SEQ // GRID_W
HEAD_DIM = 128
CONV_WIDTH = D_MODEL // 2
GQA_HEADS = 8
GQA_KV_HEADS = 2
GQA_GROUP = GQA_HEADS // GQA_KV_HEADS
NA_HEADS = 16
NA_WIN_ROWS = 8
NA_WIN_COLS = 16
D_FF = 4 * D_MODEL
ROPE_THETA = 10000.0
NORM_EPS = 1e-6
AB_IN = 3 * CONV_WIDTH + (GQA_HEADS + 2 * GQA_KV_HEADS) * HEAD_DIM
KV_W = GQA_KV_HEADS * HEAD_DIM
ATTN_SCALE = HEAD_DIM ** -0.5

NA_QROWS = 4
NA_SLAB_ROWS = NA_QROWS + NA_WIN_ROWS
NA_QBLK = NA_QROWS * GRID_W
NA_SLAB = NA_SLAB_ROWS * GRID_W
NA_NBLK = GRID_H // NA_QROWS
MASK_NEG = -1e30

V7X_VMEM_LIMIT = 56 * 1024 * 1024

BF16 = jnp.bfloat16
F32 = jnp.float32


def _cparams(n_axes):
    return pltpu.CompilerParams(
        dimension_semantics=("parallel",) + ("arbitrary",) * (n_axes - 1),
        vmem_limit_bytes=V7X_VMEM_LIMIT)


def _nt_dot(a, b):
    return lax.dot_general(a, b, (((1,), (1,)), ((), ())), preferred_element_type=F32)


def _ada_norm(x, g, shift, scale):
    ms = jnp.mean(x * x, axis=-1, keepdims=True)
    y = x * lax.rsqrt(ms + NORM_EPS)
    return (y * g) * (1.0 + scale) + shift


def _head_norm(x, g):
    ms = jnp.mean(x * x, axis=-1, keepdims=True)
    return x * lax.rsqrt(ms + NORM_EPS) * g


def _rope(x, cos, sin_lo, sin_hi):
    return x * cos + pltpu.roll(x, HEAD_DIM - 32, 1) * sin_lo + pltpu.roll(x, 32, 1) * sin_hi


def _softmax_pv(scores, values):
    m = scores[0].max(axis=-1, keepdims=True)
    for s in scores[1:]:
        m = jnp.maximum(m, s.max(axis=-1, keepdims=True))
    l = None
    o = None
    for s, v in zip(scores, values):
        p = jnp.exp(s - m)
        ls = jnp.sum(p, axis=-1, keepdims=True)
        os_ = jnp.dot(p.astype(BF16), v, preferred_element_type=F32)
        l = ls if l is None else l + ls
        o = os_ if o is None else o + os_
    return o / l


def _mod_kernel(c_ref, w_ref, b_ref, o_ref):
    c = c_ref[...]
    s = (c * jax.nn.sigmoid(c)).astype(BF16)
    o_ref[...] = jnp.dot(s, w_ref[...].astype(BF16), preferred_element_type=F32) + b_ref[...]


def _modulation(cond, mod_w, mod_b):
    depth, d, n = mod_w.shape
    tn = 1024
    rows = cond.shape[0]
    return pl.pallas_call(
        _mod_kernel,
        grid=(depth, n // tn),
        in_specs=[
            pl.BlockSpec((rows, d), lambda l, j: (0, 0)),
            pl.BlockSpec((None, d, tn), lambda l, j: (l, 0, j)),
            pl.BlockSpec((None, 1, tn), lambda l, j: (l, 0, j)),
        ],
        out_specs=pl.BlockSpec((None, rows, tn), lambda l, j: (l, 0, j)),
        out_shape=jax.ShapeDtypeStruct((depth, rows, n), F32),
        compiler_params=_cparams(2),
        name="modulation",
    )(cond, mod_w, mod_b.reshape(depth, 1, n))


def _mod_spec(which, row_of_tile):
    return pl.BlockSpec((None, None, 1, D_MODEL), lambda i, j: (row_of_tile(i), which, 0, 0))


Q_W = GQA_HEADS * HEAD_DIM
IN0_TN = Q_W + 2 * KV_W


def _in0_kernel(x_ref, g_ref, shift_ref, scale_ref, w_ref, qn_ref, kn_ref, *rest, rope):
    if rope:
        cos_ref, slo_ref, shi_ref, zc_ref, q_ref, kv_ref, h_scr = rest
    else:
        zc_ref, q_ref, kv_ref, h_scr = rest
    j = pl.program_id(1)
    n_zc = 3 * CONV_WIDTH // IN0_TN

    @pl.when(j == 0)
    def _():
        h_scr[...] = _ada_norm(x_ref[...], g_ref[...], shift_ref[...], scale_ref[...]).astype(BF16)

    acc = jnp.dot(h_scr[...], w_ref[...], preferred_element_type=F32)

    def normed(blk, gain):
        y = _head_norm(blk, gain)
        if rope:
            y = _rope(y, cos_ref[...], slo_ref[...], shi_ref[...])
        return y

    @pl.when(j < n_zc)
    def _():
        zc_ref[...] = acc.astype(zc_ref.dtype)

    @pl.when(j == n_zc)
    def _():
        for hh in range(GQA_HEADS):
            sl = slice(hh * HEAD_DIM, (hh + 1) * HEAD_DIM)
            q_ref[:, sl] = (normed(acc[:, sl], qn_ref[...]) * ATTN_SCALE).astype(q_ref.dtype)
        for hh in range(GQA_KV_HEADS):
            src = slice(Q_W + hh * HEAD_DIM, Q_W + (hh + 1) * HEAD_DIM)
            kv_ref[:, hh * HEAD_DIM:(hh + 1) * HEAD_DIM] = normed(acc[:, src], kn_ref[...]).astype(kv_ref.dtype)
        kv_ref[:, KV_W:] = acc[:, Q_W + KV_W:].astype(kv_ref.dtype)


def _in_proj0(x, norm_g, mods, row_of_tile, w, qn, kn, rope_tabs, kv_dtype, tm):
    t = x.shape[0]
    tn = IN0_TN
    n_zc = 3 * CONV_WIDTH // tn
    rope = rope_tabs is not None
    vec = pl.BlockSpec((1, HEAD_DIM), lambda i, j: (0, 0))
    in_specs = [
        pl.BlockSpec((tm, D_MODEL), lambda i, j: (i, 0)),
        pl.BlockSpec((1, D_MODEL), lambda i, j: (0, 0)),
        _mod_spec(0, row_of_tile),
        _mod_spec(1, row_of_tile),
        pl.BlockSpec((D_MODEL, tn), lambda i, j: (0, j)),
        vec, vec,
    ]
    args = [x, norm_g, mods, mods, w, qn, kn]
    if rope:
        nt = DEC_SEQ // tm
        tab = pl.BlockSpec((tm, HEAD_DIM), lambda i, j: (i % nt, 0))
        in_specs += [tab, tab, tab]
        args += list(rope_tabs)
    return pl.pallas_call(
        functools.partial(_in0_kernel, rope=rope),
        grid=(t // tm, AB_IN // tn),
        in_specs=in_specs,
        out_specs=[
            pl.BlockSpec((tm, tn), lambda i, j: (i, jnp.minimum(j, n_zc - 1))),
            pl.BlockSpec((tm, Q_W), lambda i, j: (i, 0)),
            pl.BlockSpec((tm, 2 * KV_W), lambda i, j: (i, 0)),
        ],
        out_shape=[
            jax.ShapeDtypeStruct((t, 3 * CONV_WIDTH), BF16),
            jax.ShapeDtypeStruct((t, GQA_HEADS * HEAD_DIM), BF16),
            jax.ShapeDtypeStruct((t, 2 * KV_W), kv_dtype),
        ],
        scratch_shapes=[pltpu.VMEM((tm, D_MODEL), BF16)],
        compiler_params=_cparams(2),
        name="in_proj0",
    )(*args)


def _in1_kernel(x_ref, g_ref, shift_ref, scale_ref, w_ref, q_ref, k_ref, v_ref, h_scr, *, nb):
    j = pl.program_id(1)

    @pl.when(j == 0)
    def _():
        h_scr[...] = _ada_norm(x_ref[...], g_ref[...], shift_ref[...], scale_ref[...]).astype(BF16)

    acc = jnp.dot(h_scr[...], w_ref[...], preferred_element_type=F32)

    @pl.when(j < nb)
    def _():
        q_ref[...] = (acc * ATTN_SCALE).astype(q_ref.dtype)

    @pl.when((j >= nb) & (j < 2 * nb))
    def _():
        k_ref[...] = acc.astype(k_ref.dtype)

    @pl.when(j >= 2 * nb)
    def _():
        v_ref[...] = acc.astype(v_ref.dtype)


def _in_proj1(x, norm_g, mods, row_of_tile, w, kv_dtype, tm, tn):
    t = x.shape[0]
    nb = D_MODEL // tn
    return pl.pallas_call(
        functools.partial(_in1_kernel, nb=nb),
        grid=(t // tm, 3 * nb),
        in_specs=[
            pl.BlockSpec((tm, D_MODEL), lambda i, j: (i, 0)),
            pl.BlockSpec((1, D_MODEL), lambda i, j: (0, 0)),
            _mod_spec(0, row_of_tile),
            _mod_spec(1, row_of_tile),
            pl.BlockSpec((D_MODEL, tn), lambda i, j: (0, j)),
        ],
        out_specs=[
            pl.BlockSpec((tm, tn), lambda i, j: (i, jnp.minimum(j, nb - 1))),
            pl.BlockSpec((tm, tn), lambda i, j: (i, jnp.clip(j - nb, 0, nb - 1))),
            pl.BlockSpec((tm, tn), lambda i, j: (i, jnp.maximum(j - 2 * nb, 0))),
        ],
        out_shape=[
            jax.ShapeDtypeStruct((t, D_MODEL), BF16),
            jax.ShapeDtypeStruct((t, D_MODEL), kv_dtype),
            jax.ShapeDtypeStruct((t, D_MODEL), kv_dtype),
        ],
        scratch_shapes=[pltpu.VMEM((tm, D_MODEL), BF16)],
        compiler_params=_cparams(2),
        name="in_proj1",
    )(x, norm_g, mods, mods, w)


def _gated_conv(zc, prev_row, next_row, cw):
    c = CONV_WIDTH
    s = zc.shape[0]
    gb = zc[:, 0:c].astype(F32)
    u = zc[:, c:2 * c].astype(F32) * zc[:, 2 * c:3 * c].astype(F32)
    row = lax.broadcasted_iota(jnp.int32, u.shape, 0)
    prev = jnp.where(row == 0, prev_row, pltpu.roll(u, 1, 0))
    nxt = jnp.where(row == s - 1, next_row, pltpu.roll(u, s - 1, 0))
    return gb * (prev * cw[0:1] + u * cw[1:2] + nxt * cw[2:3])


def _attn0_ctx_kernel(zc_ref, q_ref, kv_ref, cw_ref, o_ref):
    zero = jnp.zeros((1, CONV_WIDTH), F32)
    a = _gated_conv(zc_ref[...], zero, zero, cw_ref[...])
    o_ref[:, 0:CONV_WIDTH] = a.astype(o_ref.dtype)
    for g in range(GQA_KV_HEADS):
        k = kv_ref[:, g * HEAD_DIM:(g + 1) * HEAD_DIM].astype(BF16)
        v = kv_ref[:, KV_W + g * HEAD_DIM:KV_W + (g + 1) * HEAD_DIM].astype(BF16)
        for hh in range(GQA_GROUP):
            h = g * GQA_GROUP + hh
            q = q_ref[:, h * HEAD_DIM:(h + 1) * HEAD_DIM]
            o = _softmax_pv([_nt_dot(q, k)], [v])
            o_ref[:, CONV_WIDTH + h * HEAD_DIM:CONV_WIDTH + (h + 1) * HEAD_DIM] = o.astype(o_ref.dtype)


def _attn0_ctx(zc, q, kv, cw):
    t = zc.shape[0]
    return pl.pallas_call(
        _attn0_ctx_kernel,
        grid=(t // SEQ,),
        in_specs=[
            pl.BlockSpec((SEQ, 3 * CONV_WIDTH), lambda b: (b, 0)),
            pl.BlockSpec((SEQ, GQA_HEADS * HEAD_DIM), lambda b: (b, 0)),
            pl.BlockSpec((SEQ, 2 * KV_W), lambda b: (b, 0)),
            pl.BlockSpec((3, CONV_WIDTH), lambda b: (0, 0)),
        ],
        out_specs=pl.BlockSpec((SEQ, D_MODEL), lambda b: (b, 0)),
        out_shape=jax.ShapeDtypeStruct((t, D_MODEL), BF16),
        compiler_params=_cparams(1),
        name="mixer0_ctx",
    )(zc, q, kv, cw)


_HALO = 16


def _attn0_lat_kernel(zc_ref, zp_ref, zn_ref, q_ref, kv_ref, ck_ref, cv_ref, cw_ref, o_ref, *, nqb):
    qb = pl.program_id(1)
    c = CONV_WIDTH
    up = zp_ref[:, c:2 * c].astype(F32) * zp_ref[:, 2 * c:3 * c].astype(F32)
    un = zn_ref[:, c:2 * c].astype(F32) * zn_ref[:, 2 * c:3 * c].astype(F32)
    up = jnp.where(qb == 0, 0.0, up[_HALO - 1:_HALO, :])
    un = jnp.where(qb == nqb - 1, 0.0, un[0:1, :])
    a = _gated_conv(zc_ref[...], up, un, cw_ref[...])
    o_ref[:, 0:c] = a.astype(o_ref.dtype)
    for g in range(GQA_KV_HEADS):
        ks = slice(g * HEAD_DIM, (g + 1) * HEAD_DIM)
        vs = slice(KV_W + g * HEAD_DIM, KV_W + (g + 1) * HEAD_DIM)
        k = kv_ref[:, ks]
        v = kv_ref[:, vs]
        ck = ck_ref[:, ks].astype(BF16)
        cv = cv_ref[:, ks].astype(BF16)
        for hh in range(GQA_GROUP):
            h = g * GQA_GROUP + hh
            q = q_ref[:, h * HEAD_DIM:(h + 1) * HEAD_DIM]
            o = _softmax_pv([_nt_dot(q, ck), _nt_dot(q, k)], [cv, v])
            o_ref[:, c + h * HEAD_DIM:c + (h + 1) * HEAD_DIM] = o.astype(o_ref.dtype)


def _attn0_lat(zc, q, kv, ck, cv, cw, tq):
    nb, t, _ = zc.shape
    nqb = t // tq
    hb = tq // _HALO
    return pl.pallas_call(
        functools.partial(_attn0_lat_kernel, nqb=nqb),
        grid=(nb, nqb),
        in_specs=[
            pl.BlockSpec((None, tq, 3 * CONV_WIDTH), lambda b, i: (b, i, 0)),
            pl.BlockSpec((None, _HALO, 3 * CONV_WIDTH), lambda b, i: (b, jnp.maximum(i * hb - 1, 0), 0)),
            pl.BlockSpec((None, _HALO, 3 * CONV_WIDTH),
                         lambda b, i: (b, jnp.minimum((i + 1) * hb, t // _HALO - 1), 0)),
            pl.BlockSpec((None, tq, GQA_HEADS * HEAD_DIM), lambda b, i: (b, i, 0)),
            pl.BlockSpec((None, t, 2 * KV_W), lambda b, i: (b, 0, 0)),
            pl.BlockSpec((None, PAST_LEN, KV_W), lambda b, i: (b, 0, 0)),
            pl.BlockSpec((None, PAST_LEN, KV_W), lambda b, i: (b, 0, 0)),
            pl.BlockSpec((3, CONV_WIDTH), lambda b, i: (0, 0)),
        ],
        out_specs=pl.BlockSpec((None, tq, D_MODEL), lambda b, i: (b, i, 0)),
        out_shape=jax.ShapeDtypeStruct((nb, t, D_MODEL), BF16),
        compiler_params=_cparams(2),
        name="mixer0_lat",
    )(zc, zc, zc, q, kv, ck, cv, cw)


def _attn1_ctx_kernel(q_ref, k_ref, v_ref, o_ref):
    for h in range(NA_HEADS):
        sl = slice(h * HEAD_DIM, (h + 1) * HEAD_DIM)
        k = k_ref[:, sl].astype(BF16)
        v = v_ref[:, sl].astype(BF16)
        o = _softmax_pv([_nt_dot(q_ref[:, sl], k)], [v])
        o_ref[:, sl] = o.astype(o_ref.dtype)


def _attn1_ctx(q, k, v):
    t = q.shape[0]
    blk = pl.BlockSpec((SEQ, D_MODEL), lambda b: (b, 0))
    return pl.pallas_call(
        _attn1_ctx_kernel,
        grid=(t // SEQ,),
        in_specs=[blk, blk, blk],
        out_specs=blk,
        out_shape=jax.ShapeDtypeStruct((t, D_MODEL), BF16),
        compiler_params=_cparams(1),
        name="mixer1_ctx",
    )(q, k, v)


def _na_slab_row(blk):
    return min(max(NA_QROWS * blk - NA_WIN_ROWS // 2, 0), GRID_H - NA_SLAB_ROWS)


NA_NROW = 2 * NA_WIN_ROWS - 1
NA_MASKED = NA_NROW
NA_FIRST_IN = NA_WIN_ROWS - 1 - NA_WIN_ROWS // 2
NA_LAST_IN = NA_FIRST_IN + NA_WIN_ROWS - 1
NA_PIECE_PAIRS = ([(m, m + 1) for m in range(NA_NROW - 1)]
                  + [(NA_MASKED, NA_FIRST_IN), (NA_LAST_IN, NA_MASKED)])


def _na_piece(blk, dr, jp):
    r = NA_QROWS * blk + dr
    ks = min(max(r - NA_WIN_ROWS // 2, 0), GRID_H - NA_WIN_ROWS)
    pair = []
    for j in (2 * jp, 2 * jp + 1):
        krow = _na_slab_row(blk) + j
        pair.append(krow - r + NA_WIN_ROWS - 1 if ks <= krow < ks + NA_WIN_ROWS else NA_MASKED)
    pair = tuple(pair)
    if pair == (NA_MASKED, NA_MASKED):
        return None
    return NA_PIECE_PAIRS.index(pair)


def _na_bias_pieces(rel_bias):
    pad = GRID_W - NA_WIN_COLS
    tp = jnp.pad(rel_bias, ((0, 0), (0, 0), (pad, pad)), constant_values=MASK_NEG)
    tcol = jnp.stack([tp[:, :, GRID_W - 1 - c:2 * GRID_W - 1 - c] for c in range(GRID_W)], axis=2)
    c = np.arange(GRID_W)[:, None]
    kc = np.arange(GRID_W)[None, :]
    cs = np.clip(c - NA_WIN_COLS // 2, 0, GRID_W - NA_WIN_COLS)
    tcol = jnp.where((kc >= cs) & (kc < cs + NA_WIN_COLS), tcol, MASK_NEG)
    masked = jnp.full((NA_HEADS, 1, GRID_W, GRID_W), MASK_NEG, F32)
    ext = jnp.concatenate([tcol, masked], axis=1)
    left = jnp.concatenate([ext[:, a:a + 1] for a, _ in NA_PIECE_PAIRS], axis=1)
    right = jnp.concatenate([ext[:, b:b + 1] for _, b in NA_PIECE_PAIRS], axis=1)
    return jnp.concatenate([left, right], axis=-1)


def _na_bias_block(pieces_ref, blk):
    rows = []
    for dr in range(NA_QROWS):
        cols = []
        for jp in range(NA_SLAB_ROWS // 2):
            m = _na_piece(blk, dr, jp)
            cols.append(jnp.full((GRID_W, 2 * GRID_W), MASK_NEG, F32) if m is None else pieces_ref[m])
        rows.append(jnp.concatenate(cols, axis=1))
    return jnp.concatenate(rows, axis=0)


def _attn1_lat_kernel(q_ref, k_ref, v_ref, ck_ref, cv_ref, pieces_ref, o_ref):
    ck = ck_ref[...].astype(BF16)
    cv = cv_ref[...].astype(BF16)
    for blk in range(NA_NBLK):
        rows = slice(blk * NA_QBLK, (blk + 1) * NA_QBLK)
        s0 = _na_slab_row(blk) * GRID_W
        slab = slice(s0, s0 + NA_SLAB)
        q = q_ref[rows, :]
        s_loc = _nt_dot(q, k_ref[slab, :]) + _na_bias_block(pieces_ref, blk)
        o = _softmax_pv([s_loc, _nt_dot(q, ck)], [v_ref[slab, :], cv])
        o_ref[rows, :] = o.astype(o_ref.dtype)


def _attn1_lat(q, k, v, ck, cv, pieces):
    nb, t, _ = q.shape
    head = pl.BlockSpec((None, t, HEAD_DIM), lambda b, h: (b, 0, h))
    ctx = pl.BlockSpec((None, PAST_LEN, HEAD_DIM), lambda b, h: (b, 0, h))
    return pl.pallas_call(
        _attn1_lat_kernel,
        grid=(nb, NA_HEADS),
        in_specs=[head, head, head, ctx, ctx,
                  pl.BlockSpec((None, len(NA_PIECE_PAIRS), GRID_W, 2 * GRID_W), lambda b, h: (h, 0, 0, 0))],
        out_specs=head,
        out_shape=jax.ShapeDtypeStruct((nb, t, D_MODEL), BF16),
        compiler_params=_cparams(2),
        name="mixer1_lat",
    )(q, k, v, ck, cv, pieces)


def _out_proj_kernel(m_ref, w_ref, x_ref, gate_ref, o_ref):
    acc = jnp.dot(m_ref[...], w_ref[...], preferred_element_type=F32)
    o_ref[...] = x_ref[...] + gate_ref[...] * acc


def _out_proj(mix, w, x, mods, row_of_tile, tm, tn):
    t, k = mix.shape
    return pl.pallas_call(
        _out_proj_kernel,
        grid=(t // tm, D_MODEL // tn),
        in_specs=[
            pl.BlockSpec((tm, k), lambda i, j: (i, 0)),
            pl.BlockSpec((k, tn), lambda i, j: (0, j)),
            pl.BlockSpec((tm, tn), lambda i, j: (i, j)),
            pl.BlockSpec((None, None, 1, tn), lambda i, j: (row_of_tile(i), 2, 0, j)),
        ],
        out_specs=pl.BlockSpec((tm, tn), lambda i, j: (i, j)),
        out_shape=jax.ShapeDtypeStruct((t, D_MODEL), F32),
        compiler_params=_cparams(2),
        name="out_proj",
    )(mix, w, x, mods)


def _mlp_kernel(x_ref, g_ref, shift_ref, scale_ref, gate_ref, w1_ref, w2_ref, fg_ref, o_ref, h_scr,
                *, nf, final):
    f = pl.program_id(1)

    @pl.when(f == 0)
    def _():
        h_scr[...] = _ada_norm(x_ref[...], g_ref[...], shift_ref[...], scale_ref[...]).astype(BF16)

    u = jnp.dot(h_scr[...], w1_ref[...], preferred_element_type=F32)
    u = jnp.square(jnp.maximum(u, 0.0)).astype(BF16)
    part = jnp.dot(u, w2_ref[...], preferred_element_type=F32)

    @pl.when(f == 0)
    def _():
        o_ref[...] = part

    @pl.when(f > 0)
    def _():
        o_ref[...] += part

    @pl.when(f == nf - 1)
    def _():
        y = x_ref[...] + gate_ref[...] * o_ref[...]
        if final:
            ms = jnp.mean(y * y, axis=-1, keepdims=True)
            y = y * lax.rsqrt(ms + NORM_EPS) * fg_ref[...]
        o_ref[...] = y


def _mlp(x, norm_g, mods, row_of_tile, w1, w2, final_g, final, tm, tf):
    t = x.shape[0]
    nf = D_FF // tf

    def mod(which):
        return pl.BlockSpec((None, None, 1, D_MODEL), lambda i, f: (row_of_tile(i), which, 0, 0))

    return pl.pallas_call(
        functools.partial(_mlp_kernel, nf=nf, final=final),
        grid=(t // tm, nf),
        in_specs=[
            pl.BlockSpec((tm, D_MODEL), lambda i, f: (i, 0)),
            pl.BlockSpec((1, D_MODEL), lambda i, f: (0, 0)),
            mod(3), mod(4), mod(5),
            pl.BlockSpec((D_MODEL, tf), lambda i, f: (0, f)),
            pl.BlockSpec((tf, D_MODEL), lambda i, f: (f, 0)),
            pl.BlockSpec((1, D_MODEL), lambda i, f: (0, 0)),
        ],
        out_specs=pl.BlockSpec((tm, D_MODEL), lambda i, f: (i, 0)),
        out_shape=jax.ShapeDtypeStruct((t, D_MODEL), F32),
        scratch_shapes=[pltpu.VMEM((tm, D_MODEL), BF16)],
        compiler_params=_cparams(2),
        name="mlp",
    )(x, norm_g, mods, mods, mods, w1, w2, final_g)


def _rope_tables():
    t = np.arange(DEC_SEQ)
    half = HEAD_DIM // 4
    inv = ROPE_THETA ** (-np.arange(half, dtype=np.float32) / half)
    ang_r = (t // GRID_W).astype(np.float32)[:, None] * inv
    ang_c = (t % GRID_W).astype(np.float32)[:, None] * inv
    zero = np.zeros_like(ang_r)
    cos = np.concatenate([np.cos(ang_r)] * 2 + [np.cos(ang_c)] * 2, axis=-1)
    sin_lo = np.concatenate([-np.sin(ang_r), zero, -np.sin(ang_c), zero], axis=-1)
    sin_hi = np.concatenate([zero, np.sin(ang_r), zero, np.sin(ang_c)], axis=-1)
    return tuple(jnp.asarray(a, F32) for a in (cos, sin_lo, sin_hi))


def kernel(x_prompt, x_sample, cache_attn_k, cache_attn_v, cache_na_k, cache_na_v, c, c_ctx, mod_w, mod_b,
           norm1_g, norm2_g, ab_w_in, ab_conv_w, ab_q_norm, ab_k_norm, ab_w_out, na_w_qkv, na_rel_bias,
           na_w_out, mlp_w1, mlp_w2, final_norm_g):
    n_ctx = BATCH * SEQ
    xp = x_prompt.reshape(n_ctx, D_MODEL)
    xs = x_sample.reshape(DEC_BATCH * DEC_SEQ, D_MODEL)

    cond = jnp.concatenate([c_ctx[None, :], c, jnp.zeros((8 - 1 - DEC_BATCH, D_MODEL), F32)], axis=0)
    mods = _modulation(cond, mod_w, mod_b).reshape(2, 8, 6, 1, D_MODEL)

    tm_in, tm_out, tm_mlp, tf = 512, 1024, 512, 1024

    def ctx_row(i):
        return 0

    def lat_row(tm):
        return lambda i: 1 + i // (DEC_SEQ // tm)

    rope_tabs = _rope_tables()
    fg = final_norm_g.reshape(1, D_MODEL)

    m0 = mods[0]
    g1 = norm1_g[0].reshape(1, D_MODEL)
    w_in = ab_w_in[0].astype(BF16)
    qn = ab_q_norm[0].reshape(1, HEAD_DIM)
    kn = ab_k_norm[0].reshape(1, HEAD_DIM)
    cw = ab_conv_w[0]
    w_out = ab_w_out[0].astype(BF16)

    zc_p, q_p, kv_p = _in_proj0(xp, g1, m0, ctx_row, w_in, qn, kn, None, F32, tm_in)
    zc_s, q_s, kv_s = _in_proj0(xs, g1, m0, lat_row(tm_in), w_in, qn, kn, rope_tabs, BF16, tm_in)
    new_attn_k = kv_p[:, :KV_W].reshape(BATCH, 1, SEQ, GQA_KV_HEADS, HEAD_DIM)
    new_attn_v = kv_p[:, KV_W:].reshape(BATCH, 1, SEQ, GQA_KV_HEADS, HEAD_DIM)

    mix_p = _attn0_ctx(zc_p, q_p, kv_p, cw)
    lat3 = lambda a: a.reshape(DEC_BATCH, DEC_SEQ, a.shape[-1])
    mix_s = _attn0_lat(lat3(zc_s), lat3(q_s), lat3(kv_s),
                       cache_attn_k[:, 0].reshape(DEC_BATCH, PAST_LEN, KV_W),
                       cache_attn_v[:, 0].reshape(DEC_BATCH, PAST_LEN, KV_W), cw, 256)
    mix_s = mix_s.reshape(DEC_BATCH * DEC_SEQ, D_MODEL)

    xp = _out_proj(mix_p, w_out, xp, m0, ctx_row, tm_out, 1024)
    xs = _out_proj(mix_s, w_out, xs, m0, lat_row(tm_out), tm_out, 1024)

    g2 = norm2_g[0].reshape(1, D_MODEL)
    w1 = mlp_w1[0].astype(BF16)
    w2 = mlp_w2[0].astype(BF16)
    xp = _mlp(xp, g2, m0, ctx_row, w1, w2, fg, False, tm_mlp, tf)
    xs = _mlp(xs, g2, m0, lat_row(tm_mlp), w1, w2, fg, False, tm_mlp, tf)

    m1 = mods[1]
    g1 = norm1_g[1].reshape(1, D_MODEL)
    w_qkv = na_w_qkv[0].astype(BF16)
    w_out = na_w_out[0].astype(BF16)

    q_p, k_p, v_p = _in_proj1(xp, g1, m1, ctx_row, w_qkv, F32, tm_in, 1024)
    q_s, k_s, v_s = _in_proj1(xs, g1, m1, lat_row(tm_in), w_qkv, BF16, tm_in, 1024)
    new_na_k = k_p.reshape(BATCH, 1, SEQ, NA_HEADS, HEAD_DIM)
    new_na_v = v_p.reshape(BATCH, 1, SEQ, NA_HEADS, HEAD_DIM)

    mix_p = _attn1_ctx(q_p, k_p, v_p)
    mix_s = _attn1_lat(lat3(q_s), lat3(k_s), lat3(v_s),
                       cache_na_k[:, 0].reshape(DEC_BATCH, PAST_LEN, D_MODEL),
                       cache_na_v[:, 0].reshape(DEC_BATCH, PAST_LEN, D_MODEL),
                       _na_bias_pieces(na_rel_bias[0]))
    mix_s = mix_s.reshape(DEC_BATCH * DEC_SEQ, D_MODEL)

    xp = _out_proj(mix_p, w_out, xp, m1, ctx_row, tm_out, 1024)
    xs = _out_proj(mix_s, w_out, xs, m1, lat_row(tm_out), tm_out, 1024)

    g2 = norm2_g[1].reshape(1, D_MODEL)
    w1 = mlp_w1[1].astype(BF16)
    w2 = mlp_w2[1].astype(BF16)
    yp = _mlp(xp, g2, m1, ctx_row, w1, w2, fg, True, tm_mlp, tf)
    ys = _mlp(xs, g2, m1, lat_row(tm_mlp), w1, w2, fg, True, tm_mlp, tf)

    return (yp.reshape(BATCH, SEQ, D_MODEL), ys.reshape(DEC_BATCH, DEC_SEQ, D_MODEL),
            new_attn_k, new_attn_v, new_na_k, new_na_v)
```

```python
import functools

import numpy as np
import jax
import jax.numpy as jnp
from jax import lax
from jax.experimental import pallas as pl
from jax.experimental.pallas import tpu as pltpu

D_MODEL = 2048
BATCH = 32
SEQ = 256
DEC_BATCH = 2
DEC_SEQ = 2048
PAST_LEN = 512
GRID_W = 64
GRID_H = DEC_SEQ // GRID_W
HEAD_DIM = 128
CONV_WIDTH = D_MODEL // 2
GQA_HEADS = 8
GQA_KV_HEADS = 2
GQA_GROUP = GQA_HEADS // GQA_KV_HEADS
NA_HEADS = 16
NA_WIN_ROWS = 8
NA_WIN_COLS = 16
D_FF = 4 * D_MODEL
ROPE_THETA = 10000.0
NORM_EPS = 1e-6
AB_IN = 3 * CONV_WIDTH + (GQA_HEADS + 2 * GQA_KV_HEADS) * HEAD_DIM
KV_W = GQA_KV_HEADS * HEAD_DIM
ATTN_SCALE = HEAD_DIM ** -0.5

NA_QROWS = 4
NA_SLAB_ROWS = NA_QROWS + NA_WIN_ROWS
NA_QBLK = NA_QROWS * GRID_W
NA_SLAB = NA_SLAB_ROWS * GRID_W
NA_NBLK = GRID_H // NA_QROWS
MASK_NEG = -1e30

V7X_VMEM_LIMIT = 56 * 1024 * 1024

BF16 = jnp.bfloat16
F32 = jnp.float32


def _cparams(n_axes):
    return pltpu.CompilerParams(
        dimension_semantics=("parallel",) + ("arbitrary",) * (n_axes - 1),
        vmem_limit_bytes=V7X_VMEM_LIMIT)


def _nt_dot(a, b):
    return lax.dot_general(a, b, (((1,), (1,)), ((), ())), preferred_element_type=F32)


NORM_ROWS = 16


def _ada_norm_rows(h_ref, x_ref, g_ref, shift_ref, scale_ref):
    gain = g_ref[...] * (1.0 + scale_ref[...])
    shift = shift_ref[...]

    def body(c, carry):
        rows = pl.ds(pl.multiple_of(c * NORM_ROWS, NORM_ROWS), NORM_ROWS)
        x = x_ref[rows, :]
        ms = jnp.mean(x * x, axis=-1, keepdims=True)
        h_ref[rows, :] = (x * lax.rsqrt(ms + NORM_EPS) * gain + shift).astype(h_ref.dtype)
        return carry

    lax.fori_loop(0, x_ref.shape[0] // NORM_ROWS, body, 0, unroll=8)


def _head_norm(x, g):
    ms = jnp.mean(x * x, axis=-1, keepdims=True)
    return x * lax.rsqrt(ms + NORM_EPS) * g


def _rope(x, cos, sin_lo, sin_hi):
    return x * cos + pltpu.roll(x, HEAD_DIM - 32, 1) * sin_lo + pltpu.roll(x, 32, 1) * sin_hi


def _softmax_pv(scores, values):
    m = scores[0].max(axis=-1, keepdims=True)
    for s in scores[1:]:
        m = jnp.maximum(m, s.max(axis=-1, keepdims=True))
    l = None
    o = None
    for s, v in zip(scores, values):
        p = jnp.exp(s - m)
        ls = jnp.sum(p, axis=-1, keepdims=True)
        os_ = jnp.dot(p.astype(BF16), v, preferred_element_type=F32)
        l = ls if l is None else l + ls
        o = os_ if o is None else o + os_
    return o / l


def _mod_kernel(c_ref, w_ref, b_ref, o_ref):
    c = c_ref[...]
    s = (c * jax.nn.sigmoid(c)).astype(BF16)
    o_ref[...] = jnp.dot(s, w_ref[...].astype(BF16), preferred_element_type=F32) + b_ref[...]


def _modulation(cond, mod_w, mod_b):
    depth, d, n = mod_w.shape
    tn = 1024
    rows = cond.shape[0]
    return pl.pallas_call(
        _mod_kernel,
        grid=(depth, n // tn),
        in_specs=[
            pl.BlockSpec((rows, d), lambda l, j: (0, 0)),
            pl.BlockSpec((None, d, tn), lambda l, j: (l, 0, j)),
            pl.BlockSpec((None, 1, tn), lambda l, j: (l, 0, j)),
        ],
        out_specs=pl.BlockSpec((None, rows, tn), lambda l, j: (l, 0, j)),
        out_shape=jax.ShapeDtypeStruct((depth, rows, n), F32),
        compiler_params=_cparams(2),
        name="modulation",
    )(cond, mod_w, mod_b.reshape(depth, 1, n))


def _mod_spec(which, row_of_tile):
    return pl.BlockSpec((None, None, 1, D_MODEL), lambda i, j: (row_of_tile(i), which, 0, 0))


Q_W = GQA_HEADS * HEAD_DIM
IN0_TN = Q_W + 2 * KV_W


def _in0_kernel(x_ref, g_ref, shift_ref, scale_ref, w_ref, qn_ref, kn_ref, *rest, rope):
    if rope:
        cos_ref, slo_ref, shi_ref, zc_ref, q_ref, kv_ref, h_scr = rest
    else:
        zc_ref, q_ref, kv_ref, h_scr = rest
    j = pl.program_id(1)
    n_zc = 3 * CONV_WIDTH // IN0_TN

    @pl.when(j == 0)
    def _():
        _ada_norm_rows(h_scr, x_ref, g_ref, shift_ref, scale_ref)

    acc = jnp.dot(h_scr[...], w_ref[...], preferred_element_type=F32)

    def normed(blk, gain):
        y = _head_norm(blk, gain)
        if rope:
            y = _rope(y, cos_ref[...], slo_ref[...], shi_ref[...])
        return y

    @pl.when(j < n_zc)
    def _():
        zc_ref[...] = acc.astype(zc_ref.dtype)

    @pl.when(j == n_zc)
    def _():
        for hh in range(GQA_HEADS):
            sl = slice(hh * HEAD_DIM, (hh + 1) * HEAD_DIM)
            q_ref[:, sl] = (normed(acc[:, sl], qn_ref[...]) * ATTN_SCALE).astype(q_ref.dtype)
        for hh in range(GQA_KV_HEADS):
            src = slice(Q_W + hh * HEAD_DIM, Q_W + (hh + 1) * HEAD_DIM)
            kv_ref[:, hh * HEAD_DIM:(hh + 1) * HEAD_DIM] = normed(acc[:, src], kn_ref[...]).astype(kv_ref.dtype)
        kv_ref[:, KV_W:] = acc[:, Q_W + KV_W:].astype(kv_ref.dtype)


def _in_proj0(x, norm_g, mods, row_of_tile, w, qn, kn, rope_tabs, kv_dtype, tm):
    t = x.shape[0]
    tn = IN0_TN
    n_zc = 3 * CONV_WIDTH // tn
    rope = rope_tabs is not None
    vec = pl.BlockSpec((1, HEAD_DIM), lambda i, j: (0, 0))
    in_specs = [
        pl.BlockSpec((tm, D_MODEL), lambda i, j: (i, 0)),
        pl.BlockSpec((1, D_MODEL), lambda i, j: (0, 0)),
        _mod_spec(0, row_of_tile),
        _mod_spec(1, row_of_tile),
        pl.BlockSpec((D_MODEL, tn), lambda i, j: (0, j)),
        vec, vec,
    ]
    args = [x, norm_g, mods, mods, w, qn, kn]
    if rope:
        nt = DEC_SEQ // tm
        tab = pl.BlockSpec((tm, HEAD_DIM), lambda i, j: (i % nt, 0))
        in_specs += [tab, tab, tab]
        args += list(rope_tabs)
    return pl.pallas_call(
        functools.partial(_in0_kernel, rope=rope),
        grid=(t // tm, AB_IN // tn),
        in_specs=in_specs,
        out_specs=[
            pl.BlockSpec((tm, tn), lambda i, j: (i, jnp.minimum(j, n_zc - 1))),
            pl.BlockSpec((tm, Q_W), lambda i, j: (i, 0)),
            pl.BlockSpec((tm, 2 * KV_W), lambda i, j: (i, 0)),
        ],
        out_shape=[
            jax.ShapeDtypeStruct((t, 3 * CONV_WIDTH), BF16),
            jax.ShapeDtypeStruct((t, GQA_HEADS * HEAD_DIM), BF16),
            jax.ShapeDtypeStruct((t, 2 * KV_W), kv_dtype),
        ],
        scratch_shapes=[pltpu.VMEM((tm, D_MODEL), BF16)],
        compiler_params=_cparams(2),
        name="in_proj0",
    )(*args)


def _in1_kernel(x_ref, g_ref, shift_ref, scale_ref, w_ref, o_ref, h_scr, *, nb):
    j = pl.program_id(1)

    @pl.when(j == 0)
    def _():
        _ada_norm_rows(h_scr, x_ref, g_ref, shift_ref, scale_ref)

    acc = jnp.dot(h_scr[...], w_ref[...], preferred_element_type=F32)
    o_ref[...] = (acc * jnp.where(j < nb, ATTN_SCALE, 1.0)).astype(o_ref.dtype)


def _in_proj1(x, norm_g, mods, row_of_tile, w, out_dtype, tm, tn):
    t = x.shape[0]
    nb = D_MODEL // tn
    return pl.pallas_call(
        functools.partial(_in1_kernel, nb=nb),
        grid=(t // tm, 3 * nb),
        in_specs=[
            pl.BlockSpec((tm, D_MODEL), lambda i, j: (i, 0)),
            pl.BlockSpec((1, D_MODEL), lambda i, j: (0, 0)),
            _mod_spec(0, row_of_tile),
            _mod_spec(1, row_of_tile),
            pl.BlockSpec((D_MODEL, tn), lambda i, j: (0, j)),
        ],
        out_specs=pl.BlockSpec((None, tm, tn), lambda i, j: (j // nb, i, j % nb)),
        out_shape=jax.ShapeDtypeStruct((3, t, D_MODEL), out_dtype),
        scratch_shapes=[pltpu.VMEM((tm, D_MODEL), BF16)],
        compiler_params=_cparams(2),
        name="in_proj1",
    )(x, norm_g, mods, mods, w)


def _gated_conv(zc, prev_row, next_row, cw):
    c = CONV_WIDTH
    s = zc.shape[0]
    gb = zc[:, 0:c].astype(F32)
    u = zc[:, c:2 * c].astype(F32) * zc[:, 2 * c:3 * c].astype(F32)
    row = lax.broadcasted_iota(jnp.int32, u.shape, 0)
    prev = jnp.where(row == 0, prev_row, pltpu.roll(u, 1, 0))
    nxt = jnp.where(row == s - 1, next_row, pltpu.roll(u, s - 1, 0))
    return gb * (prev * cw[0:1] + u * cw[1:2] + nxt * cw[2:3])


def _attn0_ctx_kernel(zc_ref, q_ref, kv_ref, cw_ref, o_ref):
    zero = jnp.zeros((1, CONV_WIDTH), F32)
    a = _gated_conv(zc_ref[...], zero, zero, cw_ref[...])
    o_ref[:, 0:CONV_WIDTH] = a.astype(o_ref.dtype)
    for g in range(GQA_KV_HEADS):
        k = kv_ref[:, g * HEAD_DIM:(g + 1) * HEAD_DIM].astype(BF16)
        v = kv_ref[:, KV_W + g * HEAD_DIM:KV_W + (g + 1) * HEAD_DIM].astype(BF16)
        for hh in range(GQA_GROUP):
            h = g * GQA_GROUP + hh
            q = q_ref[:, h * HEAD_DIM:(h + 1) * HEAD_DIM]
            o = _softmax_pv([_nt_dot(q, k)], [v])
            o_ref[:, CONV_WIDTH + h * HEAD_DIM:CONV_WIDTH + (h + 1) * HEAD_DIM] = o.astype(o_ref.dtype)


def _attn0_ctx(zc, q, kv, cw):
    t = zc.shape[0]
    return pl.pallas_call(
        _attn0_ctx_kernel,
        grid=(t // SEQ,),
        in_specs=[
            pl.BlockSpec((SEQ, 3 * CONV_WIDTH), lambda b: (b, 0)),
            pl.BlockSpec((SEQ, GQA_HEADS * HEAD_DIM), lambda b: (b, 0)),
            pl.BlockSpec((SEQ, 2 * KV_W), lambda b: (b, 0)),
            pl.BlockSpec((3, CONV_WIDTH), lambda b: (0, 0)),
        ],
        out_specs=pl.BlockSpec((SEQ, D_MODEL), lambda b: (b, 0)),
        out_shape=jax.ShapeDtypeStruct((t, D_MODEL), BF16),
        compiler_params=_cparams(1),
        name="mixer0_ctx",
    )(zc, q, kv, cw)


_HALO = 16


def _attn0_lat_kernel(zc_ref, zp_ref, zn_ref, q_ref, kv_ref, ck_ref, cv_ref, cw_ref, o_ref, *, nqb):
    qb = pl.program_id(1)
    c = CONV_WIDTH
    up = zp_ref[:, c:2 * c].astype(F32) * zp_ref[:, 2 * c:3 * c].astype(F32)
    un = zn_ref[:, c:2 * c].astype(F32) * zn_ref[:, 2 * c:3 * c].astype(F32)
    up = jnp.where(qb == 0, 0.0, up[_HALO - 1:_HALO, :])
    un = jnp.where(qb == nqb - 1, 0.0, un[0:1, :])
    a = _gated_conv(zc_ref[...], up, un, cw_ref[...])
    o_ref[:, 0:c] = a.astype(o_ref.dtype)
    for g in range(GQA_KV_HEADS):
        ks = slice(g * HEAD_DIM, (g + 1) * HEAD_DIM)
        vs = slice(KV_W + g * HEAD_DIM, KV_W + (g + 1) * HEAD_DIM)
        k = kv_ref[:, ks]
        v = kv_ref[:, vs]
        ck = ck_ref[:, ks].astype(BF16)
        cv = cv_ref[:, ks].astype(BF16)
        for hh in range(GQA_GROUP):
            h = g * GQA_GROUP + hh
            q = q_ref[:, h * HEAD_DIM:(h + 1) * HEAD_DIM]
            o = _softmax_pv([_nt_dot(q, ck), _nt_dot(q, k)], [cv, v])
            o_ref[:, c + h * HEAD_DIM:c + (h + 1) * HEAD_DIM] = o.astype(o_ref.dtype)


def _attn0_lat(zc, q, kv, ck, cv, cw, tq):
    nb, t, _ = zc.shape
    nqb = t // tq
    hb = tq // _HALO
    return pl.pallas_call(
        functools.partial(_attn0_lat_kernel, nqb=nqb),
        grid=(nb, nqb),
        in_specs=[
            pl.BlockSpec((None, tq, 3 * CONV_WIDTH), lambda b, i: (b, i, 0)),
            pl.BlockSpec((None, _HALO, 3 * CONV_WIDTH), lambda b, i: (b, jnp.maximum(i * hb - 1, 0), 0)),
            pl.BlockSpec((None, _HALO, 3 * CONV_WIDTH),
                         lambda b, i: (b, jnp.minimum((i + 1) * hb, t // _HALO - 1), 0)),
            pl.BlockSpec((None, tq, GQA_HEADS * HEAD_DIM), lambda b, i: (b, i, 0)),
            pl.BlockSpec((None, t, 2 * KV_W), lambda b, i: (b, 0, 0)),
            pl.BlockSpec((None, PAST_LEN, KV_W), lambda b, i: (b, 0, 0)),
            pl.BlockSpec((None, PAST_LEN, KV_W), lambda b, i: (b, 0, 0)),
            pl.BlockSpec((3, CONV_WIDTH), lambda b, i: (0, 0)),
        ],
        out_specs=pl.BlockSpec((None, tq, D_MODEL), lambda b, i: (b, i, 0)),
        out_shape=jax.ShapeDtypeStruct((nb, t, D_MODEL), BF16),
        compiler_params=_cparams(2),
        name="mixer0_lat",
    )(zc, zc, zc, q, kv, ck, cv, cw)


def _attn1_ctx_kernel(q_ref, k_ref, v_ref, o_ref):
    for h in range(NA_HEADS):
        sl = slice(h * HEAD_DIM, (h + 1) * HEAD_DIM)
        q = q_ref[:, sl].astype(BF16)
        k = k_ref[:, sl].astype(BF16)
        v = v_ref[:, sl].astype(BF16)
        o = _softmax_pv([_nt_dot(q, k)], [v])
        o_ref[:, sl] = o.astype(o_ref.dtype)


def _attn1_ctx(qkv):
    t = qkv.shape[1]
    part = lambda p: pl.BlockSpec((None, SEQ, D_MODEL), lambda b: (p, b, 0))
    return pl.pallas_call(
        _attn1_ctx_kernel,
        grid=(t // SEQ,),
        in_specs=[part(0), part(1), part(2)],
        out_specs=pl.BlockSpec((SEQ, D_MODEL), lambda b: (b, 0)),
        out_shape=jax.ShapeDtypeStruct((t, D_MODEL), BF16),
        compiler_params=_cparams(1),
        name="mixer1_ctx",
    )(qkv, qkv, qkv)


def _na_slab_row(blk):
    return min(max(NA_QROWS * blk - NA_WIN_ROWS // 2, 0), GRID_H - NA_SLAB_ROWS)


NA_NROW = 2 * NA_WIN_ROWS - 1
NA_MASKED = NA_NROW
NA_FIRST_IN = NA_WIN_ROWS - 1 - NA_WIN_ROWS // 2
NA_LAST_IN = NA_FIRST_IN + NA_WIN_ROWS - 1
NA_PIECE_PAIRS = ([(m, m + 1) for m in range(NA_NROW - 1)]
                  + [(NA_MASKED, NA_FIRST_IN), (NA_LAST_IN, NA_MASKED)])


def _na_piece(blk, dr, jp):
    r = NA_QROWS * blk + dr
    ks = min(max(r - NA_WIN_ROWS // 2, 0), GRID_H - NA_WIN_ROWS)
    pair = []
    for j in (2 * jp, 2 * jp + 1):
        krow = _na_slab_row(blk) + j
        pair.append(krow - r + NA_WIN_ROWS - 1 if ks <= krow < ks + NA_WIN_ROWS else NA_MASKED)
    pair = tuple(pair)
    if pair == (NA_MASKED, NA_MASKED):
        return None
    return NA_PIECE_PAIRS.index(pair)


def _na_bias_pieces(rel_bias):
    pad = GRID_W - NA_WIN_COLS
    period = 2 * GRID_W
    p = jnp.pad(rel_bias, ((0, 0), (0, 0), (pad, period - (2 * NA_WIN_COLS - 1) - pad)),
                constant_values=MASK_NEG)
    rep = jnp.tile(p, (1, 1, GRID_W))[..., :GRID_W * (period - 1)]
    tcol = rep.reshape(NA_HEADS, NA_NROW, GRID_W, period - 1)[..., GRID_W - 1:]
    c = np.arange(GRID_W)[:, None]
    kc = np.arange(GRID_W)[None, :]
    cs = np.clip(c - NA_WIN_COLS // 2, 0, GRID_W - NA_WIN_COLS)
    tcol = jnp.where((kc >= cs) & (kc < cs + NA_WIN_COLS), tcol, MASK_NEG)
    masked = jnp.full((NA_HEADS, 1, GRID_W, GRID_W), MASK_NEG, F32)
    ext = jnp.concatenate([tcol, masked], axis=1)
    one = lambda a: ext[:, a:a + 1]
    left = jnp.concatenate([ext[:, :NA_NROW - 1], one(NA_MASKED), one(NA_LAST_IN)], axis=1)
    right = jnp.concatenate([ext[:, 1:NA_NROW], one(NA_FIRST_IN), one(NA_MASKED)], axis=1)
    return jnp.concatenate([left, right], axis=-1)


def _na_bias_block(pieces_ref, blk):
    rows = []
    for dr in range(NA_QROWS):
        cols = []
        for jp in range(NA_SLAB_ROWS // 2):
            m = _na_piece(blk, dr, jp)
            cols.append(jnp.full((GRID_W, 2 * GRID_W), MASK_NEG, F32) if m is None else pieces_ref[m])
        rows.append(jnp.concatenate(cols, axis=1))
    return jnp.concatenate(rows, axis=0)


def _attn1_lat_kernel(q_ref, k_ref, v_ref, ck_ref, cv_ref, pieces_ref, o_ref):
    ck = ck_ref[...].astype(BF16)
    cv = cv_ref[...].astype(BF16)
    for blk in range(NA_NBLK):
        rows = slice(blk * NA_QBLK, (blk + 1) * NA_QBLK)
        s0 = _na_slab_row(blk) * GRID_W
        slab = slice(s0, s0 + NA_SLAB)
        q = q_ref[rows, :]
        s_loc = _nt_dot(q, k_ref[slab, :]) + _na_bias_block(pieces_ref, blk)
        o = _softmax_pv([s_loc, _nt_dot(q, ck)], [v_ref[slab, :], cv])
        o_ref[rows, :] = o.astype(o_ref.dtype)


def _attn1_lat(qkv, ck, cv, pieces):
    _, nb, t, _ = qkv.shape
    part = lambda p: pl.BlockSpec((None, None, t, HEAD_DIM), lambda b, h: (p, b, 0, h))
    head = pl.BlockSpec((None, t, HEAD_DIM), lambda b, h: (b, 0, h))
    ctx = pl.BlockSpec((None, PAST_LEN, HEAD_DIM), lambda b, h: (b, 0, h))
    return pl.pallas_call(
        _attn1_lat_kernel,
        grid=(nb, NA_HEADS),
        in_specs=[part(0), part(1), part(2), ctx, ctx,
                  pl.BlockSpec((None, len(NA_PIECE_PAIRS), GRID_W, 2 * GRID_W), lambda b, h: (h, 0, 0, 0))],
        out_specs=head,
        out_shape=jax.ShapeDtypeStruct((nb, t, D_MODEL), BF16),
        compiler_params=_cparams(2),
        name="mixer1_lat",
    )(qkv, qkv, qkv, ck, cv, pieces)


def _out_proj_kernel(m_ref, w_ref, x_ref, gate_ref, o_ref):
    acc = jnp.dot(m_ref[...], w_ref[...], preferred_element_type=F32)
    o_ref[...] = x_ref[...] + gate_ref[...] * acc


def _out_proj(mix, w, x, mods, row_of_tile, tm, tn):
    t, k = mix.shape
    return pl.pallas_call(
        _out_proj_kernel,
        grid=(t // tm, D_MODEL // tn),
        in_specs=[
            pl.BlockSpec((tm, k), lambda i, j: (i, 0)),
            pl.BlockSpec((k, tn), lambda i, j: (0, j)),
            pl.BlockSpec((tm, tn), lambda i, j: (i, j)),
            pl.BlockSpec((None, None, 1, tn), lambda i, j: (row_of_tile(i), 2, 0, j)),
        ],
        out_specs=pl.BlockSpec((tm, tn), lambda i, j: (i, j)),
        out_shape=jax.ShapeDtypeStruct((t, D_MODEL), F32),
        compiler_params=_cparams(2),
        name="out_proj",
    )(mix, w, x, mods)


def _mlp_kernel(x_ref, g_ref, shift_ref, scale_ref, gate_ref, w1_ref, w2_ref, fg_ref, o_ref, h_scr,
                *, nf, final):
    f = pl.program_id(1)

    @pl.when(f == 0)
    def _():
        _ada_norm_rows(h_scr, x_ref, g_ref, shift_ref, scale_ref)
        o_ref[...] = jnp.zeros_like(o_ref)

    u = jnp.dot(h_scr[...], w1_ref[...], preferred_element_type=F32)
    u = jnp.square(jnp.maximum(u, 0.0)).astype(BF16)
    o_ref[...] += jnp.dot(u, w2_ref[...], preferred_element_type=F32)

    @pl.when(f == nf - 1)
    def _():
        y = x_ref[...] + gate_ref[...] * o_ref[...]
        if final:
            ms = jnp.mean(y * y, axis=-1, keepdims=True)
            y = y * lax.rsqrt(ms + NORM_EPS) * fg_ref[...]
        o_ref[...] = y


def _mlp(x, norm_g, mods, row_of_tile, w1, w2, final_g, final, tm, tf):
    t = x.shape[0]
    nf = D_FF // tf

    def mod(which):
        return pl.BlockSpec((None, None, 1, D_MODEL), lambda i, f: (row_of_tile(i), which, 0, 0))

    return pl.pallas_call(
        functools.partial(_mlp_kernel, nf=nf, final=final),
        grid=(t // tm, nf),
        in_specs=[
            pl.BlockSpec((tm, D_MODEL), lambda i, f: (i, 0)),
            pl.BlockSpec((1, D_MODEL), lambda i, f: (0, 0)),
            mod(3), mod(4), mod(5),
            pl.BlockSpec((D_MODEL, tf), lambda i, f: (0, f)),
            pl.BlockSpec((tf, D_MODEL), lambda i, f: (f, 0)),
            pl.BlockSpec((1, D_MODEL), lambda i, f: (0, 0)),
        ],
        out_specs=pl.BlockSpec((tm, D_MODEL), lambda i, f: (i, 0)),
        out_shape=jax.ShapeDtypeStruct((t, D_MODEL), F32),
        scratch_shapes=[pltpu.VMEM((tm, D_MODEL), BF16)],
        compiler_params=_cparams(2),
        name="mlp",
    )(x, norm_g, mods, mods, mods, w1, w2, final_g)


def _rope_tables():
    t = np.arange(DEC_SEQ)
    half = HEAD_DIM // 4
    inv = ROPE_THETA ** (-np.arange(half, dtype=np.float32) / half)
    ang_r = (t // GRID_W).astype(np.float32)[:, None] * inv
    ang_c = (t % GRID_W).astype(np.float32)[:, None] * inv
    zero = np.zeros_like(ang_r)
    cos = np.concatenate([np.cos(ang_r)] * 2 + [np.cos(ang_c)] * 2, axis=-1)
    sin_lo = np.concatenate([-np.sin(ang_r), zero, -np.sin(ang_c), zero], axis=-1)
    sin_hi = np.concatenate([zero, np.sin(ang_r), zero, np.sin(ang_c)], axis=-1)
    return tuple(jnp.asarray(a, F32) for a in (cos, sin_lo, sin_hi))


def kernel(x_prompt, x_sample, cache_attn_k, cache_attn_v, cache_na_k, cache_na_v, c, c_ctx, mod_w, mod_b,
           norm1_g, norm2_g, ab_w_in, ab_conv_w, ab_q_norm, ab_k_norm, ab_w_out, na_w_qkv, na_rel_bias,
           na_w_out, mlp_w1, mlp_w2, final_norm_g):
    n_ctx = BATCH * SEQ
    xp = x_prompt.reshape(n_ctx, D_MODEL)
    xs = x_sample.reshape(DEC_BATCH * DEC_SEQ, D_MODEL)

    cond = jnp.concatenate([c_ctx[None, :], c, jnp.zeros((8 - 1 - DEC_BATCH, D_MODEL), F32)], axis=0)
    mods = _modulation(cond, mod_w, mod_b).reshape(2, 8, 6, 1, D_MODEL)

    tm_in, tm_out, tm_mlp, tf = 512, 1024, 512, 1024

    def ctx_row(i):
        return 0

    def lat_row(tm):
        return lambda i: 1 + i // (DEC_SEQ // tm)

    rope_tabs = _rope_tables()
    fg = final_norm_g.reshape(1, D_MODEL)

    m0 = mods[0]
    g1 = norm1_g[0].reshape(1, D_MODEL)
    w_in = ab_w_in[0].astype(BF16)
    qn = ab_q_norm[0].reshape(1, HEAD_DIM)
    kn = ab_k_norm[0].reshape(1, HEAD_DIM)
    cw = ab_conv_w[0]
    w_out = ab_w_out[0].astype(BF16)

    zc_p, q_p, kv_p = _in_proj0(xp, g1, m0, ctx_row, w_in, qn, kn, None, F32, tm_in)
    zc_s, q_s, kv_s = _in_proj0(xs, g1, m0, lat_row(tm_in), w_in, qn, kn, rope_tabs, BF16, tm_in)
    new_attn_k = kv_p[:, :KV_W].reshape(BATCH, 1, SEQ, GQA_KV_HEADS, HEAD_DIM)
    new_attn_v = kv_p[:, KV_W:].reshape(BATCH, 1, SEQ, GQA_KV_HEADS, HEAD_DIM)

    mix_p = _attn0_ctx(zc_p, q_p, kv_p, cw)
    lat3 = lambda a: a.reshape(DEC_BATCH, DEC_SEQ, a.shape[-1])
    mix_s = _attn0_lat(lat3(zc_s), lat3(q_s), lat3(kv_s),
                       cache_attn_k[:, 0].reshape(DEC_BATCH, PAST_LEN, KV_W),
                       cache_attn_v[:, 0].reshape(DEC_BATCH, PAST_LEN, KV_W), cw, 256)
    mix_s = mix_s.reshape(DEC_BATCH * DEC_SEQ, D_MODEL)

    xp = _out_proj(mix_p, w_out, xp, m0, ctx_row, tm_out, 1024)
    xs = _out_proj(mix_s, w_out, xs, m0, lat_row(tm_out), tm_out, 1024)

    g2 = norm2_g[0].reshape(1, D_MODEL)
    w1 = mlp_w1[0].astype(BF16)
    w2 = mlp_w2[0].astype(BF16)
    xp = _mlp(xp, g2, m0, ctx_row, w1, w2, fg, False, tm_mlp, tf)
    xs = _mlp(xs, g2, m0, lat_row(tm_mlp), w1, w2, fg, False, tm_mlp, tf)

    m1 = mods[1]
    g1 = norm1_g[1].reshape(1, D_MODEL)
    w_qkv = na_w_qkv[0].astype(BF16)
    w_out = na_w_out[0].astype(BF16)

    qkv_p = _in_proj1(xp, g1, m1, ctx_row, w_qkv, F32, tm_in, 1024)
    qkv_s = _in_proj1(xs, g1, m1, lat_row(tm_in), w_qkv, BF16, tm_in, 1024)
    new_na_k = qkv_p[1].reshape(BATCH, 1, SEQ, NA_HEADS, HEAD_DIM)
    new_na_v = qkv_p[2].reshape(BATCH, 1, SEQ, NA_HEADS, HEAD_DIM)

    mix_p = _attn1_ctx(qkv_p)
    mix_s = _attn1_lat(qkv_s.reshape(3, DEC_BATCH, DEC_SEQ, D_MODEL),
                       cache_na_k[:, 0].reshape(DEC_BATCH, PAST_LEN, D_MODEL),
                       cache_na_v[:, 0].reshape(DEC_BATCH, PAST_LEN, D_MODEL),
                       _na_bias_pieces(na_rel_bias[0]))
    mix_s = mix_s.reshape(DEC_BATCH * DEC_SEQ, D_MODEL)

    xp = _out_proj(mix_p, w_out, xp, m1, ctx_row, tm_out, 1024)
    xs = _out_proj(mix_s, w_out, xs, m1, lat_row(tm_out), tm_out, 1024)

    g2 = norm2_g[1].reshape(1, D_MODEL)
    w1 = mlp_w1[1].astype(BF16)
    w2 = mlp_w2[1].astype(BF16)
    yp = _mlp(xp, g2, m1, ctx_row, w1, w2, fg, True, tm_mlp, tf)
    ys = _mlp(xs, g2, m1, lat_row(tm_mlp), w1, w2, fg, True, tm_mlp, tf)

    return (yp.reshape(BATCH, SEQ, D_MODEL), ys.reshape(DEC_BATCH, DEC_SEQ, D_MODEL),
            new_attn_k, new_attn_v, new_na_k, new_na_v)
```

```python
import functools

import numpy as np
import jax
import jax.numpy as jnp
from jax import lax
from jax.experimental import pallas as pl
from jax.experimental.pallas import tpu as pltpu

D_MODEL = 2048
BATCH = 32
SEQ = 256
DEC_BATCH = 2
DEC_SEQ = 2048
PAST_LEN = 512
GRID_W = 64
GRID_H = DEC_SEQ // GRID_W
HEAD_DIM = 128
CONV_WIDTH = D_MODEL // 2
GQA_HEADS = 8
GQA_KV_HEADS = 2
GQA_GROUP = GQA_HEADS // GQA_KV_HEADS
NA_HEADS = 16
NA_WIN_ROWS = 8
NA_WIN_COLS = 16
D_FF = 4 * D_MODEL
ROPE_THETA = 10000.0
NORM_EPS = 1e-6
AB_IN = 3 * CONV_WIDTH + (GQA_HEADS + 2 * GQA_KV_HEADS) * HEAD_DIM
KV_W = GQA_KV_HEADS * HEAD_DIM
ATTN_SCALE = HEAD_DIM ** -0.5

NA_QROWS = 4
NA_SLAB_ROWS = NA_QROWS + NA_WIN_ROWS
NA_QBLK = NA_QROWS * GRID_W
NA_SLAB = NA_SLAB_ROWS * GRID_W
NA_NBLK = GRID_H // NA_QROWS
MASK_NEG = -1e30

V7X_VMEM_LIMIT = 56 * 1024 * 1024

BF16 = jnp.bfloat16
F32 = jnp.float32


def _cparams(n_axes):
    return pltpu.CompilerParams(
        dimension_semantics=("parallel",) + ("arbitrary",) * (n_axes - 1),
        vmem_limit_bytes=V7X_VMEM_LIMIT)


def _nt_dot(a, b):
    return lax.dot_general(a, b, (((1,), (1,)), ((), ())), preferred_element_type=F32)


NORM_ROWS = 16


def _ada_norm_rows(h_ref, x_ref, g_ref, shift_ref, scale_ref):
    gain = g_ref[...] * (1.0 + scale_ref[...])
    shift = shift_ref[...]

    def body(c, carry):
        rows = pl.ds(pl.multiple_of(c * NORM_ROWS, NORM_ROWS), NORM_ROWS)
        x = x_ref[rows, :]
        ms = jnp.mean(x * x, axis=-1, keepdims=True)
        h_ref[rows, :] = (x * lax.rsqrt(ms + NORM_EPS) * gain + shift).astype(h_ref.dtype)
        return carry

    lax.fori_loop(0, x_ref.shape[0] // NORM_ROWS, body, 0, unroll=8)


def _head_norm(x, g):
    ms = jnp.mean(x * x, axis=-1, keepdims=True)
    return x * lax.rsqrt(ms + NORM_EPS) * g


def _rope(x, cos, sin_lo, sin_hi):
    return x * cos + pltpu.roll(x, HEAD_DIM - 32, 1) * sin_lo + pltpu.roll(x, 32, 1) * sin_hi


def _softmax_pv(scores, values):
    m = scores[0].max(axis=-1, keepdims=True)
    for s in scores[1:]:
        m = jnp.maximum(m, s.max(axis=-1, keepdims=True))
    l = None
    o = None
    for s, v in zip(scores, values):
        p = jnp.exp(s - m)
        ls = jnp.sum(p, axis=-1, keepdims=True)
        os_ = jnp.dot(p.astype(BF16), v, preferred_element_type=F32)
        l = ls if l is None else l + ls
        o = os_ if o is None else o + os_
    return o / l


def _mod_kernel(c_ref, w_ref, b_ref, o_ref):
    c = c_ref[...]
    s = (c * jax.nn.sigmoid(c)).astype(BF16)
    o_ref[...] = jnp.dot(s, w_ref[...].astype(BF16), preferred_element_type=F32) + b_ref[...]


def _modulation(cond, mod_w, mod_b):
    depth, d, n = mod_w.shape
    tn = 1024
    rows = cond.shape[0]
    return pl.pallas_call(
        _mod_kernel,
        grid=(depth, n // tn),
        in_specs=[
            pl.BlockSpec((rows, d), lambda l, j: (0, 0)),
            pl.BlockSpec((None, d, tn), lambda l, j: (l, 0, j)),
            pl.BlockSpec((None, 1, tn), lambda l, j: (l, 0, j)),
        ],
        out_specs=pl.BlockSpec((None, rows, tn), lambda l, j: (l, 0, j)),
        out_shape=jax.ShapeDtypeStruct((depth, rows, n), F32),
        compiler_params=_cparams(2),
        name="modulation",
    )(cond, mod_w, mod_b.reshape(depth, 1, n))


def _mod_spec(which, row_of_tile):
    return pl.BlockSpec((None, None, 1, D_MODEL), lambda i, j: (row_of_tile(i), which, 0, 0))


Q_W = GQA_HEADS * HEAD_DIM
IN0_TN = Q_W + 2 * KV_W


def _in0_kernel(x_ref, g_ref, shift_ref, scale_ref, w_ref, qn_ref, kn_ref, *rest, rope):
    if rope:
        cos_ref, slo_ref, shi_ref, zc_ref, q_ref, kv_ref, h_scr = rest
    else:
        zc_ref, q_ref, kv_ref, h_scr = rest
    j = pl.program_id(1)
    n_zc = 3 * CONV_WIDTH // IN0_TN

    @pl.when(j == 0)
    def _():
        _ada_norm_rows(h_scr, x_ref, g_ref, shift_ref, scale_ref)

    acc = jnp.dot(h_scr[...], w_ref[...], preferred_element_type=F32)

    def normed(blk, gain):
        y = _head_norm(blk, gain)
        if rope:
            y = _rope(y, cos_ref[...], slo_ref[...], shi_ref[...])
        return y

    @pl.when(j < n_zc)
    def _():
        zc_ref[...] = acc.astype(zc_ref.dtype)

    @pl.when(j == n_zc)
    def _():
        for hh in range(GQA_HEADS):
            sl = slice(hh * HEAD_DIM, (hh + 1) * HEAD_DIM)
            q_ref[:, sl] = (normed(acc[:, sl], qn_ref[...]) * ATTN_SCALE).astype(q_ref.dtype)
        for hh in range(GQA_KV_HEADS):
            src = slice(Q_W + hh * HEAD_DIM, Q_W + (hh + 1) * HEAD_DIM)
            kv_ref[:, hh * HEAD_DIM:(hh + 1) * HEAD_DIM] = normed(acc[:, src], kn_ref[...]).astype(kv_ref.dtype)
        kv_ref[:, KV_W:] = acc[:, Q_W + KV_W:].astype(kv_ref.dtype)


def _in_proj0(x, norm_g, mods, row_of_tile, w, qn, kn, rope_tabs, kv_dtype, tm):
    t = x.shape[0]
    tn = IN0_TN
    n_zc = 3 * CONV_WIDTH // tn
    rope = rope_tabs is not None
    vec = pl.BlockSpec((1, HEAD_DIM), lambda i, j: (0, 0))
    in_specs = [
        pl.BlockSpec((tm, D_MODEL), lambda i, j: (i, 0)),
        pl.BlockSpec((1, D_MODEL), lambda i, j: (0, 0)),
        _mod_spec(0, row_of_tile),
        _mod_spec(1, row_of_tile),
        pl.BlockSpec((D_MODEL, tn), lambda i, j: (0, j)),
        vec, vec,
    ]
    args = [x, norm_g, mods, mods, w, qn, kn]
    if rope:
        nt = DEC_SEQ // tm
        tab = pl.BlockSpec((tm, HEAD_DIM), lambda i, j: (i % nt, 0))
        in_specs += [tab, tab, tab]
        args += list(rope_tabs)
    return pl.pallas_call(
        functools.partial(_in0_kernel, rope=rope),
        grid=(t // tm, AB_IN // tn),
        in_specs=in_specs,
        out_specs=[
            pl.BlockSpec((tm, tn), lambda i, j: (i, jnp.minimum(j, n_zc - 1))),
            pl.BlockSpec((tm, Q_W), lambda i, j: (i, 0)),
            pl.BlockSpec((tm, 2 * KV_W), lambda i, j: (i, 0)),
        ],
        out_shape=[
            jax.ShapeDtypeStruct((t, 3 * CONV_WIDTH), BF16),
            jax.ShapeDtypeStruct((t, GQA_HEADS * HEAD_DIM), BF16),
            jax.ShapeDtypeStruct((t, 2 * KV_W), kv_dtype),
        ],
        scratch_shapes=[pltpu.VMEM((tm, D_MODEL), BF16)],
        compiler_params=_cparams(2),
        name="in_proj0",
    )(*args)


def _in1_kernel(x_ref, g_ref, shift_ref, scale_ref, w_ref, o_ref, h_scr, *, nb):
    j = pl.program_id(1)

    @pl.when(j == 0)
    def _():
        _ada_norm_rows(h_scr, x_ref, g_ref, shift_ref, scale_ref)

    acc = jnp.dot(h_scr[...], w_ref[...], preferred_element_type=F32)
    o_ref[...] = (acc * jnp.where(j < nb, ATTN_SCALE, 1.0)).astype(o_ref.dtype)


def _in_proj1(x, norm_g, mods, row_of_tile, w, out_dtype, tm, tn):
    t = x.shape[0]
    nb = D_MODEL // tn
    return pl.pallas_call(
        functools.partial(_in1_kernel, nb=nb),
        grid=(t // tm, 3 * nb),
        in_specs=[
            pl.BlockSpec((tm, D_MODEL), lambda i, j: (i, 0)),
            pl.BlockSpec((1, D_MODEL), lambda i, j: (0, 0)),
            _mod_spec(0, row_of_tile),
            _mod_spec(1, row_of_tile),
            pl.BlockSpec((D_MODEL, tn), lambda i, j: (0, j)),
        ],
        out_specs=pl.BlockSpec((None, tm, tn), lambda i, j: (j // nb, i, j % nb)),
        out_shape=jax.ShapeDtypeStruct((3, t, D_MODEL), out_dtype),
        scratch_shapes=[pltpu.VMEM((tm, D_MODEL), BF16)],
        compiler_params=_cparams(2),
        name="in_proj1",
    )(x, norm_g, mods, mods, w)


def _gated_conv(zc, prev_row, next_row, cw):
    c = CONV_WIDTH
    s = zc.shape[0]
    gb = zc[:, 0:c].astype(F32)
    u = zc[:, c:2 * c].astype(F32) * zc[:, 2 * c:3 * c].astype(F32)
    row = lax.broadcasted_iota(jnp.int32, u.shape, 0)
    prev = jnp.where(row == 0, prev_row, pltpu.roll(u, 1, 0))
    nxt = jnp.where(row == s - 1, next_row, pltpu.roll(u, s - 1, 0))
    return gb * (prev * cw[0:1] + u * cw[1:2] + nxt * cw[2:3])


def _attn0_ctx_kernel(zc_ref, q_ref, kv_ref, cw_ref, o_ref):
    zero = jnp.zeros((1, CONV_WIDTH), F32)
    a = _gated_conv(zc_ref[...], zero, zero, cw_ref[...])
    o_ref[:, 0:CONV_WIDTH] = a.astype(o_ref.dtype)
    for g in range(GQA_KV_HEADS):
        k = kv_ref[:, g * HEAD_DIM:(g + 1) * HEAD_DIM].astype(BF16)
        v = kv_ref[:, KV_W + g * HEAD_DIM:KV_W + (g + 1) * HEAD_DIM].astype(BF16)
        for hh in range(GQA_GROUP):
            h = g * GQA_GROUP + hh
            q = q_ref[:, h * HEAD_DIM:(h + 1) * HEAD_DIM]
            o = _softmax_pv([_nt_dot(q, k)], [v])
            o_ref[:, CONV_WIDTH + h * HEAD_DIM:CONV_WIDTH + (h + 1) * HEAD_DIM] = o.astype(o_ref.dtype)


def _attn0_ctx(zc, q, kv, cw):
    t = zc.shape[0]
    return pl.pallas_call(
        _attn0_ctx_kernel,
        grid=(t // SEQ,),
        in_specs=[
            pl.BlockSpec((SEQ, 3 * CONV_WIDTH), lambda b: (b, 0)),
            pl.BlockSpec((SEQ, GQA_HEADS * HEAD_DIM), lambda b: (b, 0)),
            pl.BlockSpec((SEQ, 2 * KV_W), lambda b: (b, 0)),
            pl.BlockSpec((3, CONV_WIDTH), lambda b: (0, 0)),
        ],
        out_specs=pl.BlockSpec((SEQ, D_MODEL), lambda b: (b, 0)),
        out_shape=jax.ShapeDtypeStruct((t, D_MODEL), BF16),
        compiler_params=_cparams(1),
        name="mixer0_ctx",
    )(zc, q, kv, cw)


_HALO = 16


def _attn0_lat_kernel(zc_ref, zp_ref, zn_ref, q_ref, kv_ref, ck_ref, cv_ref, cw_ref, o_ref, *, nqb):
    qb = pl.program_id(1)
    c = CONV_WIDTH
    up = zp_ref[:, c:2 * c].astype(F32) * zp_ref[:, 2 * c:3 * c].astype(F32)
    un = zn_ref[:, c:2 * c].astype(F32) * zn_ref[:, 2 * c:3 * c].astype(F32)
    up = jnp.where(qb == 0, 0.0, up[_HALO - 1:_HALO, :])
    un = jnp.where(qb == nqb - 1, 0.0, un[0:1, :])
    a = _gated_conv(zc_ref[...], up, un, cw_ref[...])
    o_ref[:, 0:c] = a.astype(o_ref.dtype)
    for g in range(GQA_KV_HEADS):
        ks = slice(g * HEAD_DIM, (g + 1) * HEAD_DIM)
        vs = slice(KV_W + g * HEAD_DIM, KV_W + (g + 1) * HEAD_DIM)
        k = kv_ref[:, ks]
        v = kv_ref[:, vs]
        ck = ck_ref[:, ks].astype(BF16)
        cv = cv_ref[:, ks].astype(BF16)
        for hh in range(GQA_GROUP):
            h = g * GQA_GROUP + hh
            q = q_ref[:, h * HEAD_DIM:(h + 1) * HEAD_DIM]
            o = _softmax_pv([_nt_dot(q, ck), _nt_dot(q, k)], [cv, v])
            o_ref[:, c + h * HEAD_DIM:c + (h + 1) * HEAD_DIM] = o.astype(o_ref.dtype)


def _attn0_lat(zc, q, kv, ck, cv, cw, tq):
    nb, t, _ = zc.shape
    nqb = t // tq
    hb = tq // _HALO
    return pl.pallas_call(
        functools.partial(_attn0_lat_kernel, nqb=nqb),
        grid=(nb, nqb),
        in_specs=[
            pl.BlockSpec((None, tq, 3 * CONV_WIDTH), lambda b, i: (b, i, 0)),
            pl.BlockSpec((None, _HALO, 3 * CONV_WIDTH), lambda b, i: (b, jnp.maximum(i * hb - 1, 0), 0)),
            pl.BlockSpec((None, _HALO, 3 * CONV_WIDTH),
                         lambda b, i: (b, jnp.minimum((i + 1) * hb, t // _HALO - 1), 0)),
            pl.BlockSpec((None, tq, GQA_HEADS * HEAD_DIM), lambda b, i: (b, i, 0)),
            pl.BlockSpec((None, t, 2 * KV_W), lambda b, i: (b, 0, 0)),
            pl.BlockSpec((None, PAST_LEN, KV_W), lambda b, i: (b, 0, 0)),
            pl.BlockSpec((None, PAST_LEN, KV_W), lambda b, i: (b, 0, 0)),
            pl.BlockSpec((3, CONV_WIDTH), lambda b, i: (0, 0)),
        ],
        out_specs=pl.BlockSpec((None, tq, D_MODEL), lambda b, i: (b, i, 0)),
        out_shape=jax.ShapeDtypeStruct((nb, t, D_MODEL), BF16),
        compiler_params=_cparams(2),
        name="mixer0_lat",
    )(zc, zc, zc, q, kv, ck, cv, cw)


def _attn1_ctx_kernel(q_ref, k_ref, v_ref, o_ref):
    for h in range(NA_HEADS):
        sl = slice(h * HEAD_DIM, (h + 1) * HEAD_DIM)
        q = q_ref[:, sl].astype(BF16)
        k = k_ref[:, sl].astype(BF16)
        v = v_ref[:, sl].astype(BF16)
        o = _softmax_pv([_nt_dot(q, k)], [v])
        o_ref[:, sl] = o.astype(o_ref.dtype)


def _attn1_ctx(qkv):
    t = qkv.shape[1]
    part = lambda p: pl.BlockSpec((None, SEQ, D_MODEL), lambda b: (p, b, 0))
    return pl.pallas_call(
        _attn1_ctx_kernel,
        grid=(t // SEQ,),
        in_specs=[part(0), part(1), part(2)],
        out_specs=pl.BlockSpec((SEQ, D_MODEL), lambda b: (b, 0)),
        out_shape=jax.ShapeDtypeStruct((t, D_MODEL), BF16),
        compiler_params=_cparams(1),
        name="mixer1_ctx",
    )(qkv, qkv, qkv)


def _na_slab_row(blk):
    return min(max(NA_QROWS * blk - NA_WIN_ROWS // 2, 0), GRID_H - NA_SLAB_ROWS)


NA_NROW = 2 * NA_WIN_ROWS - 1
NA_MASKED = NA_NROW
NA_FIRST_IN = NA_WIN_ROWS - 1 - NA_WIN_ROWS // 2
NA_LAST_IN = NA_FIRST_IN + NA_WIN_ROWS - 1
NA_PIECE_PAIRS = ([(m, m + 1) for m in range(NA_NROW - 1)]
                  + [(NA_MASKED, NA_FIRST_IN), (NA_LAST_IN, NA_MASKED)])


def _na_piece(blk, dr, jp):
    r = NA_QROWS * blk + dr
    ks = min(max(r - NA_WIN_ROWS // 2, 0), GRID_H - NA_WIN_ROWS)
    pair = []
    for j in (2 * jp, 2 * jp + 1):
        krow = _na_slab_row(blk) + j
        pair.append(krow - r + NA_WIN_ROWS - 1 if ks <= krow < ks + NA_WIN_ROWS else NA_MASKED)
    pair = tuple(pair)
    if pair == (NA_MASKED, NA_MASKED):
        return None
    return NA_PIECE_PAIRS.index(pair)


def _na_bias_pieces(rel_bias):
    pad = GRID_W - NA_WIN_COLS
    period = 2 * GRID_W
    p = jnp.pad(rel_bias, ((0, 0), (0, 0), (pad, period - (2 * NA_WIN_COLS - 1) - pad)),
                constant_values=MASK_NEG)
    rep = jnp.tile(p, (1, 1, GRID_W))[..., :GRID_W * (period - 1)]
    tcol = rep.reshape(NA_HEADS, NA_NROW, GRID_W, period - 1)[..., GRID_W - 1:]
    c = np.arange(GRID_W)[:, None]
    kc = np.arange(GRID_W)[None, :]
    cs = np.clip(c - NA_WIN_COLS // 2, 0, GRID_W - NA_WIN_COLS)
    tcol = jnp.where((kc >= cs) & (kc < cs + NA_WIN_COLS), tcol, MASK_NEG)
    masked = jnp.full((NA_HEADS, 1, GRID_W, GRID_W), MASK_NEG, F32)
    ext = jnp.concatenate([tcol, masked], axis=1)
    one = lambda a: ext[:, a:a + 1]
    left = jnp.concatenate([ext[:, :NA_NROW - 1], one(NA_MASKED), one(NA_LAST_IN)], axis=1)
    right = jnp.concatenate([ext[:, 1:NA_NROW], one(NA_FIRST_IN), one(NA_MASKED)], axis=1)
    return jnp.concatenate([left, right], axis=-1)


def _na_bias_block(pieces_ref, blk):
    rows = []
    for dr in range(NA_QROWS):
        cols = []
        for jp in range(NA_SLAB_ROWS // 2):
            m = _na_piece(blk, dr, jp)
            cols.append(jnp.full((GRID_W, 2 * GRID_W), MASK_NEG, F32) if m is None else pieces_ref[m])
        rows.append(jnp.concatenate(cols, axis=1))
    return jnp.concatenate(rows, axis=0)


def _attn1_lat_kernel(q_ref, k_ref, v_ref, ck_ref, cv_ref, pieces_ref, o_ref):
    ck = ck_ref[...].astype(BF16)
    cv = cv_ref[...].astype(BF16)
    for blk in range(NA_NBLK):
        rows = slice(blk * NA_QBLK, (blk + 1) * NA_QBLK)
        s0 = _na_slab_row(blk) * GRID_W
        slab = slice(s0, s0 + NA_SLAB)
        q = q_ref[rows, :]
        s_loc = _nt_dot(q, k_ref[slab, :]) + _na_bias_block(pieces_ref, blk)
        o = _softmax_pv([s_loc, _nt_dot(q, ck)], [v_ref[slab, :], cv])
        o_ref[rows, :] = o.astype(o_ref.dtype)


def _attn1_lat(qkv, ck, cv, pieces):
    _, nb, t, _ = qkv.shape
    part = lambda p: pl.BlockSpec((None, None, t, HEAD_DIM), lambda b, h: (p, b, 0, h))
    head = pl.BlockSpec((None, t, HEAD_DIM), lambda b, h: (b, 0, h))
    ctx = pl.BlockSpec((None, PAST_LEN, HEAD_DIM), lambda b, h: (b, 0, h))
    return pl.pallas_call(
        _attn1_lat_kernel,
        grid=(nb, NA_HEADS),
        in_specs=[part(0), part(1), part(2), ctx, ctx,
                  pl.BlockSpec((None, len(NA_PIECE_PAIRS), GRID_W, 2 * GRID_W), lambda b, h: (h, 0, 0, 0))],
        out_specs=head,
        out_shape=jax.ShapeDtypeStruct((nb, t, D_MODEL), BF16),
        compiler_params=_cparams(2),
        name="mixer1_lat",
    )(qkv, qkv, qkv, ck, cv, pieces)


def _out_proj_kernel(m_ref, w_ref, x_ref, gate_ref, o_ref):
    acc = jnp.dot(m_ref[...], w_ref[...], preferred_element_type=F32)
    o_ref[...] = x_ref[...] + gate_ref[...] * acc


def _out_proj(mix, w, x, mods, row_of_tile, tm, tn):
    t, k = mix.shape
    return pl.pallas_call(
        _out_proj_kernel,
        grid=(t // tm, D_MODEL // tn),
        in_specs=[
            pl.BlockSpec((tm, k), lambda i, j: (i, 0)),
            pl.BlockSpec((k, tn), lambda i, j: (0, j)),
            pl.BlockSpec((tm, tn), lambda i, j: (i, j)),
            pl.BlockSpec((None, None, 1, tn), lambda i, j: (row_of_tile(i), 2, 0, j)),
        ],
        out_specs=pl.BlockSpec((tm, tn), lambda i, j: (i, j)),
        out_shape=jax.ShapeDtypeStruct((t, D_MODEL), F32),
        compiler_params=_cparams(2),
        name="out_proj",
    )(mix, w, x, mods)


def _mlp_kernel(x_ref, g_ref, shift_ref, scale_ref, gate_ref, w1_ref, w2_ref, fg_ref, o_ref, h_scr,
                *, nf, final):
    f = pl.program_id(1)

    @pl.when(f == 0)
    def _():
        _ada_norm_rows(h_scr, x_ref, g_ref, shift_ref, scale_ref)
        o_ref[...] = jnp.zeros_like(o_ref)

    u = jnp.dot(h_scr[...], w1_ref[...].astype(BF16), preferred_element_type=F32)
    u = jnp.square(jnp.maximum(u, 0.0)).astype(BF16)
    o_ref[...] += jnp.dot(u, w2_ref[...].astype(BF16), preferred_element_type=F32)

    @pl.when(f == nf - 1)
    def _():
        y = x_ref[...] + gate_ref[...] * o_ref[...]
        if final:
            ms = jnp.mean(y * y, axis=-1, keepdims=True)
            y = y * lax.rsqrt(ms + NORM_EPS) * fg_ref[...]
        o_ref[...] = y


def _mlp(x, norm_g, mods, row_of_tile, w1, w2, final_g, final, tm, tf):
    t = x.shape[0]
    nf = D_FF // tf

    def mod(which):
        return pl.BlockSpec((None, None, 1, D_MODEL), lambda i, f: (row_of_tile(i), which, 0, 0))

    return pl.pallas_call(
        functools.partial(_mlp_kernel, nf=nf, final=final),
        grid=(t // tm, nf),
        in_specs=[
            pl.BlockSpec((tm, D_MODEL), lambda i, f: (i, 0), pipeline_mode=pl.Buffered(1)),
            pl.BlockSpec((1, D_MODEL), lambda i, f: (0, 0)),
            mod(3), mod(4), mod(5),
            pl.BlockSpec((D_MODEL, tf), lambda i, f: (0, f)),
            pl.BlockSpec((tf, D_MODEL), lambda i, f: (f, 0)),
            pl.BlockSpec((1, D_MODEL), lambda i, f: (0, 0)),
        ],
        out_specs=pl.BlockSpec((tm, D_MODEL), lambda i, f: (i, 0)),
        out_shape=jax.ShapeDtypeStruct((t, D_MODEL), F32),
        scratch_shapes=[pltpu.VMEM((tm, D_MODEL), BF16)],
        compiler_params=_cparams(2),
        name="mlp",
    )(x, norm_g, mods, mods, mods, w1, w2, final_g)


def _rope_tables():
    t = np.arange(DEC_SEQ)
    half = HEAD_DIM // 4
    inv = ROPE_THETA ** (-np.arange(half, dtype=np.float32) / half)
    ang_r = (t // GRID_W).astype(np.float32)[:, None] * inv
    ang_c = (t % GRID_W).astype(np.float32)[:, None] * inv
    zero = np.zeros_like(ang_r)
    cos = np.concatenate([np.cos(ang_r)] * 2 + [np.cos(ang_c)] * 2, axis=-1)
    sin_lo = np.concatenate([-np.sin(ang_r), zero, -np.sin(ang_c), zero], axis=-1)
    sin_hi = np.concatenate([zero, np.sin(ang_r), zero, np.sin(ang_c)], axis=-1)
    return tuple(jnp.asarray(a, F32) for a in (cos, sin_lo, sin_hi))


def kernel(x_prompt, x_sample, cache_attn_k, cache_attn_v, cache_na_k, cache_na_v, c, c_ctx, mod_w, mod_b,
           norm1_g, norm2_g, ab_w_in, ab_conv_w, ab_q_norm, ab_k_norm, ab_w_out, na_w_qkv, na_rel_bias,
           na_w_out, mlp_w1, mlp_w2, final_norm_g):
    n_ctx = BATCH * SEQ
    xp = x_prompt.reshape(n_ctx, D_MODEL)
    xs = x_sample.reshape(DEC_BATCH * DEC_SEQ, D_MODEL)

    cond = jnp.concatenate([c_ctx[None, :], c, jnp.zeros((8 - 1 - DEC_BATCH, D_MODEL), F32)], axis=0)
    mods = _modulation(cond, mod_w, mod_b).reshape(2, 8, 6, 1, D_MODEL)

    tm_in, tm_out, tm_mlp, tf = 512, 1024, 1024, 512

    def ctx_row(i):
        return 0

    def lat_row(tm):
        return lambda i: 1 + i // (DEC_SEQ // tm)

    rope_tabs = _rope_tables()
    fg = final_norm_g.reshape(1, D_MODEL)

    m0 = mods[0]
    g1 = norm1_g[0].reshape(1, D_MODEL)
    w_in = ab_w_in[0].astype(BF16)
    qn = ab_q_norm[0].reshape(1, HEAD_DIM)
    kn = ab_k_norm[0].reshape(1, HEAD_DIM)
    cw = ab_conv_w[0]
    w_out = ab_w_out[0].astype(BF16)

    zc_p, q_p, kv_p = _in_proj0(xp, g1, m0, ctx_row, w_in, qn, kn, None, F32, tm_in)
    zc_s, q_s, kv_s = _in_proj0(xs, g1, m0, lat_row(tm_in), w_in, qn, kn, rope_tabs, BF16, tm_in)
    new_attn_k = kv_p[:, :KV_W].reshape(BATCH, 1, SEQ, GQA_KV_HEADS, HEAD_DIM)
    new_attn_v = kv_p[:, KV_W:].reshape(BATCH, 1, SEQ, GQA_KV_HEADS, HEAD_DIM)

    mix_p = _attn0_ctx(zc_p, q_p, kv_p, cw)
    lat3 = lambda a: a.reshape(DEC_BATCH, DEC_SEQ, a.shape[-1])
    mix_s = _attn0_lat(lat3(zc_s), lat3(q_s), lat3(kv_s),
                       cache_attn_k[:, 0].reshape(DEC_BATCH, PAST_LEN, KV_W),
                       cache_attn_v[:, 0].reshape(DEC_BATCH, PAST_LEN, KV_W), cw, 256)
    mix_s = mix_s.reshape(DEC_BATCH * DEC_SEQ, D_MODEL)

    xp = _out_proj(mix_p, w_out, xp, m0, ctx_row, tm_out, 1024)
    xs = _out_proj(mix_s, w_out, xs, m0, lat_row(tm_out), tm_out, 1024)

    g2 = norm2_g[0].reshape(1, D_MODEL)
    w1 = mlp_w1[0]
    w2 = mlp_w2[0]
    xp = _mlp(xp, g2, m0, ctx_row, w1, w2, fg, False, tm_mlp, tf)
    xs = _mlp(xs, g2, m0, lat_row(tm_mlp), w1, w2, fg, False, tm_mlp, tf)

    m1 = mods[1]
    g1 = norm1_g[1].reshape(1, D_MODEL)
    w_qkv = na_w_qkv[0].astype(BF16)
    w_out = na_w_out[0].astype(BF16)

    qkv_p = _in_proj1(xp, g1, m1, ctx_row, w_qkv, F32, 1024, 1024)
    qkv_s = _in_proj1(xs, g1, m1, lat_row(1024), w_qkv, BF16, 1024, 1024)
    new_na_k = qkv_p[1].reshape(BATCH, 1, SEQ, NA_HEADS, HEAD_DIM)
    new_na_v = qkv_p[2].reshape(BATCH, 1, SEQ, NA_HEADS, HEAD_DIM)

    mix_p = _attn1_ctx(qkv_p)
    mix_s = _attn1_lat(qkv_s.reshape(3, DEC_BATCH, DEC_SEQ, D_MODEL),
                       cache_na_k[:, 0].reshape(DEC_BATCH, PAST_LEN, D_MODEL),
                       cache_na_v[:, 0].reshape(DEC_BATCH, PAST_LEN, D_MODEL),
                       _na_bias_pieces(na_rel_bias[0]))
    mix_s = mix_s.reshape(DEC_BATCH * DEC_SEQ, D_MODEL)

    xp = _out_proj(mix_p, w_out, xp, m1, ctx_row, tm_out, 1024)
    xs = _out_proj(mix_s, w_out, xs, m1, lat_row(tm_out), tm_out, 1024)

    g2 = norm2_g[1].reshape(1, D_MODEL)
    w1 = mlp_w1[1]
    w2 = mlp_w2[1]
    yp = _mlp(xp, g2, m1, ctx_row, w1, w2, fg, True, tm_mlp, tf)
    ys = _mlp(xs, g2, m1, lat_row(tm_mlp), w1, w2, fg, True, tm_mlp, tf)

    return (yp.reshape(BATCH, SEQ, D_MODEL), ys.reshape(DEC_BATCH, DEC_SEQ, D_MODEL),
            new_attn_k, new_attn_v, new_na_k, new_na_v)
```

```python
import functools

import numpy as np
import jax
import jax.numpy as jnp
from jax import lax
from jax.experimental import pallas as pl
from jax.experimental.pallas import tpu as pltpu

D_MODEL = 2048
BATCH = 32
SEQ = 256
DEC_BATCH = 2
DEC_SEQ = 2048
PAST_LEN = 512
GRID_W = 64
GRID_H = DEC_SEQ // GRID_W
HEAD_DIM = 128
CONV_WIDTH = D_MODEL // 2
GQA_HEADS = 8
GQA_KV_HEADS = 2
GQA_GROUP = GQA_HEADS // GQA_KV_HEADS
NA_HEADS = 16
NA_WIN_ROWS = 8
NA_WIN_COLS = 16
D_FF = 4 * D_MODEL
ROPE_THETA = 10000.0
NORM_EPS = 1e-6
AB_IN = 3 * CONV_WIDTH + (GQA_HEADS + 2 * GQA_KV_HEADS) * HEAD_DIM
KV_W = GQA_KV_HEADS * HEAD_DIM
ATTN_SCALE = HEAD_DIM ** -0.5

NA_QROWS = 4
NA_SLAB_ROWS = NA_QROWS + NA_WIN_ROWS
NA_QBLK = NA_QROWS * GRID_W
NA_SLAB = NA_SLAB_ROWS * GRID_W
NA_NBLK = GRID_H // NA_QROWS
MASK_NEG = -1e30

V7X_VMEM_LIMIT = 56 * 1024 * 1024

BF16 = jnp.bfloat16
F32 = jnp.float32


def _cparams(n_axes):
    return pltpu.CompilerParams(
        dimension_semantics=("parallel",) + ("arbitrary",) * (n_axes - 1),
        vmem_limit_bytes=V7X_VMEM_LIMIT)


def _nt_dot(a, b):
    return lax.dot_general(a, b, (((1,), (1,)), ((), ())), preferred_element_type=F32)


NORM_ROWS = 16


def _ada_norm_rows(h_ref, x_ref, g_ref, shift_ref, scale_ref):
    gain = g_ref[...] * (1.0 + scale_ref[...])
    shift = shift_ref[...]

    def body(c, carry):
        rows = pl.ds(pl.multiple_of(c * NORM_ROWS, NORM_ROWS), NORM_ROWS)
        x = x_ref[rows, :]
        ms = jnp.mean(x * x, axis=-1, keepdims=True)
        h_ref[rows, :] = (x * lax.rsqrt(ms + NORM_EPS) * gain + shift).astype(h_ref.dtype)
        return carry

    lax.fori_loop(0, x_ref.shape[0] // NORM_ROWS, body, 0, unroll=8)


def _head_norm(x, g):
    ms = jnp.mean(x * x, axis=-1, keepdims=True)
    return x * lax.rsqrt(ms + NORM_EPS) * g


def _rope(x, cos, sin_lo, sin_hi):
    return x * cos + pltpu.roll(x, HEAD_DIM - 32, 1) * sin_lo + pltpu.roll(x, 32, 1) * sin_hi


def _softmax_pv(scores, values):
    m = scores[0].max(axis=-1, keepdims=True)
    for s in scores[1:]:
        m = jnp.maximum(m, s.max(axis=-1, keepdims=True))
    l = None
    o = None
    for s, v in zip(scores, values):
        p = jnp.exp(s - m)
        ls = jnp.sum(p, axis=-1, keepdims=True)
        os_ = jnp.dot(p.astype(BF16), v, preferred_element_type=F32)
        l = ls if l is None else l + ls
        o = os_ if o is None else o + os_
    return o / l


def _mod_kernel(c_ref, w_ref, b_ref, o_ref):
    c = c_ref[...]
    s = (c * jax.nn.sigmoid(c)).astype(BF16)
    o_ref[...] = jnp.dot(s, w_ref[...].astype(BF16), preferred_element_type=F32) + b_ref[...]


def _modulation(cond, mod_w, mod_b):
    depth, d, n = mod_w.shape
    tn = 1024
    rows = cond.shape[0]
    return pl.pallas_call(
        _mod_kernel,
        grid=(depth, n // tn),
        in_specs=[
            pl.BlockSpec((rows, d), lambda l, j: (0, 0)),
            pl.BlockSpec((None, d, tn), lambda l, j: (l, 0, j)),
            pl.BlockSpec((None, 1, tn), lambda l, j: (l, 0, j)),
        ],
        out_specs=pl.BlockSpec((None, rows, tn), lambda l, j: (l, 0, j)),
        out_shape=jax.ShapeDtypeStruct((depth, rows, n), F32),
        compiler_params=_cparams(2),
        name="modulation",
    )(cond, mod_w, mod_b.reshape(depth, 1, n))


def _mod_spec(which, row_of_tile):
    return pl.BlockSpec((None, None, 1, D_MODEL), lambda i, j: (row_of_tile(i), which, 0, 0))


Q_W = GQA_HEADS * HEAD_DIM
IN0_TN = Q_W + 2 * KV_W


def _in0_kernel(x_ref, g_ref, shift_ref, scale_ref, w_ref, qn_ref, kn_ref, *rest, rope):
    if rope:
        cos_ref, slo_ref, shi_ref, zc_ref, q_ref, kv_ref, h_scr = rest
    else:
        zc_ref, q_ref, kv_ref, h_scr = rest
    j = pl.program_id(1)
    n_zc = 3 * CONV_WIDTH // IN0_TN

    @pl.when(j == 0)
    def _():
        _ada_norm_rows(h_scr, x_ref, g_ref, shift_ref, scale_ref)

    acc = jnp.dot(h_scr[...], w_ref[...], preferred_element_type=F32)

    def normed(blk, gain):
        y = _head_norm(blk, gain)
        if rope:
            y = _rope(y, cos_ref[...], slo_ref[...], shi_ref[...])
        return y

    @pl.when(j < n_zc)
    def _():
        zc_ref[...] = acc.astype(zc_ref.dtype)

    @pl.when(j == n_zc)
    def _():
        for hh in range(GQA_HEADS):
            sl = slice(hh * HEAD_DIM, (hh + 1) * HEAD_DIM)
            q_ref[:, sl] = (normed(acc[:, sl], qn_ref[...]) * ATTN_SCALE).astype(q_ref.dtype)
        for hh in range(GQA_KV_HEADS):
            src = slice(Q_W + hh * HEAD_DIM, Q_W + (hh + 1) * HEAD_DIM)
            kv_ref[:, hh * HEAD_DIM:(hh + 1) * HEAD_DIM] = normed(acc[:, src], kn_ref[...]).astype(kv_ref.dtype)
        kv_ref[:, KV_W:] = acc[:, Q_W + KV_W:].astype(kv_ref.dtype)


def _in_proj0(x, norm_g, mods, row_of_tile, w, qn, kn, rope_tabs, kv_dtype, tm):
    t = x.shape[0]
    tn = IN0_TN
    n_zc = 3 * CONV_WIDTH // tn
    rope = rope_tabs is not None
    vec = pl.BlockSpec((1, HEAD_DIM), lambda i, j: (0, 0))
    in_specs = [
        pl.BlockSpec((tm, D_MODEL), lambda i, j: (i, 0)),
        pl.BlockSpec((1, D_MODEL), lambda i, j: (0, 0)),
        _mod_spec(0, row_of_tile),
        _mod_spec(1, row_of_tile),
        pl.BlockSpec((D_MODEL, tn), lambda i, j: (0, j)),
        vec, vec,
    ]
    args = [x, norm_g, mods, mods, w, qn, kn]
    if rope:
        nt = DEC_SEQ // tm
        tab = pl.BlockSpec((tm, HEAD_DIM), lambda i, j: (i % nt, 0))
        in_specs += [tab, tab, tab]
        args += list(rope_tabs)
    return pl.pallas_call(
        functools.partial(_in0_kernel, rope=rope),
        grid=(t // tm, AB_IN // tn),
        in_specs=in_specs,
        out_specs=[
            pl.BlockSpec((tm, tn), lambda i, j: (i, jnp.minimum(j, n_zc - 1))),
            pl.BlockSpec((tm, Q_W), lambda i, j: (i, 0)),
            pl.BlockSpec((tm, 2 * KV_W), lambda i, j: (i, 0)),
        ],
        out_shape=[
            jax.ShapeDtypeStruct((t, 3 * CONV_WIDTH), BF16),
            jax.ShapeDtypeStruct((t, GQA_HEADS * HEAD_DIM), BF16),
            jax.ShapeDtypeStruct((t, 2 * KV_W), kv_dtype),
        ],
        scratch_shapes=[pltpu.VMEM((tm, D_MODEL), BF16)],
        compiler_params=_cparams(2),
        name="in_proj0",
    )(*args)


def _in1_kernel(x_ref, g_ref, shift_ref, scale_ref, w_ref, o_ref, h_scr, *, nb):
    j = pl.program_id(1)

    @pl.when(j == 0)
    def _():
        _ada_norm_rows(h_scr, x_ref, g_ref, shift_ref, scale_ref)

    acc = jnp.dot(h_scr[...], w_ref[...], preferred_element_type=F32)
    o_ref[...] = (acc * jnp.where(j < nb, ATTN_SCALE, 1.0)).astype(o_ref.dtype)


def _in_proj1(x, norm_g, mods, row_of_tile, w, out_dtype, tm, tn):
    t = x.shape[0]
    nb = D_MODEL // tn
    return pl.pallas_call(
        functools.partial(_in1_kernel, nb=nb),
        grid=(t // tm, 3 * nb),
        in_specs=[
            pl.BlockSpec((tm, D_MODEL), lambda i, j: (i, 0)),
            pl.BlockSpec((1, D_MODEL), lambda i, j: (0, 0)),
            _mod_spec(0, row_of_tile),
            _mod_spec(1, row_of_tile),
            pl.BlockSpec((D_MODEL, tn), lambda i, j: (0, j)),
        ],
        out_specs=pl.BlockSpec((None, tm, tn), lambda i, j: (j // nb, i, j % nb)),
        out_shape=jax.ShapeDtypeStruct((3, t, D_MODEL), out_dtype),
        scratch_shapes=[pltpu.VMEM((tm, D_MODEL), BF16)],
        compiler_params=_cparams(2),
        name="in_proj1",
    )(x, norm_g, mods, mods, w)


def _in1q_kernel(x_ref, g_ref, shift_ref, scale_ref, w_ref, q_ref, h_ref):
    @pl.when(pl.program_id(1) == 0)
    def _():
        _ada_norm_rows(h_ref, x_ref, g_ref, shift_ref, scale_ref)

    acc = jnp.dot(h_ref[...], w_ref[...], preferred_element_type=F32)
    q_ref[...] = (acc * ATTN_SCALE).astype(q_ref.dtype)


def _in_proj1_q(x, norm_g, mods, row_of_tile, w, tm, tn):
    t = x.shape[0]
    return pl.pallas_call(
        _in1q_kernel,
        grid=(t // tm, D_MODEL // tn),
        in_specs=[
            pl.BlockSpec((tm, D_MODEL), lambda i, j: (i, 0)),
            pl.BlockSpec((1, D_MODEL), lambda i, j: (0, 0)),
            _mod_spec(0, row_of_tile),
            _mod_spec(1, row_of_tile),
            pl.BlockSpec((D_MODEL, tn), lambda i, j: (0, j)),
        ],
        out_specs=[pl.BlockSpec((tm, tn), lambda i, j: (i, j)),
                   pl.BlockSpec((tm, D_MODEL), lambda i, j: (i, 0))],
        out_shape=[jax.ShapeDtypeStruct((t, D_MODEL), BF16), jax.ShapeDtypeStruct((t, D_MODEL), BF16)],
        compiler_params=_cparams(2),
        name="in_proj1_q",
    )(x, norm_g, mods, mods, w)


def _in1kv_kernel(h_ref, w_ref, flat_ref, heads_ref):
    acc = jnp.dot(h_ref[...], w_ref[...], preferred_element_type=F32)
    flat_ref[...] = acc.astype(flat_ref.dtype)
    heads_ref[...] = acc.reshape(heads_ref.shape)


def _in_proj1_kv(h, w, part, tm, tn):
    t = h.shape[0]
    nb = D_MODEL // tn
    hb = tn // HEAD_DIM
    return pl.pallas_call(
        _in1kv_kernel,
        grid=(t // tm, nb),
        in_specs=[
            pl.BlockSpec((tm, D_MODEL), lambda i, j: (i, 0)),
            pl.BlockSpec((D_MODEL, tn), lambda i, j: (0, part * nb + j)),
        ],
        out_specs=[pl.BlockSpec((tm, tn), lambda i, j: (i, j)),
                   pl.BlockSpec((tm, hb, HEAD_DIM), lambda i, j: (i, j, 0))],
        out_shape=[jax.ShapeDtypeStruct((t, D_MODEL), BF16),
                   jax.ShapeDtypeStruct((t, NA_HEADS, HEAD_DIM), F32)],
        compiler_params=_cparams(2),
        name="in_proj1_kv",
    )(h, w)


def _gated_conv(zc, prev_row, next_row, cw):
    c = CONV_WIDTH
    s = zc.shape[0]
    gb = zc[:, 0:c].astype(F32)
    u = zc[:, c:2 * c].astype(F32) * zc[:, 2 * c:3 * c].astype(F32)
    row = lax.broadcasted_iota(jnp.int32, u.shape, 0)
    prev = jnp.where(row == 0, prev_row, pltpu.roll(u, 1, 0))
    nxt = jnp.where(row == s - 1, next_row, pltpu.roll(u, s - 1, 0))
    return gb * (prev * cw[0:1] + u * cw[1:2] + nxt * cw[2:3])


def _attn0_ctx_kernel(zc_ref, q_ref, kv_ref, cw_ref, o_ref):
    zero = jnp.zeros((1, CONV_WIDTH), F32)
    a = _gated_conv(zc_ref[...], zero, zero, cw_ref[...])
    o_ref[:, 0:CONV_WIDTH] = a.astype(o_ref.dtype)
    for g in range(GQA_KV_HEADS):
        k = kv_ref[:, g * HEAD_DIM:(g + 1) * HEAD_DIM].astype(BF16)
        v = kv_ref[:, KV_W + g * HEAD_DIM:KV_W + (g + 1) * HEAD_DIM].astype(BF16)
        for hh in range(GQA_GROUP):
            h = g * GQA_GROUP + hh
            q = q_ref[:, h * HEAD_DIM:(h + 1) * HEAD_DIM]
            o = _softmax_pv([_nt_dot(q, k)], [v])
            o_ref[:, CONV_WIDTH + h * HEAD_DIM:CONV_WIDTH + (h + 1) * HEAD_DIM] = o.astype(o_ref.dtype)


def _attn0_ctx(zc, q, kv, cw):
    t = zc.shape[0]
    return pl.pallas_call(
        _attn0_ctx_kernel,
        grid=(t // SEQ,),
        in_specs=[
            pl.BlockSpec((SEQ, 3 * CONV_WIDTH), lambda b: (b, 0)),
            pl.BlockSpec((SEQ, GQA_HEADS * HEAD_DIM), lambda b: (b, 0)),
            pl.BlockSpec((SEQ, 2 * KV_W), lambda b: (b, 0)),
            pl.BlockSpec((3, CONV_WIDTH), lambda b: (0, 0)),
        ],
        out_specs=pl.BlockSpec((SEQ, D_MODEL), lambda b: (b, 0)),
        out_shape=jax.ShapeDtypeStruct((t, D_MODEL), BF16),
        compiler_params=_cparams(1),
        name="mixer0_ctx",
    )(zc, q, kv, cw)


_HALO = 16


def _attn0_lat_kernel(zc_ref, zp_ref, zn_ref, q_ref, kv_ref, ck_ref, cv_ref, cw_ref, o_ref, *, nqb):
    qb = pl.program_id(1)
    c = CONV_WIDTH
    up = zp_ref[:, c:2 * c].astype(F32) * zp_ref[:, 2 * c:3 * c].astype(F32)
    un = zn_ref[:, c:2 * c].astype(F32) * zn_ref[:, 2 * c:3 * c].astype(F32)
    up = jnp.where(qb == 0, 0.0, up[_HALO - 1:_HALO, :])
    un = jnp.where(qb == nqb - 1, 0.0, un[0:1, :])
    a = _gated_conv(zc_ref[...], up, un, cw_ref[...])
    o_ref[:, 0:c] = a.astype(o_ref.dtype)
    for g in range(GQA_KV_HEADS):
        ks = slice(g * HEAD_DIM, (g + 1) * HEAD_DIM)
        vs = slice(KV_W + g * HEAD_DIM, KV_W + (g + 1) * HEAD_DIM)
        k = kv_ref[:, ks]
        v = kv_ref[:, vs]
        ck = ck_ref[:, ks].astype(BF16)
        cv = cv_ref[:, ks].astype(BF16)
        for hh in range(GQA_GROUP):
            h = g * GQA_GROUP + hh
            q = q_ref[:, h * HEAD_DIM:(h + 1) * HEAD_DIM]
            o = _softmax_pv([_nt_dot(q, ck), _nt_dot(q, k)], [cv, v])
            o_ref[:, c + h * HEAD_DIM:c + (h + 1) * HEAD_DIM] = o.astype(o_ref.dtype)


def _attn0_lat(zc, q, kv, ck, cv, cw, tq):
    nb, t, _ = zc.shape
    nqb = t // tq
    hb = tq // _HALO
    return pl.pallas_call(
        functools.partial(_attn0_lat_kernel, nqb=nqb),
        grid=(nb, nqb),
        in_specs=[
            pl.BlockSpec((None, tq, 3 * CONV_WIDTH), lambda b, i: (b, i, 0)),
            pl.BlockSpec((None, _HALO, 3 * CONV_WIDTH), lambda b, i: (b, jnp.maximum(i * hb - 1, 0), 0)),
            pl.BlockSpec((None, _HALO, 3 * CONV_WIDTH),
                         lambda b, i: (b, jnp.minimum((i + 1) * hb, t // _HALO - 1), 0)),
            pl.BlockSpec((None, tq, GQA_HEADS * HEAD_DIM), lambda b, i: (b, i, 0)),
            pl.BlockSpec((None, t, 2 * KV_W), lambda b, i: (b, 0, 0)),
            pl.BlockSpec((None, PAST_LEN, KV_W), lambda b, i: (b, 0, 0)),
            pl.BlockSpec((None, PAST_LEN, KV_W), lambda b, i: (b, 0, 0)),
            pl.BlockSpec((3, CONV_WIDTH), lambda b, i: (0, 0)),
        ],
        out_specs=pl.BlockSpec((None, tq, D_MODEL), lambda b, i: (b, i, 0)),
        out_shape=jax.ShapeDtypeStruct((nb, t, D_MODEL), BF16),
        compiler_params=_cparams(2),
        name="mixer0_lat",
    )(zc, zc, zc, q, kv, ck, cv, cw)


def _attn1_ctx_kernel(q_ref, k_ref, v_ref, o_ref):
    for h in range(NA_HEADS):
        sl = slice(h * HEAD_DIM, (h + 1) * HEAD_DIM)
        o = _softmax_pv([_nt_dot(q_ref[:, sl], k_ref[:, sl])], [v_ref[:, sl]])
        o_ref[:, sl] = o.astype(o_ref.dtype)


def _attn1_ctx(q, k, v):
    t = q.shape[0]
    blk = pl.BlockSpec((SEQ, D_MODEL), lambda b: (b, 0))
    return pl.pallas_call(
        _attn1_ctx_kernel,
        grid=(t // SEQ,),
        in_specs=[blk, blk, blk],
        out_specs=blk,
        out_shape=jax.ShapeDtypeStruct((t, D_MODEL), BF16),
        compiler_params=_cparams(1),
        name="mixer1_ctx",
    )(q, k, v)


def _na_slab_row(blk):
    return min(max(NA_QROWS * blk - NA_WIN_ROWS // 2, 0), GRID_H - NA_SLAB_ROWS)


NA_NROW = 2 * NA_WIN_ROWS - 1
NA_MASKED = NA_NROW
NA_FIRST_IN = NA_WIN_ROWS - 1 - NA_WIN_ROWS // 2
NA_LAST_IN = NA_FIRST_IN + NA_WIN_ROWS - 1
NA_PIECE_PAIRS = ([(m, m + 1) for m in range(NA_NROW - 1)]
                  + [(NA_MASKED, NA_FIRST_IN), (NA_LAST_IN, NA_MASKED)])


def _na_piece(blk, dr, jp):
    r = NA_QROWS * blk + dr
    ks = min(max(r - NA_WIN_ROWS // 2, 0), GRID_H - NA_WIN_ROWS)
    pair = []
    for j in (2 * jp, 2 * jp + 1):
        krow = _na_slab_row(blk) + j
        pair.append(krow - r + NA_WIN_ROWS - 1 if ks <= krow < ks + NA_WIN_ROWS else NA_MASKED)
    pair = tuple(pair)
    if pair == (NA_MASKED, NA_MASKED):
        return None
    return NA_PIECE_PAIRS.index(pair)


def _na_bias_pieces(rel_bias):
    pad = GRID_W - NA_WIN_COLS
    period = 2 * GRID_W
    p = jnp.pad(rel_bias, ((0, 0), (0, 0), (pad, period - (2 * NA_WIN_COLS - 1) - pad)),
                constant_values=MASK_NEG)
    rep = jnp.tile(p, (1, 1, GRID_W))[..., :GRID_W * (period - 1)]
    tcol = rep.reshape(NA_HEADS, NA_NROW, GRID_W, period - 1)[..., GRID_W - 1:]
    c = np.arange(GRID_W)[:, None]
    kc = np.arange(GRID_W)[None, :]
    cs = np.clip(c - NA_WIN_COLS // 2, 0, GRID_W - NA_WIN_COLS)
    tcol = jnp.where((kc >= cs) & (kc < cs + NA_WIN_COLS), tcol, MASK_NEG)
    masked = jnp.full((NA_HEADS, 1, GRID_W, GRID_W), MASK_NEG, F32)
    ext = jnp.concatenate([tcol, masked], axis=1)
    one = lambda a: ext[:, a:a + 1]
    left = jnp.concatenate([ext[:, :NA_NROW - 1], one(NA_MASKED), one(NA_LAST_IN)], axis=1)
    right = jnp.concatenate([ext[:, 1:NA_NROW], one(NA_FIRST_IN), one(NA_MASKED)], axis=1)
    return jnp.concatenate([left, right], axis=-1)


def _na_bias_block(pieces_ref, blk):
    rows = []
    for dr in range(NA_QROWS):
        cols = []
        for jp in range(NA_SLAB_ROWS // 2):
            m = _na_piece(blk, dr, jp)
            cols.append(jnp.full((GRID_W, 2 * GRID_W), MASK_NEG, F32) if m is None else pieces_ref[m])
        rows.append(jnp.concatenate(cols, axis=1))
    return jnp.concatenate(rows, axis=0)


def _attn1_lat_kernel(q_ref, k_ref, v_ref, ck_ref, cv_ref, pieces_ref, o_ref):
    ck = ck_ref[...].astype(BF16)
    cv = cv_ref[...].astype(BF16)
    for blk in range(NA_NBLK):
        rows = slice(blk * NA_QBLK, (blk + 1) * NA_QBLK)
        s0 = _na_slab_row(blk) * GRID_W
        slab = slice(s0, s0 + NA_SLAB)
        q = q_ref[rows, :]
        s_loc = _nt_dot(q, k_ref[slab, :]) + _na_bias_block(pieces_ref, blk)
        o = _softmax_pv([s_loc, _nt_dot(q, ck)], [v_ref[slab, :], cv])
        o_ref[rows, :] = o.astype(o_ref.dtype)


def _attn1_lat(qkv, ck, cv, pieces):
    _, nb, t, _ = qkv.shape
    part = lambda p: pl.BlockSpec((None, None, t, HEAD_DIM), lambda b, h: (p, b, 0, h))
    head = pl.BlockSpec((None, t, HEAD_DIM), lambda b, h: (b, 0, h))
    ctx = pl.BlockSpec((None, PAST_LEN, HEAD_DIM), lambda b, h: (b, 0, h))
    return pl.pallas_call(
        _attn1_lat_kernel,
        grid=(nb, NA_HEADS),
        in_specs=[part(0), part(1), part(2), ctx, ctx,
                  pl.BlockSpec((None, len(NA_PIECE_PAIRS), GRID_W, 2 * GRID_W), lambda b, h: (h, 0, 0, 0))],
        out_specs=head,
        out_shape=jax.ShapeDtypeStruct((nb, t, D_MODEL), BF16),
        compiler_params=_cparams(2),
        name="mixer1_lat",
    )(qkv, qkv, qkv, ck, cv, pieces)


def _out_proj_kernel(m_ref, w_ref, x_ref, gate_ref, o_ref):
    acc = jnp.dot(m_ref[...], w_ref[...], preferred_element_type=F32)
    o_ref[...] = x_ref[...] + gate_ref[...] * acc


def _out_proj(mix, w, x, mods, row_of_tile, tm, tn):
    t, k = mix.shape
    return pl.pallas_call(
        _out_proj_kernel,
        grid=(t // tm, D_MODEL // tn),
        in_specs=[
            pl.BlockSpec((tm, k), lambda i, j: (i, 0)),
            pl.BlockSpec((k, tn), lambda i, j: (0, j)),
            pl.BlockSpec((tm, tn), lambda i, j: (i, j)),
            pl.BlockSpec((None, None, 1, tn), lambda i, j: (row_of_tile(i), 2, 0, j)),
        ],
        out_specs=pl.BlockSpec((tm, tn), lambda i, j: (i, j)),
        out_shape=jax.ShapeDtypeStruct((t, D_MODEL), F32),
        compiler_params=_cparams(2),
        name="out_proj",
    )(mix, w, x, mods)


def _mlp_kernel(x_ref, g_ref, shift_ref, scale_ref, gate_ref, w1_ref, w2_ref, fg_ref, o_ref, h_scr,
                *, nf, final):
    f = pl.program_id(1)

    @pl.when(f == 0)
    def _():
        _ada_norm_rows(h_scr, x_ref, g_ref, shift_ref, scale_ref)
        o_ref[...] = jnp.zeros_like(o_ref)

    u = jnp.dot(h_scr[...], w1_ref[...], preferred_element_type=F32)
    u = jnp.square(jnp.maximum(u, 0.0)).astype(BF16)
    o_ref[...] += jnp.dot(u, w2_ref[...], preferred_element_type=F32)

    @pl.when(f == nf - 1)
    def _():
        y = x_ref[...] + gate_ref[...] * o_ref[...]
        if final:
            ms = jnp.mean(y * y, axis=-1, keepdims=True)
            y = y * lax.rsqrt(ms + NORM_EPS) * fg_ref[...]
        o_ref[...] = y


def _mlp(x, norm_g, mods, row_of_tile, w1, w2, layer, final_g, final, tm, tf):
    t = x.shape[0]
    nf = D_FF // tf

    def mod(which):
        return pl.BlockSpec((None, None, 1, D_MODEL), lambda i, f: (row_of_tile(i), which, 0, 0))

    return pl.pallas_call(
        functools.partial(_mlp_kernel, nf=nf, final=final),
        grid=(t // tm, nf),
        in_specs=[
            pl.BlockSpec((tm, D_MODEL), lambda i, f: (i, 0)),
            pl.BlockSpec((1, D_MODEL), lambda i, f: (0, 0)),
            mod(3), mod(4), mod(5),
            pl.BlockSpec((None, D_MODEL, tf), lambda i, f: (layer, 0, f)),
            pl.BlockSpec((None, tf, D_MODEL), lambda i, f: (layer, f, 0)),
            pl.BlockSpec((1, D_MODEL), lambda i, f: (0, 0)),
        ],
        out_specs=pl.BlockSpec((tm, D_MODEL), lambda i, f: (i, 0)),
        out_shape=jax.ShapeDtypeStruct((t, D_MODEL), F32),
        scratch_shapes=[pltpu.VMEM((tm, D_MODEL), BF16)],
        compiler_params=_cparams(2),
        name="mlp",
    )(x, norm_g, mods, mods, mods, w1, w2, final_g)


def _rope_tables():
    t = np.arange(DEC_SEQ)
    half = HEAD_DIM // 4
    inv = ROPE_THETA ** (-np.arange(half, dtype=np.float32) / half)
    ang_r = (t // GRID_W).astype(np.float32)[:, None] * inv
    ang_c = (t % GRID_W).astype(np.float32)[:, None] * inv
    zero = np.zeros_like(ang_r)
    cos = np.concatenate([np.cos(ang_r)] * 2 + [np.cos(ang_c)] * 2, axis=-1)
    sin_lo = np.concatenate([-np.sin(ang_r), zero, -np.sin(ang_c), zero], axis=-1)
    sin_hi = np.concatenate([zero, np.sin(ang_r), zero, np.sin(ang_c)], axis=-1)
    return tuple(jnp.asarray(a, F32) for a in (cos, sin_lo, sin_hi))


def kernel(x_prompt, x_sample, cache_attn_k, cache_attn_v, cache_na_k, cache_na_v, c, c_ctx, mod_w, mod_b,
           norm1_g, norm2_g, ab_w_in, ab_conv_w, ab_q_norm, ab_k_norm, ab_w_out, na_w_qkv, na_rel_bias,
           na_w_out, mlp_w1, mlp_w2, final_norm_g):
    n_ctx = BATCH * SEQ
    xp = x_prompt.reshape(n_ctx, D_MODEL)
    xs = x_sample.reshape(DEC_BATCH * DEC_SEQ, D_MODEL)

    cond = jnp.concatenate([c_ctx[None, :], c, jnp.zeros((8 - 1 - DEC_BATCH, D_MODEL), F32)], axis=0)
    mods = _modulation(cond, mod_w, mod_b).reshape(2, 8, 6, 1, D_MODEL)

    tm_in, tm_out, tm_mlp, tf = 512, 1024, 512, 1024
    w1_all = mlp_w1.astype(BF16)
    w2_all = mlp_w2.astype(BF16)

    def ctx_row(i):
        return 0

    def lat_row(tm):
        return lambda i: 1 + i // (DEC_SEQ // tm)

    rope_tabs = _rope_tables()
    fg = final_norm_g.reshape(1, D_MODEL)

    m0 = mods[0]
    g1 = norm1_g[0].reshape(1, D_MODEL)
    w_in = ab_w_in[0].astype(BF16)
    qn = ab_q_norm[0].reshape(1, HEAD_DIM)
    kn = ab_k_norm[0].reshape(1, HEAD_DIM)
    cw = ab_conv_w[0]
    w_out = ab_w_out[0].astype(BF16)

    zc_p, q_p, kv_p = _in_proj0(xp, g1, m0, ctx_row, w_in, qn, kn, None, F32, tm_in)
    zc_s, q_s, kv_s = _in_proj0(xs, g1, m0, lat_row(tm_in), w_in, qn, kn, rope_tabs, BF16, tm_in)
    new_attn_k = kv_p[:, :KV_W].reshape(BATCH, 1, SEQ, GQA_KV_HEADS, HEAD_DIM)
    new_attn_v = kv_p[:, KV_W:].reshape(BATCH, 1, SEQ, GQA_KV_HEADS, HEAD_DIM)

    mix_p = _attn0_ctx(zc_p, q_p, kv_p, cw)
    lat3 = lambda a: a.reshape(DEC_BATCH, DEC_SEQ, a.shape[-1])
    mix_s = _attn0_lat(lat3(zc_s), lat3(q_s), lat3(kv_s),
                       cache_attn_k[:, 0].reshape(DEC_BATCH, PAST_LEN, KV_W),
                       cache_attn_v[:, 0].reshape(DEC_BATCH, PAST_LEN, KV_W), cw, 256)
    mix_s = mix_s.reshape(DEC_BATCH * DEC_SEQ, D_MODEL)

    xp = _out_proj(mix_p, w_out, xp, m0, ctx_row, tm_out, 1024)
    xs = _out_proj(mix_s, w_out, xs, m0, lat_row(tm_out), tm_out, 1024)

    g2 = norm2_g[0].reshape(1, D_MODEL)
    xp = _mlp(xp, g2, m0, ctx_row, w1_all, w2_all, 0, fg, False, tm_mlp, tf)
    xs = _mlp(xs, g2, m0, lat_row(tm_mlp), w1_all, w2_all, 0, fg, False, tm_mlp, tf)

    m1 = mods[1]
    g1 = norm1_g[1].reshape(1, D_MODEL)
    w_qkv = na_w_qkv[0].astype(BF16)
    w_out = na_w_out[0].astype(BF16)

    q_p, h_p = _in_proj1_q(xp, g1, m1, ctx_row, w_qkv, 1024, 1024)
    k_p, k_heads = _in_proj1_kv(h_p, w_qkv, 1, 1024, 1024)
    v_p, v_heads = _in_proj1_kv(h_p, w_qkv, 2, 1024, 1024)
    qkv_s = _in_proj1(xs, g1, m1, lat_row(1024), w_qkv, BF16, 1024, 1024)
    new_na_k = k_heads.reshape(BATCH, 1, SEQ, NA_HEADS, HEAD_DIM)
    new_na_v = v_heads.reshape(BATCH, 1, SEQ, NA_HEADS, HEAD_DIM)

    mix_p = _attn1_ctx(q_p, k_p, v_p)
    mix_s = _attn1_lat(qkv_s.reshape(3, DEC_BATCH, DEC_SEQ, D_MODEL),
                       cache_na_k[:, 0].reshape(DEC_BATCH, PAST_LEN, D_MODEL),
                       cache_na_v[:, 0].reshape(DEC_BATCH, PAST_LEN, D_MODEL),
                       _na_bias_pieces(na_rel_bias[0]))
    mix_s = mix_s.reshape(DEC_BATCH * DEC_SEQ, D_MODEL)

    xp = _out_proj(mix_p, w_out, xp, m1, ctx_row, tm_out, 1024)
    xs = _out_proj(mix_s, w_out, xs, m1, lat_row(tm_out), tm_out, 1024)

    g2 = norm2_g[1].reshape(1, D_MODEL)
    yp = _mlp(xp, g2, m1, ctx_row, w1_all, w2_all, 1, fg, True, tm_mlp, tf)
    ys = _mlp(xs, g2, m1, lat_row(tm_mlp), w1_all, w2_all, 1, fg, True, tm_mlp, tf)

    return (yp.reshape(BATCH, SEQ, D_MODEL), ys.reshape(DEC_BATCH, DEC_SEQ, D_MODEL),
            new_attn_k, new_attn_v, new_na_k, new_na_v)
```

```python
import functools

import numpy as np
import jax
import jax.numpy as jnp
from jax import lax
from jax.experimental import pallas as pl
from jax.experimental.pallas import tpu as pltpu

D_MODEL = 2048
BATCH = 32
SEQ = 256
DEC_BATCH = 2
DEC_SEQ = 2048
PAST_LEN = 512
GRID_W = 64
GRID_H = DEC_SEQ // GRID_W
HEAD_DIM = 128
CONV_WIDTH = D_MODEL // 2
GQA_HEADS = 8
GQA_KV_HEADS = 2
GQA_GROUP = GQA_HEADS // GQA_KV_HEADS
NA_HEADS = 16
NA_WIN_ROWS = 8
NA_WIN_COLS = 16
D_FF = 4 * D_MODEL
ROPE_THETA = 10000.0
NORM_EPS = 1e-6
AB_IN = 3 * CONV_WIDTH + (GQA_HEADS + 2 * GQA_KV_HEADS) * HEAD_DIM
KV_W = GQA_KV_HEADS * HEAD_DIM
ATTN_SCALE = HEAD_DIM ** -0.5

NA_QROWS = 4
NA_SLAB_ROWS = NA_QROWS + NA_WIN_ROWS
NA_QBLK = NA_QROWS * GRID_W
NA_SLAB = NA_SLAB_ROWS * GRID_W
NA_NBLK = GRID_H // NA_QROWS
MASK_NEG = -1e30

V7X_VMEM_LIMIT = 56 * 1024 * 1024

BF16 = jnp.bfloat16
F32 = jnp.float32


def _cparams(n_axes):
    return pltpu.CompilerParams(
        dimension_semantics=("parallel",) + ("arbitrary",) * (n_axes - 1),
        vmem_limit_bytes=V7X_VMEM_LIMIT)


def _nt_dot(a, b):
    return lax.dot_general(a, b, (((1,), (1,)), ((), ())), preferred_element_type=F32)


NORM_ROWS = 16


def _ada_norm_rows(h_ref, x_ref, g_ref, shift_ref, scale_ref):
    gain = g_ref[...] * (1.0 + scale_ref[...])
    shift = shift_ref[...]

    def body(c, carry):
        rows = pl.ds(pl.multiple_of(c * NORM_ROWS, NORM_ROWS), NORM_ROWS)
        x = x_ref[rows, :]
        ms = jnp.mean(x * x, axis=-1, keepdims=True)
        h_ref[rows, :] = (x * lax.rsqrt(ms + NORM_EPS) * gain + shift).astype(h_ref.dtype)
        return carry

    lax.fori_loop(0, x_ref.shape[0] // NORM_ROWS, body, 0, unroll=8)


def _head_norm(x, g):
    ms = jnp.mean(x * x, axis=-1, keepdims=True)
    return x * lax.rsqrt(ms + NORM_EPS) * g


def _rope(x, cos, sin_lo, sin_hi):
    return x * cos + pltpu.roll(x, HEAD_DIM - 32, 1) * sin_lo + pltpu.roll(x, 32, 1) * sin_hi


def _softmax_pv(scores, values):
    m = scores[0].max(axis=-1, keepdims=True)
    for s in scores[1:]:
        m = jnp.maximum(m, s.max(axis=-1, keepdims=True))
    l = None
    o = None
    for s, v in zip(scores, values):
        p = jnp.exp(s - m)
        ls = jnp.sum(p, axis=-1, keepdims=True)
        os_ = jnp.dot(p.astype(BF16), v, preferred_element_type=F32)
        l = ls if l is None else l + ls
        o = os_ if o is None else o + os_
    return o / l


def _mod_kernel(c_ref, w_ref, b_ref, o_ref):
    c = c_ref[...]
    s = (c * jax.nn.sigmoid(c)).astype(BF16)
    o_ref[...] = jnp.dot(s, w_ref[...].astype(BF16), preferred_element_type=F32) + b_ref[...]


def _modulation(cond, mod_w, mod_b):
    depth, d, n = mod_w.shape
    tn = 1024
    rows = cond.shape[0]
    return pl.pallas_call(
        _mod_kernel,
        grid=(depth, n // tn),
        in_specs=[
            pl.BlockSpec((rows, d), lambda l, j: (0, 0)),
            pl.BlockSpec((None, d, tn), lambda l, j: (l, 0, j)),
            pl.BlockSpec((None, 1, tn), lambda l, j: (l, 0, j)),
        ],
        out_specs=pl.BlockSpec((None, rows, tn), lambda l, j: (l, 0, j)),
        out_shape=jax.ShapeDtypeStruct((depth, rows, n), F32),
        compiler_params=_cparams(2),
        name="modulation",
    )(cond, mod_w, mod_b.reshape(depth, 1, n))


def _mod_spec(which, row_of_tile):
    return pl.BlockSpec((None, None, 1, D_MODEL), lambda i, j: (row_of_tile(i), which, 0, 0))


Q_W = GQA_HEADS * HEAD_DIM
IN0_TN = Q_W + 2 * KV_W


def _in0_kernel(x_ref, g_ref, shift_ref, scale_ref, w_ref, qn_ref, kn_ref, *rest, rope):
    if rope:
        cos_ref, slo_ref, shi_ref, zc_ref, q_ref, kv_ref, h_scr = rest
    else:
        zc_ref, q_ref, kv_ref, h_scr = rest
    j = pl.program_id(1)
    n_zc = 3 * CONV_WIDTH // IN0_TN

    @pl.when(j == 0)
    def _():
        _ada_norm_rows(h_scr, x_ref, g_ref, shift_ref, scale_ref)

    acc = jnp.dot(h_scr[...], w_ref[...], preferred_element_type=F32)
    zc_ref[...] = acc.astype(zc_ref.dtype)

    def normed(blk, gain):
        y = _head_norm(blk, gain)
        if rope:
            y = _rope(y, cos_ref[...], slo_ref[...], shi_ref[...])
        return y

    @pl.when(j == n_zc)
    def _():
        for hh in range(GQA_HEADS):
            sl = slice(hh * HEAD_DIM, (hh + 1) * HEAD_DIM)
            q_ref[:, sl] = (normed(acc[:, sl], qn_ref[...]) * ATTN_SCALE).astype(q_ref.dtype)
        for hh in range(GQA_KV_HEADS):
            src = slice(Q_W + hh * HEAD_DIM, Q_W + (hh + 1) * HEAD_DIM)
            kv_ref[:, hh * HEAD_DIM:(hh + 1) * HEAD_DIM] = normed(acc[:, src], kn_ref[...]).astype(kv_ref.dtype)
        kv_ref[:, KV_W:] = acc[:, Q_W + KV_W:].astype(kv_ref.dtype)


def _in_proj0(x, norm_g, mods, row_of_tile, w, qn, kn, rope_tabs, kv_dtype, tm):
    t = x.shape[0]
    tn = IN0_TN
    rope = rope_tabs is not None
    vec = pl.BlockSpec((1, HEAD_DIM), lambda i, j: (0, 0))
    in_specs = [
        pl.BlockSpec((tm, D_MODEL), lambda i, j: (i, 0)),
        pl.BlockSpec((1, D_MODEL), lambda i, j: (0, 0)),
        _mod_spec(0, row_of_tile),
        _mod_spec(1, row_of_tile),
        pl.BlockSpec((D_MODEL, tn), lambda i, j: (0, j)),
        vec, vec,
    ]
    args = [x, norm_g, mods, mods, w, qn, kn]
    if rope:
        nt = DEC_SEQ // tm
        tab = pl.BlockSpec((tm, HEAD_DIM), lambda i, j: (i % nt, 0))
        in_specs += [tab, tab, tab]
        args += list(rope_tabs)
    return pl.pallas_call(
        functools.partial(_in0_kernel, rope=rope),
        grid=(t // tm, AB_IN // tn),
        in_specs=in_specs,
        out_specs=[
            pl.BlockSpec((tm, tn), lambda i, j: (i, j)),
            pl.BlockSpec((tm, Q_W), lambda i, j: (i, 0)),
            pl.BlockSpec((tm, 2 * KV_W), lambda i, j: (i, 0)),
        ],
        out_shape=[
            jax.ShapeDtypeStruct((t, AB_IN), BF16),
            jax.ShapeDtypeStruct((t, GQA_HEADS * HEAD_DIM), BF16),
            jax.ShapeDtypeStruct((t, 2 * KV_W), kv_dtype),
        ],
        scratch_shapes=[pltpu.VMEM((tm, D_MODEL), BF16)],
        compiler_params=_cparams(2),
        name="in_proj0",
    )(*args)


def _in1_kernel(x_ref, g_ref, shift_ref, scale_ref, w_ref, o_ref, h_scr, *, nb):
    j = pl.program_id(1)

    @pl.when(j == 0)
    def _():
        _ada_norm_rows(h_scr, x_ref, g_ref, shift_ref, scale_ref)

    acc = jnp.dot(h_scr[...], w_ref[...], preferred_element_type=F32)
    o_ref[...] = (acc * jnp.where(j < nb, ATTN_SCALE, 1.0)).astype(o_ref.dtype)


def _in_proj1(x, norm_g, mods, row_of_tile, w, out_dtype, tm, tn):
    t = x.shape[0]
    nb = D_MODEL // tn
    return pl.pallas_call(
        functools.partial(_in1_kernel, nb=nb),
        grid=(t // tm, 3 * nb),
        in_specs=[
            pl.BlockSpec((tm, D_MODEL), lambda i, j: (i, 0)),
            pl.BlockSpec((1, D_MODEL), lambda i, j: (0, 0)),
            _mod_spec(0, row_of_tile),
            _mod_spec(1, row_of_tile),
            pl.BlockSpec((D_MODEL, tn), lambda i, j: (0, j)),
        ],
        out_specs=pl.BlockSpec((None, tm, tn), lambda i, j: (j // nb, i, j % nb)),
        out_shape=jax.ShapeDtypeStruct((3, t, D_MODEL), out_dtype),
        scratch_shapes=[pltpu.VMEM((tm, D_MODEL), BF16)],
        compiler_params=_cparams(2),
        name="in_proj1",
    )(x, norm_g, mods, mods, w)


def _in1q_kernel(x_ref, g_ref, shift_ref, scale_ref, w_ref, q_ref, h_ref):
    @pl.when(pl.program_id(1) == 0)
    def _():
        _ada_norm_rows(h_ref, x_ref, g_ref, shift_ref, scale_ref)

    acc = jnp.dot(h_ref[...], w_ref[...], preferred_element_type=F32)
    q_ref[...] = (acc * ATTN_SCALE).astype(q_ref.dtype)


def _in_proj1_q(x, norm_g, mods, row_of_tile, w, tm, tn):
    t = x.shape[0]
    return pl.pallas_call(
        _in1q_kernel,
        grid=(t // tm, D_MODEL // tn),
        in_specs=[
            pl.BlockSpec((tm, D_MODEL), lambda i, j: (i, 0)),
            pl.BlockSpec((1, D_MODEL), lambda i, j: (0, 0)),
            _mod_spec(0, row_of_tile),
            _mod_spec(1, row_of_tile),
            pl.BlockSpec((D_MODEL, tn), lambda i, j: (0, j)),
        ],
        out_specs=[pl.BlockSpec((tm, tn), lambda i, j: (i, j)),
                   pl.BlockSpec((tm, D_MODEL), lambda i, j: (i, 0))],
        out_shape=[jax.ShapeDtypeStruct((t, D_MODEL), BF16), jax.ShapeDtypeStruct((t, D_MODEL), BF16)],
        compiler_params=_cparams(2),
        name="in_proj1_q",
    )(x, norm_g, mods, mods, w)


def _in1kv_kernel(h_ref, w_ref, flat_ref, heads_ref):
    acc = jnp.dot(h_ref[...], w_ref[...], preferred_element_type=F32)
    flat_ref[...] = acc.astype(flat_ref.dtype)
    heads_ref[...] = acc.reshape(heads_ref.shape)


def _in_proj1_kv(h, w, part, tm, tn):
    t = h.shape[0]
    nb = D_MODEL // tn
    hb = tn // HEAD_DIM
    return pl.pallas_call(
        _in1kv_kernel,
        grid=(t // tm, nb),
        in_specs=[
            pl.BlockSpec((tm, D_MODEL), lambda i, j: (i, 0)),
            pl.BlockSpec((D_MODEL, tn), lambda i, j: (0, part * nb + j)),
        ],
        out_specs=[pl.BlockSpec((tm, tn), lambda i, j: (i, j)),
                   pl.BlockSpec((tm, hb, HEAD_DIM), lambda i, j: (i, j, 0))],
        out_shape=[jax.ShapeDtypeStruct((t, D_MODEL), BF16),
                   jax.ShapeDtypeStruct((t, NA_HEADS, HEAD_DIM), F32)],
        compiler_params=_cparams(2),
        name="in_proj1_kv",
    )(h, w)


def _gated_conv(zc, prev_row, next_row, cw):
    c = CONV_WIDTH
    s = zc.shape[0]
    gb = zc[:, 0:c].astype(F32)
    u = zc[:, c:2 * c].astype(F32) * zc[:, 2 * c:3 * c].astype(F32)
    row = lax.broadcasted_iota(jnp.int32, u.shape, 0)
    prev = jnp.where(row == 0, prev_row, pltpu.roll(u, 1, 0))
    nxt = jnp.where(row == s - 1, next_row, pltpu.roll(u, s - 1, 0))
    return gb * (prev * cw[0:1] + u * cw[1:2] + nxt * cw[2:3])


def _attn0_ctx_kernel(zc_ref, q_ref, kv_ref, cw_ref, o_ref):
    zero = jnp.zeros((1, CONV_WIDTH), F32)
    a = _gated_conv(zc_ref[...], zero, zero, cw_ref[...])
    o_ref[:, 0:CONV_WIDTH] = a.astype(o_ref.dtype)
    for g in range(GQA_KV_HEADS):
        k = kv_ref[:, g * HEAD_DIM:(g + 1) * HEAD_DIM].astype(BF16)
        v = kv_ref[:, KV_W + g * HEAD_DIM:KV_W + (g + 1) * HEAD_DIM].astype(BF16)
        for hh in range(GQA_GROUP):
            h = g * GQA_GROUP + hh
            q = q_ref[:, h * HEAD_DIM:(h + 1) * HEAD_DIM]
            o = _softmax_pv([_nt_dot(q, k)], [v])
            o_ref[:, CONV_WIDTH + h * HEAD_DIM:CONV_WIDTH + (h + 1) * HEAD_DIM] = o.astype(o_ref.dtype)


def _attn0_ctx(zc, q, kv, cw):
    t = zc.shape[0]
    return pl.pallas_call(
        _attn0_ctx_kernel,
        grid=(t // SEQ,),
        in_specs=[
            pl.BlockSpec((SEQ, 3 * CONV_WIDTH), lambda b: (b, 0)),
            pl.BlockSpec((SEQ, GQA_HEADS * HEAD_DIM), lambda b: (b, 0)),
            pl.BlockSpec((SEQ, 2 * KV_W), lambda b: (b, 0)),
            pl.BlockSpec((3, CONV_WIDTH), lambda b: (0, 0)),
        ],
        out_specs=pl.BlockSpec((SEQ, D_MODEL), lambda b: (b, 0)),
        out_shape=jax.ShapeDtypeStruct((t, D_MODEL), BF16),
        compiler_params=_cparams(1),
        name="mixer0_ctx",
    )(zc, q, kv, cw)


_HALO = 16


def _attn0_lat_kernel(zc_ref, zp_ref, zn_ref, q_ref, kv_ref, ck_ref, cv_ref, cw_ref, o_ref, *, nqb):
    qb = pl.program_id(1)
    c = CONV_WIDTH
    up = zp_ref[:, c:2 * c].astype(F32) * zp_ref[:, 2 * c:3 * c].astype(F32)
    un = zn_ref[:, c:2 * c].astype(F32) * zn_ref[:, 2 * c:3 * c].astype(F32)
    up = jnp.where(qb == 0, 0.0, up[_HALO - 1:_HALO, :])
    un = jnp.where(qb == nqb - 1, 0.0, un[0:1, :])
    a = _gated_conv(zc_ref[...], up, un, cw_ref[...])
    o_ref[:, 0:c] = a.astype(o_ref.dtype)
    for g in range(GQA_KV_HEADS):
        ks = slice(g * HEAD_DIM, (g + 1) * HEAD_DIM)
        vs = slice(KV_W + g * HEAD_DIM, KV_W + (g + 1) * HEAD_DIM)
        k = kv_ref[:, ks]
        v = kv_ref[:, vs]
        ck = ck_ref[:, ks].astype(BF16)
        cv = cv_ref[:, ks].astype(BF16)
        for hh in range(GQA_GROUP):
            h = g * GQA_GROUP + hh
            q = q_ref[:, h * HEAD_DIM:(h + 1) * HEAD_DIM]
            o = _softmax_pv([_nt_dot(q, ck), _nt_dot(q, k)], [cv, v])
            o_ref[:, c + h * HEAD_DIM:c + (h + 1) * HEAD_DIM] = o.astype(o_ref.dtype)


def _attn0_lat(zc, q, kv, ck, cv, cw, tq):
    nb, t, _ = zc.shape
    nqb = t // tq
    hb = tq // _HALO
    return pl.pallas_call(
        functools.partial(_attn0_lat_kernel, nqb=nqb),
        grid=(nb, nqb),
        in_specs=[
            pl.BlockSpec((None, tq, 3 * CONV_WIDTH), lambda b, i: (b, i, 0)),
            pl.BlockSpec((None, _HALO, 3 * CONV_WIDTH), lambda b, i: (b, jnp.maximum(i * hb - 1, 0), 0)),
            pl.BlockSpec((None, _HALO, 3 * CONV_WIDTH),
                         lambda b, i: (b, jnp.minimum((i + 1) * hb, t // _HALO - 1), 0)),
            pl.BlockSpec((None, tq, GQA_HEADS * HEAD_DIM), lambda b, i: (b, i, 0)),
            pl.BlockSpec((None, t, 2 * KV_W), lambda b, i: (b, 0, 0)),
            pl.BlockSpec((None, PAST_LEN, KV_W), lambda b, i: (b, 0, 0)),
            pl.BlockSpec((None, PAST_LEN, KV_W), lambda b, i: (b, 0, 0)),
            pl.BlockSpec((3, CONV_WIDTH), lambda b, i: (0, 0)),
        ],
        out_specs=pl.BlockSpec((None, tq, D_MODEL), lambda b, i: (b, i, 0)),
        out_shape=jax.ShapeDtypeStruct((nb, t, D_MODEL), BF16),
        compiler_params=_cparams(2),
        name="mixer0_lat",
    )(zc, zc, zc, q, kv, ck, cv, cw)


def _attn1_ctx_kernel(q_ref, k_ref, v_ref, o_ref):
    for h in range(NA_HEADS):
        sl = slice(h * HEAD_DIM, (h + 1) * HEAD_DIM)
        o = _softmax_pv([_nt_dot(q_ref[:, sl], k_ref[:, sl])], [v_ref[:, sl]])
        o_ref[:, sl] = o.astype(o_ref.dtype)


def _attn1_ctx(q, k, v):
    t = q.shape[0]
    blk = pl.BlockSpec((SEQ, D_MODEL), lambda b: (b, 0))
    return pl.pallas_call(
        _attn1_ctx_kernel,
        grid=(t // SEQ,),
        in_specs=[blk, blk, blk],
        out_specs=blk,
        out_shape=jax.ShapeDtypeStruct((t, D_MODEL), BF16),
        compiler_params=_cparams(1),
        name="mixer1_ctx",
    )(q, k, v)


def _na_slab_row(blk):
    return min(max(NA_QROWS * blk - NA_WIN_ROWS // 2, 0), GRID_H - NA_SLAB_ROWS)


NA_NROW = 2 * NA_WIN_ROWS - 1
NA_MASKED = NA_NROW
NA_FIRST_IN = NA_WIN_ROWS - 1 - NA_WIN_ROWS // 2
NA_LAST_IN = NA_FIRST_IN + NA_WIN_ROWS - 1
NA_PIECE_PAIRS = ([(m, m + 1) for m in range(NA_NROW - 1)]
                  + [(NA_MASKED, NA_FIRST_IN), (NA_LAST_IN, NA_MASKED)])


def _na_piece(blk, dr, jp):
    r = NA_QROWS * blk + dr
    ks = min(max(r - NA_WIN_ROWS // 2, 0), GRID_H - NA_WIN_ROWS)
    pair = []
    for j in (2 * jp, 2 * jp + 1):
        krow = _na_slab_row(blk) + j
        pair.append(krow - r + NA_WIN_ROWS - 1 if ks <= krow < ks + NA_WIN_ROWS else NA_MASKED)
    pair = tuple(pair)
    if pair == (NA_MASKED, NA_MASKED):
        return None
    return NA_PIECE_PAIRS.index(pair)


NA_PAD = GRID_W - NA_WIN_COLS


def _na_bias_table(rel_bias):
    return jnp.pad(rel_bias, ((0, 0), (0, 1), (NA_PAD, 2 * GRID_W - (2 * NA_WIN_COLS - 1) - NA_PAD)),
                   constant_values=MASK_NEG)


def _na_fill_pieces(table_ref, pieces_ref):
    shape = (GRID_W, 2 * GRID_W)
    c = lax.broadcasted_iota(jnp.int32, shape, 0)
    lane = lax.broadcasted_iota(jnp.int32, shape, 1)
    kc = lane & (GRID_W - 1)
    cs = jnp.clip(c - NA_WIN_COLS // 2, 0, GRID_W - NA_WIN_COLS)
    in_window = (kc >= cs) & (kc < cs + NA_WIN_COLS)

    def expand(a, shift):
        row = jnp.broadcast_to(table_ref[a:a + 1, :], shape)
        return pltpu.roll(row, shift, 1, stride=1, stride_axis=0)

    left_shift = 2 * GRID_W - (GRID_W - 1)
    right_shift = 1
    for m, (a_left, a_right) in enumerate(NA_PIECE_PAIRS):
        both = jnp.where(lane < GRID_W, expand(a_left, left_shift), expand(a_right, right_shift))
        pieces_ref[m] = jnp.where(in_window, both, MASK_NEG)


def _na_bias_block(pieces_ref, blk):
    rows = []
    for dr in range(NA_QROWS):
        cols = []
        for jp in range(NA_SLAB_ROWS // 2):
            m = _na_piece(blk, dr, jp)
            cols.append(jnp.full((GRID_W, 2 * GRID_W), MASK_NEG, F32) if m is None else pieces_ref[m])
        rows.append(jnp.concatenate(cols, axis=1))
    return jnp.concatenate(rows, axis=0)


def _attn1_lat_kernel(q_ref, k_ref, v_ref, ck_ref, cv_ref, table_ref, o_ref, pieces_ref):
    _na_fill_pieces(table_ref, pieces_ref)
    ck = ck_ref[...].astype(BF16)
    cv = cv_ref[...].astype(BF16)
    for blk in range(NA_NBLK):
        rows = slice(blk * NA_QBLK, (blk + 1) * NA_QBLK)
        s0 = _na_slab_row(blk) * GRID_W
        slab = slice(s0, s0 + NA_SLAB)
        q = q_ref[rows, :]
        s_loc = _nt_dot(q, k_ref[slab, :]) + _na_bias_block(pieces_ref, blk)
        o = _softmax_pv([s_loc, _nt_dot(q, ck)], [v_ref[slab, :], cv])
        o_ref[rows, :] = o.astype(o_ref.dtype)


def _attn1_lat(qkv, ck, cv, table):
    _, nb, t, _ = qkv.shape
    part = lambda p: pl.BlockSpec((None, None, t, HEAD_DIM), lambda b, h: (p, b, 0, h))
    head = pl.BlockSpec((None, t, HEAD_DIM), lambda b, h: (b, 0, h))
    ctx = pl.BlockSpec((None, PAST_LEN, HEAD_DIM), lambda b, h: (b, 0, h))
    return pl.pallas_call(
        _attn1_lat_kernel,
        grid=(nb, NA_HEADS),
        in_specs=[part(0), part(1), part(2), ctx, ctx,
                  pl.BlockSpec((None, NA_NROW + 1, 2 * GRID_W), lambda b, h: (h, 0, 0))],
        out_specs=head,
        out_shape=jax.ShapeDtypeStruct((nb, t, D_MODEL), BF16),
        scratch_shapes=[pltpu.VMEM((len(NA_PIECE_PAIRS), GRID_W, 2 * GRID_W), F32)],
        compiler_params=_cparams(2),
        name="mixer1_lat",
    )(qkv, qkv, qkv, ck, cv, table)


def _out_proj_kernel(m_ref, w_ref, x_ref, gate_ref, o_ref):
    acc = jnp.dot(m_ref[...], w_ref[...], preferred_element_type=F32)
    o_ref[...] = x_ref[...] + gate_ref[...] * acc


def _out_proj(mix, w, x, mods, row_of_tile, tm, tn):
    t, k = mix.shape
    return pl.pallas_call(
        _out_proj_kernel,
        grid=(t // tm, D_MODEL // tn),
        in_specs=[
            pl.BlockSpec((tm, k), lambda i, j: (i, 0)),
            pl.BlockSpec((k, tn), lambda i, j: (0, j)),
            pl.BlockSpec((tm, tn), lambda i, j: (i, j)),
            pl.BlockSpec((None, None, 1, tn), lambda i, j: (row_of_tile(i), 2, 0, j)),
        ],
        out_specs=pl.BlockSpec((tm, tn), lambda i, j: (i, j)),
        out_shape=jax.ShapeDtypeStruct((t, D_MODEL), F32),
        compiler_params=_cparams(2),
        name="out_proj",
    )(mix, w, x, mods)


def _mlp_kernel(x_ref, g_ref, shift_ref, scale_ref, gate_ref, w1_ref, w2_ref, fg_ref, o_ref, h_scr,
                *, nf, final):
    f = pl.program_id(1)

    @pl.when(f == 0)
    def _():
        _ada_norm_rows(h_scr, x_ref, g_ref, shift_ref, scale_ref)
        o_ref[...] = jnp.zeros_like(o_ref)

    u = jnp.dot(h_scr[...], w1_ref[...], preferred_element_type=F32)
    u = jnp.square(jnp.maximum(u, 0.0)).astype(BF16)
    o_ref[...] += jnp.dot(u, w2_ref[...], preferred_element_type=F32)

    @pl.when(f == nf - 1)
    def _():
        y = x_ref[...] + gate_ref[...] * o_ref[...]
        if final:
            ms = jnp.mean(y * y, axis=-1, keepdims=True)
            y = y * lax.rsqrt(ms + NORM_EPS) * fg_ref[...]
        o_ref[...] = y


def _mlp(x, norm_g, mods, row_of_tile, w1, w2, layer, final_g, final, tm, tf):
    t = x.shape[0]
    nf = D_FF // tf

    def mod(which):
        return pl.BlockSpec((None, None, 1, D_MODEL), lambda i, f: (row_of_tile(i), which, 0, 0))

    return pl.pallas_call(
        functools.partial(_mlp_kernel, nf=nf, final=final),
        grid=(t // tm, nf),
        in_specs=[
            pl.BlockSpec((tm, D_MODEL), lambda i, f: (i, 0)),
            pl.BlockSpec((1, D_MODEL), lambda i, f: (0, 0)),
            mod(3), mod(4), mod(5),
            pl.BlockSpec((None, D_MODEL, tf), lambda i, f: (layer, 0, f)),
            pl.BlockSpec((None, tf, D_MODEL), lambda i, f: (layer, f, 0)),
            pl.BlockSpec((1, D_MODEL), lambda i, f: (0, 0)),
        ],
        out_specs=pl.BlockSpec((tm, D_MODEL), lambda i, f: (i, 0)),
        out_shape=jax.ShapeDtypeStruct((t, D_MODEL), F32),
        scratch_shapes=[pltpu.VMEM((tm, D_MODEL), BF16)],
        compiler_params=_cparams(2),
        name="mlp",
    )(x, norm_g, mods, mods, mods, w1, w2, final_g)


def _rope_tables():
    t = np.arange(DEC_SEQ)
    half = HEAD_DIM // 4
    inv = ROPE_THETA ** (-np.arange(half, dtype=np.float32) / half)
    ang_r = (t // GRID_W).astype(np.float32)[:, None] * inv
    ang_c = (t % GRID_W).astype(np.float32)[:, None] * inv
    zero = np.zeros_like(ang_r)
    cos = np.concatenate([np.cos(ang_r)] * 2 + [np.cos(ang_c)] * 2, axis=-1)
    sin_lo = np.concatenate([-np.sin(ang_r), zero, -np.sin(ang_c), zero], axis=-1)
    sin_hi = np.concatenate([zero, np.sin(ang_r), zero, np.sin(ang_c)], axis=-1)
    return tuple(jnp.asarray(a, F32) for a in (cos, sin_lo, sin_hi))


def kernel(x_prompt, x_sample, cache_attn_k, cache_attn_v, cache_na_k, cache_na_v, c, c_ctx, mod_w, mod_b,
           norm1_g, norm2_g, ab_w_in, ab_conv_w, ab_q_norm, ab_k_norm, ab_w_out, na_w_qkv, na_rel_bias,
           na_w_out, mlp_w1, mlp_w2, final_norm_g):
    n_ctx = BATCH * SEQ
    xp = x_prompt.reshape(n_ctx, D_MODEL)
    xs = x_sample.reshape(DEC_BATCH * DEC_SEQ, D_MODEL)

    cond = jnp.concatenate([c_ctx[None, :], c, jnp.zeros((8 - 1 - DEC_BATCH, D_MODEL), F32)], axis=0)
    mods = _modulation(cond, mod_w, mod_b).reshape(2, 8, 6, 1, D_MODEL)

    tm_in, tm_out, tm_mlp, tf = 512, 1024, 512, 1024
    w1_all = mlp_w1.astype(BF16)
    w2_all = mlp_w2.astype(BF16)

    def ctx_row(i):
        return 0

    def lat_row(tm):
        return lambda i: 1 + i // (DEC_SEQ // tm)

    rope_tabs = _rope_tables()
    fg = final_norm_g.reshape(1, D_MODEL)

    m0 = mods[0]
    g1 = norm1_g[0].reshape(1, D_MODEL)
    w_in = ab_w_in[0].astype(BF16)
    qn = ab_q_norm[0].reshape(1, HEAD_DIM)
    kn = ab_k_norm[0].reshape(1, HEAD_DIM)
    cw = ab_conv_w[0]
    w_out = ab_w_out[0].astype(BF16)

    zc_p, q_p, kv_p = _in_proj0(xp, g1, m0, ctx_row, w_in, qn, kn, None, F32, tm_in)
    zc_s, q_s, kv_s = _in_proj0(xs, g1, m0, lat_row(tm_in), w_in, qn, kn, rope_tabs, BF16, tm_in)
    new_attn_k = kv_p[:, :KV_W].reshape(BATCH, 1, SEQ, GQA_KV_HEADS, HEAD_DIM)
    new_attn_v = kv_p[:, KV_W:].reshape(BATCH, 1, SEQ, GQA_KV_HEADS, HEAD_DIM)

    mix_p = _attn0_ctx(zc_p, q_p, kv_p, cw)
    lat3 = lambda a: a.reshape(DEC_BATCH, DEC_SEQ, a.shape[-1])
    mix_s = _attn0_lat(lat3(zc_s), lat3(q_s), lat3(kv_s),
                       cache_attn_k[:, 0].reshape(DEC_BATCH, PAST_LEN, KV_W),
                       cache_attn_v[:, 0].reshape(DEC_BATCH, PAST_LEN, KV_W), cw, 256)
    mix_s = mix_s.reshape(DEC_BATCH * DEC_SEQ, D_MODEL)

    xp = _out_proj(mix_p, w_out, xp, m0, ctx_row, tm_out, 1024)
    xs = _out_proj(mix_s, w_out, xs, m0, lat_row(tm_out), tm_out, 1024)

    g2 = norm2_g[0].reshape(1, D_MODEL)
    xp = _mlp(xp, g2, m0, ctx_row, w1_all, w2_all, 0, fg, False, tm_mlp, tf)
    xs = _mlp(xs, g2, m0, lat_row(tm_mlp), w1_all, w2_all, 0, fg, False, tm_mlp, tf)

    m1 = mods[1]
    g1 = norm1_g[1].reshape(1, D_MODEL)
    w_qkv = na_w_qkv[0].astype(BF16)
    w_out = na_w_out[0].astype(BF16)

    q_p, h_p = _in_proj1_q(xp, g1, m1, ctx_row, w_qkv, 1024, 1024)
    k_p, k_heads = _in_proj1_kv(h_p, w_qkv, 1, 1024, 1024)
    v_p, v_heads = _in_proj1_kv(h_p, w_qkv, 2, 1024, 1024)
    qkv_s = _in_proj1(xs, g1, m1, lat_row(1024), w_qkv, BF16, 1024, 1024)
    new_na_k = k_heads.reshape(BATCH, 1, SEQ, NA_HEADS, HEAD_DIM)
    new_na_v = v_heads.reshape(BATCH, 1, SEQ, NA_HEADS, HEAD_DIM)

    mix_p = _attn1_ctx(q_p, k_p, v_p)
    mix_s = _attn1_lat(qkv_s.reshape(3, DEC_BATCH, DEC_SEQ, D_MODEL),
                       cache_na_k[:, 0].reshape(DEC_BATCH, PAST_LEN, D_MODEL),
                       cache_na_v[:, 0].reshape(DEC_BATCH, PAST_LEN, D_MODEL),
                       _na_bias_table(na_rel_bias[0]))
    mix_s = mix_s.reshape(DEC_BATCH * DEC_SEQ, D_MODEL)

    xp = _out_proj(mix_p, w_out, xp, m1, ctx_row, tm_out, 1024)
    xs = _out_proj(mix_s, w_out, xs, m1, lat_row(tm_out), tm_out, 1024)

    g2 = norm2_g[1].reshape(1, D_MODEL)
    yp = _mlp(xp, g2, m1, ctx_row, w1_all, w2_all, 1, fg, True, tm_mlp, tf)
    ys = _mlp(xs, g2, m1, lat_row(tm_mlp), w1_all, w2_all, 1, fg, True, tm_mlp, tf)

    return (yp.reshape(BATCH, SEQ, D_MODEL), ys.reshape(DEC_BATCH, DEC_SEQ, D_MODEL),
            new_attn_k, new_attn_v, new_na_k, new_na_v)
```

```python
import functools
import math

import numpy as np
import jax
import jax.numpy as jnp
from jax import lax
from jax.experimental import pallas as pl
from jax.experimental.pallas import tpu as pltpu

D_MODEL = 2048
BATCH = 32
SEQ = 256
DEC_BATCH = 2
DEC_SEQ = 2048
PAST_LEN = 512
GRID_W = 64
GRID_H = DEC_SEQ // GRID_W
HEAD_DIM = 128
CONV_WIDTH = D_MODEL // 2
GQA_HEADS = 8
GQA_KV_HEADS = 2
GQA_GROUP = GQA_HEADS // GQA_KV_HEADS
NA_HEADS = 16
NA_WIN_ROWS = 8
NA_WIN_COLS = 16
D_FF = 4 * D_MODEL
ROPE_THETA = 10000.0
NORM_EPS = 1e-6
AB_IN = 3 * CONV_WIDTH + (GQA_HEADS + 2 * GQA_KV_HEADS) * HEAD_DIM
KV_W = GQA_KV_HEADS * HEAD_DIM
LOG2E = math.log2(math.e)
ATTN_SCALE = HEAD_DIM ** -0.5 * LOG2E

NA_QROWS = 4
NA_SLAB_ROWS = NA_QROWS + NA_WIN_ROWS
NA_QBLK = NA_QROWS * GRID_W
NA_SLAB = NA_SLAB_ROWS * GRID_W
NA_NBLK = GRID_H // NA_QROWS
MASK_NEG = -1e30

V7X_VMEM_LIMIT = 56 * 1024 * 1024

BF16 = jnp.bfloat16
F32 = jnp.float32


def _cparams(n_axes):
    return pltpu.CompilerParams(
        dimension_semantics=("parallel",) + ("arbitrary",) * (n_axes - 1),
        vmem_limit_bytes=V7X_VMEM_LIMIT)


def _nt_dot(a, b):
    return lax.dot_general(a, b, (((1,), (1,)), ((), ())), preferred_element_type=F32)


NORM_ROWS = 16


def _ada_norm_rows(h_ref, x_ref, g_ref, shift_ref, scale_ref):
    gain = g_ref[...] * (1.0 + scale_ref[...])
    shift = shift_ref[...]

    def body(c, carry):
        rows = pl.ds(pl.multiple_of(c * NORM_ROWS, NORM_ROWS), NORM_ROWS)
        x = x_ref[rows, :]
        ms = jnp.mean(x * x, axis=-1, keepdims=True)
        h_ref[rows, :] = (x * lax.rsqrt(ms + NORM_EPS) * gain + shift).astype(h_ref.dtype)
        return carry

    lax.fori_loop(0, x_ref.shape[0] // NORM_ROWS, body, 0, unroll=8)


def _head_norm(x, g):
    ms = jnp.mean(x * x, axis=-1, keepdims=True)
    return x * lax.rsqrt(ms + NORM_EPS) * g


def _rope(x, cos, sin_lo, sin_hi):
    return x * cos + pltpu.roll(x, HEAD_DIM - 32, 1) * sin_lo + pltpu.roll(x, 32, 1) * sin_hi


def _softmax_pv(scores, values):
    m = scores[0].max(axis=-1, keepdims=True)
    for s in scores[1:]:
        m = jnp.maximum(m, s.max(axis=-1, keepdims=True))
    l = None
    o = None
    for s, v in zip(scores, values):
        p = jnp.exp2(s - m)
        ls = jnp.sum(p, axis=-1, keepdims=True)
        os_ = jnp.dot(p.astype(BF16), v, preferred_element_type=F32)
        l = ls if l is None else l + ls
        o = os_ if o is None else o + os_
    return o / l


def _mod_kernel(c_ref, w_ref, b_ref, o_ref):
    c = c_ref[...]
    s = (c * jax.nn.sigmoid(c)).astype(BF16)
    o_ref[...] = jnp.dot(s, w_ref[...].astype(BF16), preferred_element_type=F32) + b_ref[...]


def _modulation(cond, mod_w, mod_b):
    depth, d, n = mod_w.shape
    tn = 1024
    rows = cond.shape[0]
    return pl.pallas_call(
        _mod_kernel,
        grid=(depth, n // tn),
        in_specs=[
            pl.BlockSpec((rows, d), lambda l, j: (0, 0)),
            pl.BlockSpec((None, d, tn), lambda l, j: (l, 0, j)),
            pl.BlockSpec((None, 1, tn), lambda l, j: (l, 0, j)),
        ],
        out_specs=pl.BlockSpec((None, rows, tn), lambda l, j: (l, 0, j)),
        out_shape=jax.ShapeDtypeStruct((depth, rows, n), F32),
        compiler_params=_cparams(2),
        name="modulation",
    )(cond, mod_w, mod_b.reshape(depth, 1, n))


def _mod_spec(which, row_of_tile):
    return pl.BlockSpec((None, None, 1, D_MODEL), lambda i, j: (row_of_tile(i), which, 0, 0))


Q_W = GQA_HEADS * HEAD_DIM
IN0_TN = Q_W + 2 * KV_W


def _in0_kernel(x_ref, g_ref, shift_ref, scale_ref, w_ref, qn_ref, kn_ref, *rest, rope):
    if rope:
        cos_ref, slo_ref, shi_ref, zc_ref, q_ref, kv_ref, h_scr = rest
    else:
        zc_ref, q_ref, kv_ref, h_scr = rest
    j = pl.program_id(1)
    n_zc = 3 * CONV_WIDTH // IN0_TN

    @pl.when(j == 0)
    def _():
        _ada_norm_rows(h_scr, x_ref, g_ref, shift_ref, scale_ref)

    acc = jnp.dot(h_scr[...], w_ref[...], preferred_element_type=F32)
    zc_ref[...] = acc.astype(zc_ref.dtype)

    def normed(blk, gain):
        y = _head_norm(blk, gain)
        if rope:
            y = _rope(y, cos_ref[...], slo_ref[...], shi_ref[...])
        return y

    @pl.when(j == n_zc)
    def _():
        for hh in range(GQA_HEADS):
            sl = slice(hh * HEAD_DIM, (hh + 1) * HEAD_DIM)
            q_ref[:, sl] = (normed(acc[:, sl], qn_ref[...]) * ATTN_SCALE).astype(q_ref.dtype)
        for hh in range(GQA_KV_HEADS):
            src = slice(Q_W + hh * HEAD_DIM, Q_W + (hh + 1) * HEAD_DIM)
            kv_ref[:, hh * HEAD_DIM:(hh + 1) * HEAD_DIM] = normed(acc[:, src], kn_ref[...]).astype(kv_ref.dtype)
        kv_ref[:, KV_W:] = acc[:, Q_W + KV_W:].astype(kv_ref.dtype)


def _in_proj0(x, norm_g, mods, row_of_tile, w, qn, kn, rope_tabs, kv_dtype, tm):
    t = x.shape[0]
    tn = IN0_TN
    rope = rope_tabs is not None
    vec = pl.BlockSpec((1, HEAD_DIM), lambda i, j: (0, 0))
    in_specs = [
        pl.BlockSpec((tm, D_MODEL), lambda i, j: (i, 0)),
        pl.BlockSpec((1, D_MODEL), lambda i, j: (0, 0)),
        _mod_spec(0, row_of_tile),
        _mod_spec(1, row_of_tile),
        pl.BlockSpec((D_MODEL, tn), lambda i, j: (0, j)),
        vec, vec,
    ]
    args = [x, norm_g, mods, mods, w, qn, kn]
    if rope:
        nt = DEC_SEQ // tm
        tab = pl.BlockSpec((tm, HEAD_DIM), lambda i, j: (i % nt, 0))
        in_specs += [tab, tab, tab]
        args += list(rope_tabs)
    return pl.pallas_call(
        functools.partial(_in0_kernel, rope=rope),
        grid=(t // tm, AB_IN // tn),
        in_specs=in_specs,
        out_specs=[
            pl.BlockSpec((tm, tn), lambda i, j: (i, j)),
            pl.BlockSpec((tm, Q_W), lambda i, j: (i, 0)),
            pl.BlockSpec((tm, 2 * KV_W), lambda i, j: (i, 0)),
        ],
        out_shape=[
            jax.ShapeDtypeStruct((t, AB_IN), BF16),
            jax.ShapeDtypeStruct((t, GQA_HEADS * HEAD_DIM), BF16),
            jax.ShapeDtypeStruct((t, 2 * KV_W), kv_dtype),
        ],
        scratch_shapes=[pltpu.VMEM((tm, D_MODEL), BF16)],
        compiler_params=_cparams(2),
        name="in_proj0",
    )(*args)


def _in1_kernel(x_ref, g_ref, shift_ref, scale_ref, w_ref, o_ref, h_scr, *, nb):
    j = pl.program_id(1)

    @pl.when(j == 0)
    def _():
        _ada_norm_rows(h_scr, x_ref, g_ref, shift_ref, scale_ref)

    acc = jnp.dot(h_scr[...], w_ref[...], preferred_element_type=F32)
    o_ref[...] = (acc * jnp.where(j < nb, ATTN_SCALE, 1.0)).astype(o_ref.dtype)


def _in_proj1(x, norm_g, mods, row_of_tile, w, out_dtype, tm, tn):
    t = x.shape[0]
    nb = D_MODEL // tn
    return pl.pallas_call(
        functools.partial(_in1_kernel, nb=nb),
        grid=(t // tm, 3 * nb),
        in_specs=[
            pl.BlockSpec((tm, D_MODEL), lambda i, j: (i, 0)),
            pl.BlockSpec((1, D_MODEL), lambda i, j: (0, 0)),
            _mod_spec(0, row_of_tile),
            _mod_spec(1, row_of_tile),
            pl.BlockSpec((D_MODEL, tn), lambda i, j: (0, j)),
        ],
        out_specs=pl.BlockSpec((None, tm, tn), lambda i, j: (j // nb, i, j % nb)),
        out_shape=jax.ShapeDtypeStruct((3, t, D_MODEL), out_dtype),
        scratch_shapes=[pltpu.VMEM((tm, D_MODEL), BF16)],
        compiler_params=_cparams(2),
        name="in_proj1",
    )(x, norm_g, mods, mods, w)


def _in1q_kernel(x_ref, g_ref, shift_ref, scale_ref, w_ref, q_ref, h_ref):
    @pl.when(pl.program_id(1) == 0)
    def _():
        _ada_norm_rows(h_ref, x_ref, g_ref, shift_ref, scale_ref)

    acc = jnp.dot(h_ref[...], w_ref[...], preferred_element_type=F32)
    q_ref[...] = (acc * ATTN_SCALE).astype(q_ref.dtype)


def _in_proj1_q(x, norm_g, mods, row_of_tile, w, tm, tn):
    t = x.shape[0]
    return pl.pallas_call(
        _in1q_kernel,
        grid=(t // tm, D_MODEL // tn),
        in_specs=[
            pl.BlockSpec((tm, D_MODEL), lambda i, j: (i, 0)),
            pl.BlockSpec((1, D_MODEL), lambda i, j: (0, 0)),
            _mod_spec(0, row_of_tile),
            _mod_spec(1, row_of_tile),
            pl.BlockSpec((D_MODEL, tn), lambda i, j: (0, j)),
        ],
        out_specs=[pl.BlockSpec((tm, tn), lambda i, j: (i, j)),
                   pl.BlockSpec((tm, D_MODEL), lambda i, j: (i, 0))],
        out_shape=[jax.ShapeDtypeStruct((t, D_MODEL), BF16), jax.ShapeDtypeStruct((t, D_MODEL), BF16)],
        compiler_params=_cparams(2),
        name="in_proj1_q",
    )(x, norm_g, mods, mods, w)


def _in1kv_kernel(h_ref, w_ref, flat_ref, heads_ref):
    acc = jnp.dot(h_ref[...], w_ref[...], preferred_element_type=F32)
    flat_ref[...] = acc.astype(flat_ref.dtype)
    heads_ref[...] = acc.reshape(heads_ref.shape)


def _in_proj1_kv(h, w, part, tm, tn):
    t = h.shape[0]
    nb = D_MODEL // tn
    hb = tn // HEAD_DIM
    return pl.pallas_call(
        _in1kv_kernel,
        grid=(t // tm, nb),
        in_specs=[
            pl.BlockSpec((tm, D_MODEL), lambda i, j: (i, 0)),
            pl.BlockSpec((D_MODEL, tn), lambda i, j: (0, part * nb + j)),
        ],
        out_specs=[pl.BlockSpec((tm, tn), lambda i, j: (i, j)),
                   pl.BlockSpec((tm, hb, HEAD_DIM), lambda i, j: (i, j, 0))],
        out_shape=[jax.ShapeDtypeStruct((t, D_MODEL), BF16),
                   jax.ShapeDtypeStruct((t, NA_HEADS, HEAD_DIM), F32)],
        compiler_params=_cparams(2),
        name="in_proj1_kv",
    )(h, w)


def _gated_conv(zc, prev_row, next_row, cw):
    c = CONV_WIDTH
    s = zc.shape[0]
    gb = zc[:, 0:c].astype(F32)
    u = zc[:, c:2 * c].astype(F32) * zc[:, 2 * c:3 * c].astype(F32)
    row = lax.broadcasted_iota(jnp.int32, u.shape, 0)
    prev = jnp.where(row == 0, prev_row, pltpu.roll(u, 1, 0))
    nxt = jnp.where(row == s - 1, next_row, pltpu.roll(u, s - 1, 0))
    return gb * (prev * cw[0:1] + u * cw[1:2] + nxt * cw[2:3])


def _attn0_ctx_kernel(zc_ref, q_ref, kv_ref, cw_ref, o_ref):
    zero = jnp.zeros((1, CONV_WIDTH), F32)
    a = _gated_conv(zc_ref[...], zero, zero, cw_ref[...])
    o_ref[:, 0:CONV_WIDTH] = a.astype(o_ref.dtype)
    for g in range(GQA_KV_HEADS):
        k = kv_ref[:, g * HEAD_DIM:(g + 1) * HEAD_DIM].astype(BF16)
        v = kv_ref[:, KV_W + g * HEAD_DIM:KV_W + (g + 1) * HEAD_DIM].astype(BF16)
        for hh in range(GQA_GROUP):
            h = g * GQA_GROUP + hh
            q = q_ref[:, h * HEAD_DIM:(h + 1) * HEAD_DIM]
            o = _softmax_pv([_nt_dot(q, k)], [v])
            o_ref[:, CONV_WIDTH + h * HEAD_DIM:CONV_WIDTH + (h + 1) * HEAD_DIM] = o.astype(o_ref.dtype)


def _attn0_ctx(zc, q, kv, cw):
    t = zc.shape[0]
    return pl.pallas_call(
        _attn0_ctx_kernel,
        grid=(t // SEQ,),
        in_specs=[
            pl.BlockSpec((SEQ, 3 * CONV_WIDTH), lambda b: (b, 0)),
            pl.BlockSpec((SEQ, GQA_HEADS * HEAD_DIM), lambda b: (b, 0)),
            pl.BlockSpec((SEQ, 2 * KV_W), lambda b: (b, 0)),
            pl.BlockSpec((3, CONV_WIDTH), lambda b: (0, 0)),
        ],
        out_specs=pl.BlockSpec((SEQ, D_MODEL), lambda b: (b, 0)),
        out_shape=jax.ShapeDtypeStruct((t, D_MODEL), BF16),
        compiler_params=_cparams(1),
        name="mixer0_ctx",
    )(zc, q, kv, cw)


_HALO = 16


def _attn0_lat_kernel(zc_ref, zp_ref, zn_ref, q_ref, kv_ref, ck_ref, cv_ref, cw_ref, o_ref, *, nqb):
    qb = pl.program_id(1)
    c = CONV_WIDTH
    up = zp_ref[:, c:2 * c].astype(F32) * zp_ref[:, 2 * c:3 * c].astype(F32)
    un = zn_ref[:, c:2 * c].astype(F32) * zn_ref[:, 2 * c:3 * c].astype(F32)
    up = jnp.where(qb == 0, 0.0, up[_HALO - 1:_HALO, :])
    un = jnp.where(qb == nqb - 1, 0.0, un[0:1, :])
    a = _gated_conv(zc_ref[...], up, un, cw_ref[...])
    o_ref[:, 0:c] = a.astype(o_ref.dtype)
    for g in range(GQA_KV_HEADS):
        ks = slice(g * HEAD_DIM, (g + 1) * HEAD_DIM)
        vs = slice(KV_W + g * HEAD_DIM, KV_W + (g + 1) * HEAD_DIM)
        k = kv_ref[:, ks]
        v = kv_ref[:, vs]
        ck = ck_ref[:, ks].astype(BF16)
        cv = cv_ref[:, ks].astype(BF16)
        for hh in range(GQA_GROUP):
            h = g * GQA_GROUP + hh
            q = q_ref[:, h * HEAD_DIM:(h + 1) * HEAD_DIM]
            o = _softmax_pv([_nt_dot(q, ck), _nt_dot(q, k)], [cv, v])
            o_ref[:, c + h * HEAD_DIM:c + (h + 1) * HEAD_DIM] = o.astype(o_ref.dtype)


def _attn0_lat(zc, q, kv, ck, cv, cw, tq):
    nb, t, _ = zc.shape
    nqb = t // tq
    hb = tq // _HALO
    return pl.pallas_call(
        functools.partial(_attn0_lat_kernel, nqb=nqb),
        grid=(nb, nqb),
        in_specs=[
            pl.BlockSpec((None, tq, 3 * CONV_WIDTH), lambda b, i: (b, i, 0)),
            pl.BlockSpec((None, _HALO, 3 * CONV_WIDTH), lambda b, i: (b, jnp.maximum(i * hb - 1, 0), 0)),
            pl.BlockSpec((None, _HALO, 3 * CONV_WIDTH),
                         lambda b, i: (b, jnp.minimum((i + 1) * hb, t // _HALO - 1), 0)),
            pl.BlockSpec((None, tq, GQA_HEADS * HEAD_DIM), lambda b, i: (b, i, 0)),
            pl.BlockSpec((None, t, 2 * KV_W), lambda b, i: (b, 0, 0)),
            pl.BlockSpec((None, PAST_LEN, KV_W), lambda b, i: (b, 0, 0)),
            pl.BlockSpec((None, PAST_LEN, KV_W), lambda b, i: (b, 0, 0)),
            pl.BlockSpec((3, CONV_WIDTH), lambda b, i: (0, 0)),
        ],
        out_specs=pl.BlockSpec((None, tq, D_MODEL), lambda b, i: (b, i, 0)),
        out_shape=jax.ShapeDtypeStruct((nb, t, D_MODEL), BF16),
        compiler_params=_cparams(2),
        name="mixer0_lat",
    )(zc, zc, zc, q, kv, ck, cv, cw)


def _attn1_ctx_kernel(q_ref, k_ref, v_ref, o_ref):
    for h in range(NA_HEADS):
        sl = slice(h * HEAD_DIM, (h + 1) * HEAD_DIM)
        o = _softmax_pv([_nt_dot(q_ref[:, sl], k_ref[:, sl])], [v_ref[:, sl]])
        o_ref[:, sl] = o.astype(o_ref.dtype)


def _attn1_ctx(q, k, v):
    t = q.shape[0]
    blk = pl.BlockSpec((SEQ, D_MODEL), lambda b: (b, 0))
    return pl.pallas_call(
        _attn1_ctx_kernel,
        grid=(t // SEQ,),
        in_specs=[blk, blk, blk],
        out_specs=blk,
        out_shape=jax.ShapeDtypeStruct((t, D_MODEL), BF16),
        compiler_params=_cparams(1),
        name="mixer1_ctx",
    )(q, k, v)


def _na_slab_row(blk):
    return min(max(NA_QROWS * blk - NA_WIN_ROWS // 2, 0), GRID_H - NA_SLAB_ROWS)


NA_NROW = 2 * NA_WIN_ROWS - 1
NA_MASKED = NA_NROW
NA_FIRST_IN = NA_WIN_ROWS - 1 - NA_WIN_ROWS // 2
NA_LAST_IN = NA_FIRST_IN + NA_WIN_ROWS - 1
NA_PIECE_PAIRS = ([(m, m + 1) for m in range(NA_NROW - 1)]
                  + [(NA_MASKED, NA_FIRST_IN), (NA_LAST_IN, NA_MASKED)])


def _na_piece(blk, dr, jp):
    r = NA_QROWS * blk + dr
    ks = min(max(r - NA_WIN_ROWS // 2, 0), GRID_H - NA_WIN_ROWS)
    pair = []
    for j in (2 * jp, 2 * jp + 1):
        krow = _na_slab_row(blk) + j
        pair.append(krow - r + NA_WIN_ROWS - 1 if ks <= krow < ks + NA_WIN_ROWS else NA_MASKED)
    pair = tuple(pair)
    if pair == (NA_MASKED, NA_MASKED):
        return None
    return NA_PIECE_PAIRS.index(pair)


NA_PAD = GRID_W - NA_WIN_COLS


def _na_bias_table(rel_bias):
    return jnp.pad(rel_bias, ((0, 0), (0, 1), (NA_PAD, 2 * GRID_W - (2 * NA_WIN_COLS - 1) - NA_PAD)),
                   constant_values=MASK_NEG)


def _na_fill_pieces(table_ref, pieces_ref):
    shape = (GRID_W, 2 * GRID_W)
    c = lax.broadcasted_iota(jnp.int32, shape, 0)
    lane = lax.broadcasted_iota(jnp.int32, shape, 1)
    kc = lane & (GRID_W - 1)
    cs = jnp.clip(c - NA_WIN_COLS // 2, 0, GRID_W - NA_WIN_COLS)
    in_window = (kc >= cs) & (kc < cs + NA_WIN_COLS)

    def expand(a, shift):
        row = jnp.broadcast_to(table_ref[a:a + 1, :], shape)
        return pltpu.roll(row, shift, 1, stride=1, stride_axis=0)

    left_shift = 2 * GRID_W - (GRID_W - 1)
    right_shift = 1
    for m, (a_left, a_right) in enumerate(NA_PIECE_PAIRS):
        both = jnp.where(lane < GRID_W, expand(a_left, left_shift), expand(a_right, right_shift))
        pieces_ref[m] = jnp.where(in_window, both * LOG2E, MASK_NEG)


def _na_bias_block(pieces_ref, blk):
    rows = []
    for dr in range(NA_QROWS):
        cols = []
        for jp in range(NA_SLAB_ROWS // 2):
            m = _na_piece(blk, dr, jp)
            cols.append(jnp.full((GRID_W, 2 * GRID_W), MASK_NEG, F32) if m is None else pieces_ref[m])
        rows.append(jnp.concatenate(cols, axis=1))
    return jnp.concatenate(rows, axis=0)


def _attn1_lat_kernel(q_ref, k_ref, v_ref, ck_ref, cv_ref, table_ref, o_ref, pieces_ref):
    _na_fill_pieces(table_ref, pieces_ref)
    ck = ck_ref[...].astype(BF16)
    cv = cv_ref[...].astype(BF16)
    for blk in range(NA_NBLK):
        rows = slice(blk * NA_QBLK, (blk + 1) * NA_QBLK)
        s0 = _na_slab_row(blk) * GRID_W
        slab = slice(s0, s0 + NA_SLAB)
        q = q_ref[rows, :]
        s_loc = _nt_dot(q, k_ref[slab, :]) + _na_bias_block(pieces_ref, blk)
        o = _softmax_pv([s_loc, _nt_dot(q, ck)], [v_ref[slab, :], cv])
        o_ref[rows, :] = o.astype(o_ref.dtype)


def _attn1_lat(qkv, ck, cv, table):
    _, nb, t, _ = qkv.shape
    part = lambda p: pl.BlockSpec((None, None, t, HEAD_DIM), lambda b, h: (p, b, 0, h))
    head = pl.BlockSpec((None, t, HEAD_DIM), lambda b, h: (b, 0, h))
    ctx = pl.BlockSpec((None, PAST_LEN, HEAD_DIM), lambda b, h: (b, 0, h))
    return pl.pallas_call(
        _attn1_lat_kernel,
        grid=(nb, NA_HEADS),
        in_specs=[part(0), part(1), part(2), ctx, ctx,
                  pl.BlockSpec((None, NA_NROW + 1, 2 * GRID_W), lambda b, h: (h, 0, 0))],
        out_specs=head,
        out_shape=jax.ShapeDtypeStruct((nb, t, D_MODEL), BF16),
        scratch_shapes=[pltpu.VMEM((len(NA_PIECE_PAIRS), GRID_W, 2 * GRID_W), F32)],
        compiler_params=_cparams(2),
        name="mixer1_lat",
    )(qkv, qkv, qkv, ck, cv, table)


def _out_proj_kernel(m_ref, w_ref, x_ref, gate_ref, o_ref):
    acc = jnp.dot(m_ref[...], w_ref[...], preferred_element_type=F32)
    o_ref[...] = x_ref[...] + gate_ref[...] * acc


def _out_proj(mix, w, x, mods, row_of_tile, tm, tn):
    t, k = mix.shape
    return pl.pallas_call(
        _out_proj_kernel,
        grid=(t // tm, D_MODEL // tn),
        in_specs=[
            pl.BlockSpec((tm, k), lambda i, j: (i, 0)),
            pl.BlockSpec((k, tn), lambda i, j: (0, j)),
            pl.BlockSpec((tm, tn), lambda i, j: (i, j)),
            pl.BlockSpec((None, None, 1, tn), lambda i, j: (row_of_tile(i), 2, 0, j)),
        ],
        out_specs=pl.BlockSpec((tm, tn), lambda i, j: (i, j)),
        out_shape=jax.ShapeDtypeStruct((t, D_MODEL), F32),
        compiler_params=_cparams(2),
        name="out_proj",
    )(mix, w, x, mods)


def _mlp_kernel(x_ref, g_ref, shift_ref, scale_ref, gate_ref, w1_ref, w2_ref, fg_ref, *rest, nf, final):
    n_cast = (len(rest) - 2) // 2
    o_ref, h_scr = rest[n_cast], rest[-1]
    for src_ref, dst_ref in zip(rest[:n_cast], rest[n_cast + 1:-1]):
        dst_ref[...] = src_ref[...].astype(BF16)
    f = pl.program_id(1)

    @pl.when(f == 0)
    def _():
        _ada_norm_rows(h_scr, x_ref, g_ref, shift_ref, scale_ref)
        o_ref[...] = jnp.zeros_like(o_ref)

    u = jnp.dot(h_scr[...], w1_ref[...], preferred_element_type=F32)
    u = jnp.square(jnp.maximum(u, 0.0)).astype(BF16)
    o_ref[...] += jnp.dot(u, w2_ref[...], preferred_element_type=F32)

    @pl.when(f == nf - 1)
    def _():
        y = x_ref[...] + gate_ref[...] * o_ref[...]
        if final:
            ms = jnp.mean(y * y, axis=-1, keepdims=True)
            y = y * lax.rsqrt(ms + NORM_EPS) * fg_ref[...]
        o_ref[...] = y


def _mlp(x, norm_g, mods, row_of_tile, w1, w2, final_g, final, tm, tf, cast_next=None):
    t = x.shape[0]
    nf = D_FF // tf
    nt = t // tm

    def mod(which):
        return pl.BlockSpec((None, None, 1, D_MODEL), lambda i, f: (row_of_tile(i), which, 0, 0))

    in_specs = [
        pl.BlockSpec((tm, D_MODEL), lambda i, f: (i, 0)),
        pl.BlockSpec((1, D_MODEL), lambda i, f: (0, 0)),
        mod(3), mod(4), mod(5),
        pl.BlockSpec((D_MODEL, tf), lambda i, f: (0, f)),
        pl.BlockSpec((tf, D_MODEL), lambda i, f: (f, 0)),
        pl.BlockSpec((1, D_MODEL), lambda i, f: (0, 0)),
    ]
    args = [x, norm_g, mods, mods, mods, w1, w2, final_g]
    out_specs = [pl.BlockSpec((tm, D_MODEL), lambda i, f: (i, 0))]
    out_shape = [jax.ShapeDtypeStruct((t, D_MODEL), F32)]
    for w_f32, layer in cast_next or ():
        _, rows, cols = w_f32.shape
        r = rows // (nt * nf)
        in_specs.append(pl.BlockSpec((None, r, cols), functools.partial(lambda i, f, l: (l, i * nf + f, 0), l=layer)))
        args.append(w_f32)
        out_specs.append(pl.BlockSpec((r, cols), lambda i, f: (i * nf + f, 0)))
        out_shape.append(jax.ShapeDtypeStruct((rows, cols), BF16))
    res = pl.pallas_call(
        functools.partial(_mlp_kernel, nf=nf, final=final),
        grid=(nt, nf),
        in_specs=in_specs,
        out_specs=out_specs,
        out_shape=out_shape,
        scratch_shapes=[pltpu.VMEM((tm, D_MODEL), BF16)],
        compiler_params=_cparams(2),
        name="mlp",
    )(*args)
    return res if cast_next else res[0]


def _rope_tables():
    t = np.arange(DEC_SEQ)
    half = HEAD_DIM // 4
    inv = ROPE_THETA ** (-np.arange(half, dtype=np.float32) / half)
    ang_r = (t // GRID_W).astype(np.float32)[:, None] * inv
    ang_c = (t % GRID_W).astype(np.float32)[:, None] * inv
    zero = np.zeros_like(ang_r)
    cos = np.concatenate([np.cos(ang_r)] * 2 + [np.cos(ang_c)] * 2, axis=-1)
    sin_lo = np.concatenate([-np.sin(ang_r), zero, -np.sin(ang_c), zero], axis=-1)
    sin_hi = np.concatenate([zero, np.sin(ang_r), zero, np.sin(ang_c)], axis=-1)
    return tuple(jnp.asarray(a, F32) for a in (cos, sin_lo, sin_hi))


def kernel(x_prompt, x_sample, cache_attn_k, cache_attn_v, cache_na_k, cache_na_v, c, c_ctx, mod_w, mod_b,
           norm1_g, norm2_g, ab_w_in, ab_conv_w, ab_q_norm, ab_k_norm, ab_w_out, na_w_qkv, na_rel_bias,
           na_w_out, mlp_w1, mlp_w2, final_norm_g):
    n_ctx = BATCH * SEQ
    xp = x_prompt.reshape(n_ctx, D_MODEL)
    xs = x_sample.reshape(DEC_BATCH * DEC_SEQ, D_MODEL)

    cond = jnp.concatenate([c_ctx[None, :], c, jnp.zeros((8 - 1 - DEC_BATCH, D_MODEL), F32)], axis=0)
    mods = _modulation(cond, mod_w, mod_b).reshape(2, 8, 6, 1, D_MODEL)

    tm_in, tm_out, tm_mlp, tf = 512, 1024, 512, 1024

    def ctx_row(i):
        return 0

    def lat_row(tm):
        return lambda i: 1 + i // (DEC_SEQ // tm)

    rope_tabs = _rope_tables()
    fg = final_norm_g.reshape(1, D_MODEL)

    m0 = mods[0]
    g1 = norm1_g[0].reshape(1, D_MODEL)
    w_in = ab_w_in[0].astype(BF16)
    qn = ab_q_norm[0].reshape(1, HEAD_DIM)
    kn = ab_k_norm[0].reshape(1, HEAD_DIM)
    cw = ab_conv_w[0]
    w_out = ab_w_out[0].astype(BF16)

    zc_p, q_p, kv_p = _in_proj0(xp, g1, m0, ctx_row, w_in, qn, kn, None, F32, tm_in)
    zc_s, q_s, kv_s = _in_proj0(xs, g1, m0, lat_row(tm_in), w_in, qn, kn, rope_tabs, BF16, tm_in)
    new_attn_k = kv_p[:, :KV_W].reshape(BATCH, 1, SEQ, GQA_KV_HEADS, HEAD_DIM)
    new_attn_v = kv_p[:, KV_W:].reshape(BATCH, 1, SEQ, GQA_KV_HEADS, HEAD_DIM)

    mix_p = _attn0_ctx(zc_p, q_p, kv_p, cw)
    lat3 = lambda a: a.reshape(DEC_BATCH, DEC_SEQ, a.shape[-1])
    mix_s = _attn0_lat(lat3(zc_s), lat3(q_s), lat3(kv_s),
                       cache_attn_k[:, 0].reshape(DEC_BATCH, PAST_LEN, KV_W),
                       cache_attn_v[:, 0].reshape(DEC_BATCH, PAST_LEN, KV_W), cw, 256)
    mix_s = mix_s.reshape(DEC_BATCH * DEC_SEQ, D_MODEL)

    xp = _out_proj(mix_p, w_out, xp, m0, ctx_row, tm_out, 1024)
    xs = _out_proj(mix_s, w_out, xs, m0, lat_row(tm_out), tm_out, 1024)

    g2 = norm2_g[0].reshape(1, D_MODEL)
    w1 = mlp_w1[0].astype(BF16)
    w2 = mlp_w2[0].astype(BF16)
    xp, w1_next, w2_next = _mlp(xp, g2, m0, ctx_row, w1, w2, fg, False, tm_mlp, tf,
                                cast_next=[(mlp_w1, 1), (mlp_w2, 1)])
    xs, w_qkv, w_out = _mlp(xs, g2, m0, lat_row(tm_mlp), w1, w2, fg, False, tm_mlp, tf,
                            cast_next=[(na_w_qkv, 0), (na_w_out, 0)])

    m1 = mods[1]
    g1 = norm1_g[1].reshape(1, D_MODEL)

    q_p, h_p = _in_proj1_q(xp, g1, m1, ctx_row, w_qkv, 1024, 1024)
    k_p, k_heads = _in_proj1_kv(h_p, w_qkv, 1, 1024, 1024)
    v_p, v_heads = _in_proj1_kv(h_p, w_qkv, 2, 1024, 1024)
    qkv_s = _in_proj1(xs, g1, m1, lat_row(1024), w_qkv, BF16, 1024, 1024)
    new_na_k = k_heads.reshape(BATCH, 1, SEQ, NA_HEADS, HEAD_DIM)
    new_na_v = v_heads.reshape(BATCH, 1, SEQ, NA_HEADS, HEAD_DIM)

    mix_p = _attn1_ctx(q_p, k_p, v_p)
    mix_s = _attn1_lat(qkv_s.reshape(3, DEC_BATCH, DEC_SEQ, D_MODEL),
                       cache_na_k[:, 0].reshape(DEC_BATCH, PAST_LEN, D_MODEL),
                       cache_na_v[:, 0].reshape(DEC_BATCH, PAST_LEN, D_MODEL),
                       _na_bias_table(na_rel_bias[0]))
    mix_s = mix_s.reshape(DEC_BATCH * DEC_SEQ, D_MODEL)

    xp = _out_proj(mix_p, w_out, xp, m1, ctx_row, tm_out, 1024)
    xs = _out_proj(mix_s, w_out, xs, m1, lat_row(tm_out), tm_out, 1024)

    g2 = norm2_g[1].reshape(1, D_MODEL)
    yp = _mlp(xp, g2, m1, ctx_row, w1_next, w2_next, fg, True, tm_mlp, tf)
    ys = _mlp(xs, g2, m1, lat_row(tm_mlp), w1_next, w2_next, fg, True, tm_mlp, tf)

    return (yp.reshape(BATCH, SEQ, D_MODEL), ys.reshape(DEC_BATCH, DEC_SEQ, D_MODEL),
            new_attn_k, new_attn_v, new_na_k, new_na_v)
```

```python
import functools
import math

import numpy as np
import jax
import jax.numpy as jnp
from jax import lax
from jax.experimental import pallas as pl
from jax.experimental.pallas import tpu as pltpu

D_MODEL = 2048
BATCH = 32
SEQ = 256
DEC_BATCH = 2
DEC_SEQ = 2048
PAST_LEN = 512
GRID_W = 64
GRID_H = DEC_SEQ // GRID_W
HEAD_DIM = 128
CONV_WIDTH = D_MODEL // 2
GQA_HEADS = 8
GQA_KV_HEADS = 2
GQA_GROUP = GQA_HEADS // GQA_KV_HEADS
NA_HEADS = 16
NA_WIN_ROWS = 8
NA_WIN_COLS = 16
D_FF = 4 * D_MODEL
ROPE_THETA = 10000.0
NORM_EPS = 1e-6
AB_IN = 3 * CONV_WIDTH + (GQA_HEADS + 2 * GQA_KV_HEADS) * HEAD_DIM
KV_W = GQA_KV_HEADS * HEAD_DIM
LOG2E = math.log2(math.e)
ATTN_SCALE = HEAD_DIM ** -0.5 * LOG2E

NA_QROWS = 4
NA_SLAB_ROWS = NA_QROWS + NA_WIN_ROWS
NA_QBLK = NA_QROWS * GRID_W
NA_SLAB = NA_SLAB_ROWS * GRID_W
NA_NBLK = GRID_H // NA_QROWS
MASK_NEG = -1e30

V7X_VMEM_LIMIT = 56 * 1024 * 1024

BF16 = jnp.bfloat16
F32 = jnp.float32


def _cparams(n_axes):
    return pltpu.CompilerParams(
        dimension_semantics=("parallel",) + ("arbitrary",) * (n_axes - 1),
        vmem_limit_bytes=V7X_VMEM_LIMIT)


def _nt_dot(a, b):
    return lax.dot_general(a, b, (((1,), (1,)), ((), ())), preferred_element_type=F32)


NORM_ROWS = 16


def _ada_norm_rows(h_ref, x_ref, g_ref, shift_ref, scale_ref):
    gain = g_ref[...] * (1.0 + scale_ref[...])
    shift = shift_ref[...]

    def body(c, carry):
        rows = pl.ds(pl.multiple_of(c * NORM_ROWS, NORM_ROWS), NORM_ROWS)
        x = x_ref[rows, :]
        ms = jnp.mean(x * x, axis=-1, keepdims=True)
        h_ref[rows, :] = (x * lax.rsqrt(ms + NORM_EPS) * gain + shift).astype(h_ref.dtype)
        return carry

    lax.fori_loop(0, x_ref.shape[0] // NORM_ROWS, body, 0, unroll=8)


def _head_norm(x, g):
    ms = jnp.mean(x * x, axis=-1, keepdims=True)
    return x * lax.rsqrt(ms + NORM_EPS) * g


def _rope(x, cos, sin_lo, sin_hi):
    return x * cos + pltpu.roll(x, HEAD_DIM - 32, 1) * sin_lo + pltpu.roll(x, 32, 1) * sin_hi


def _softmax_pv(scores, values):
    m = scores[0].max(axis=-1, keepdims=True)
    for s in scores[1:]:
        m = jnp.maximum(m, s.max(axis=-1, keepdims=True))
    l = None
    o = None
    for s, v in zip(scores, values):
        p = jnp.exp2(s - m)
        ls = jnp.sum(p, axis=-1, keepdims=True)
        os_ = jnp.dot(p.astype(BF16), v, preferred_element_type=F32)
        l = ls if l is None else l + ls
        o = os_ if o is None else o + os_
    return o / l


def _mod_kernel(c_ref, w_ref, b_ref, o_ref):
    c = c_ref[...]
    s = (c * jax.nn.sigmoid(c)).astype(BF16)
    o_ref[...] = jnp.dot(s, w_ref[...].astype(BF16), preferred_element_type=F32) + b_ref[...]


def _modulation(cond, mod_w, mod_b):
    depth, d, n = mod_w.shape
    tn = 1024
    rows = cond.shape[0]
    return pl.pallas_call(
        _mod_kernel,
        grid=(depth, n // tn),
        in_specs=[
            pl.BlockSpec((rows, d), lambda l, j: (0, 0)),
            pl.BlockSpec((None, d, tn), lambda l, j: (l, 0, j)),
            pl.BlockSpec((None, 1, tn), lambda l, j: (l, 0, j)),
        ],
        out_specs=pl.BlockSpec((None, rows, tn), lambda l, j: (l, 0, j)),
        out_shape=jax.ShapeDtypeStruct((depth, rows, n), F32),
        compiler_params=_cparams(2),
        name="modulation",
    )(cond, mod_w, mod_b.reshape(depth, 1, n))


def _mod_spec(which, row_of_tile):
    return pl.BlockSpec((None, None, 1, D_MODEL), lambda i, j: (row_of_tile(i), which, 0, 0))


Q_W = GQA_HEADS * HEAD_DIM
IN0_TN = Q_W + 2 * KV_W


def _in0_kernel(x_ref, g_ref, shift_ref, scale_ref, w_ref, qn_ref, kn_ref, *rest, rope):
    if rope:
        cos_ref, slo_ref, shi_ref, zc_ref, q_ref, kv_ref, h_scr = rest
    else:
        zc_ref, q_ref, kv_ref, h_scr = rest
    j = pl.program_id(1)
    n_zc = 3 * CONV_WIDTH // IN0_TN

    @pl.when(j == 0)
    def _():
        _ada_norm_rows(h_scr, x_ref, g_ref, shift_ref, scale_ref)

    acc = jnp.dot(h_scr[...], w_ref[...], preferred_element_type=F32)
    zc_ref[...] = acc.astype(zc_ref.dtype)

    def normed(blk, gain):
        y = _head_norm(blk, gain)
        if rope:
            y = _rope(y, cos_ref[...], slo_ref[...], shi_ref[...])
        return y

    @pl.when(j == n_zc)
    def _():
        for hh in range(GQA_HEADS):
            sl = slice(hh * HEAD_DIM, (hh + 1) * HEAD_DIM)
            q_ref[:, sl] = (normed(acc[:, sl], qn_ref[...]) * ATTN_SCALE).astype(q_ref.dtype)
        for hh in range(GQA_KV_HEADS):
            src = slice(Q_W + hh * HEAD_DIM, Q_W + (hh + 1) * HEAD_DIM)
            kv_ref[:, hh * HEAD_DIM:(hh + 1) * HEAD_DIM] = normed(acc[:, src], kn_ref[...]).astype(kv_ref.dtype)
        kv_ref[:, KV_W:] = acc[:, Q_W + KV_W:].astype(kv_ref.dtype)


def _in_proj0(x, norm_g, mods, row_of_tile, w, qn, kn, rope_tabs, kv_dtype, tm):
    t = x.shape[0]
    tn = IN0_TN
    rope = rope_tabs is not None
    vec = pl.BlockSpec((1, HEAD_DIM), lambda i, j: (0, 0))
    in_specs = [
        pl.BlockSpec((tm, D_MODEL), lambda i, j: (i, 0)),
        pl.BlockSpec((1, D_MODEL), lambda i, j: (0, 0)),
        _mod_spec(0, row_of_tile),
        _mod_spec(1, row_of_tile),
        pl.BlockSpec((D_MODEL, tn), lambda i, j: (0, j)),
        vec, vec,
    ]
    args = [x, norm_g, mods, mods, w, qn, kn]
    if rope:
        nt = DEC_SEQ // tm
        tab = pl.BlockSpec((tm, HEAD_DIM), lambda i, j: (i % nt, 0))
        in_specs += [tab, tab, tab]
        args += list(rope_tabs)
    return pl.pallas_call(
        functools.partial(_in0_kernel, rope=rope),
        grid=(t // tm, AB_IN // tn),
        in_specs=in_specs,
        out_specs=[
            pl.BlockSpec((tm, tn), lambda i, j: (i, j)),
            pl.BlockSpec((tm, Q_W), lambda i, j: (i, 0)),
            pl.BlockSpec((tm, 2 * KV_W), lambda i, j: (i, 0)),
        ],
        out_shape=[
            jax.ShapeDtypeStruct((t, AB_IN), BF16),
            jax.ShapeDtypeStruct((t, GQA_HEADS * HEAD_DIM), BF16),
            jax.ShapeDtypeStruct((t, 2 * KV_W), kv_dtype),
        ],
        scratch_shapes=[pltpu.VMEM((tm, D_MODEL), BF16)],
        compiler_params=_cparams(2),
        name="in_proj0",
    )(*args)


def _in1_kernel(x_ref, g_ref, shift_ref, scale_ref, w_ref, o_ref, h_scr, *, nb):
    j = pl.program_id(1)

    @pl.when(j == 0)
    def _():
        _ada_norm_rows(h_scr, x_ref, g_ref, shift_ref, scale_ref)

    acc = jnp.dot(h_scr[...], w_ref[...], preferred_element_type=F32)
    o_ref[...] = (acc * jnp.where(j < nb, ATTN_SCALE, 1.0)).astype(o_ref.dtype)


def _in_proj1(x, norm_g, mods, row_of_tile, w, out_dtype, tm, tn):
    t = x.shape[0]
    nb = D_MODEL // tn
    return pl.pallas_call(
        functools.partial(_in1_kernel, nb=nb),
        grid=(t // tm, 3 * nb),
        in_specs=[
            pl.BlockSpec((tm, D_MODEL), lambda i, j: (i, 0)),
            pl.BlockSpec((1, D_MODEL), lambda i, j: (0, 0)),
            _mod_spec(0, row_of_tile),
            _mod_spec(1, row_of_tile),
            pl.BlockSpec((D_MODEL, tn), lambda i, j: (0, j)),
        ],
        out_specs=pl.BlockSpec((None, tm, tn), lambda i, j: (j // nb, i, j % nb)),
        out_shape=jax.ShapeDtypeStruct((3, t, D_MODEL), out_dtype),
        scratch_shapes=[pltpu.VMEM((tm, D_MODEL), BF16)],
        compiler_params=_cparams(2),
        name="in_proj1",
    )(x, norm_g, mods, mods, w)


def _in1q_kernel(x_ref, g_ref, shift_ref, scale_ref, w_ref, q_ref, h_ref):
    @pl.when(pl.program_id(1) == 0)
    def _():
        _ada_norm_rows(h_ref, x_ref, g_ref, shift_ref, scale_ref)

    acc = jnp.dot(h_ref[...], w_ref[...], preferred_element_type=F32)
    q_ref[...] = (acc * ATTN_SCALE).astype(q_ref.dtype)


def _in_proj1_q(x, norm_g, mods, row_of_tile, w, tm, tn):
    t = x.shape[0]
    return pl.pallas_call(
        _in1q_kernel,
        grid=(t // tm, D_MODEL // tn),
        in_specs=[
            pl.BlockSpec((tm, D_MODEL), lambda i, j: (i, 0)),
            pl.BlockSpec((1, D_MODEL), lambda i, j: (0, 0)),
            _mod_spec(0, row_of_tile),
            _mod_spec(1, row_of_tile),
            pl.BlockSpec((D_MODEL, tn), lambda i, j: (0, j)),
        ],
        out_specs=[pl.BlockSpec((tm, tn), lambda i, j: (i, j)),
                   pl.BlockSpec((tm, D_MODEL), lambda i, j: (i, 0))],
        out_shape=[jax.ShapeDtypeStruct((t, D_MODEL), BF16), jax.ShapeDtypeStruct((t, D_MODEL), BF16)],
        compiler_params=_cparams(2),
        name="in_proj1_q",
    )(x, norm_g, mods, mods, w)


def _in1kv_kernel(h_ref, w_ref, flat_ref, heads_ref):
    acc = jnp.dot(h_ref[...], w_ref[...], preferred_element_type=F32)
    flat_ref[...] = acc.astype(flat_ref.dtype)
    heads_ref[...] = acc.reshape(heads_ref.shape)


def _in_proj1_kv(h, w, part, tm, tn):
    t = h.shape[0]
    nb = D_MODEL // tn
    hb = tn // HEAD_DIM
    return pl.pallas_call(
        _in1kv_kernel,
        grid=(t // tm, nb),
        in_specs=[
            pl.BlockSpec((tm, D_MODEL), lambda i, j: (i, 0)),
            pl.BlockSpec((D_MODEL, tn), lambda i, j: (0, part * nb + j)),
        ],
        out_specs=[pl.BlockSpec((tm, tn), lambda i, j: (i, j)),
                   pl.BlockSpec((tm, hb, HEAD_DIM), lambda i, j: (i, j, 0))],
        out_shape=[jax.ShapeDtypeStruct((t, D_MODEL), BF16),
                   jax.ShapeDtypeStruct((t, NA_HEADS, HEAD_DIM), F32)],
        compiler_params=_cparams(2),
        name="in_proj1_kv",
    )(h, w)


def _gated_conv(zc, prev_row, next_row, cw):
    c = CONV_WIDTH
    s = zc.shape[0]
    gb = zc[:, 0:c].astype(F32)
    u = zc[:, c:2 * c].astype(F32) * zc[:, 2 * c:3 * c].astype(F32)
    row = lax.broadcasted_iota(jnp.int32, u.shape, 0)
    prev = jnp.where(row == 0, prev_row, pltpu.roll(u, 1, 0))
    nxt = jnp.where(row == s - 1, next_row, pltpu.roll(u, s - 1, 0))
    return gb * (prev * cw[0:1] + u * cw[1:2] + nxt * cw[2:3])


def _attn0_ctx_kernel(zc_ref, q_ref, kv_ref, cw_ref, o_ref):
    zero = jnp.zeros((1, CONV_WIDTH), F32)
    a = _gated_conv(zc_ref[...], zero, zero, cw_ref[...])
    o_ref[:, 0:CONV_WIDTH] = a.astype(o_ref.dtype)
    for g in range(GQA_KV_HEADS):
        k = kv_ref[:, g * HEAD_DIM:(g + 1) * HEAD_DIM].astype(BF16)
        v = kv_ref[:, KV_W + g * HEAD_DIM:KV_W + (g + 1) * HEAD_DIM].astype(BF16)
        for hh in range(GQA_GROUP):
            h = g * GQA_GROUP + hh
            q = q_ref[:, h * HEAD_DIM:(h + 1) * HEAD_DIM]
            o = _softmax_pv([_nt_dot(q, k)], [v])
            o_ref[:, CONV_WIDTH + h * HEAD_DIM:CONV_WIDTH + (h + 1) * HEAD_DIM] = o.astype(o_ref.dtype)


def _attn0_ctx(zc, q, kv, cw):
    t = zc.shape[0]
    return pl.pallas_call(
        _attn0_ctx_kernel,
        grid=(t // SEQ,),
        in_specs=[
            pl.BlockSpec((SEQ, 3 * CONV_WIDTH), lambda b: (b, 0)),
            pl.BlockSpec((SEQ, GQA_HEADS * HEAD_DIM), lambda b: (b, 0)),
            pl.BlockSpec((SEQ, 2 * KV_W), lambda b: (b, 0)),
            pl.BlockSpec((3, CONV_WIDTH), lambda b: (0, 0)),
        ],
        out_specs=pl.BlockSpec((SEQ, D_MODEL), lambda b: (b, 0)),
        out_shape=jax.ShapeDtypeStruct((t, D_MODEL), BF16),
        compiler_params=_cparams(1),
        name="mixer0_ctx",
    )(zc, q, kv, cw)


_HALO = 16


def _side_cast_specs(cast_list, n_steps, step_of):
    in_specs, args, out_specs, out_shape = [], [], [], []
    for w_f32, layer in cast_list:
        _, rows, cols = w_f32.shape
        r = rows // n_steps
        in_specs.append(pl.BlockSpec((None, r, cols),
                                     functools.partial(lambda *g, l: (l, step_of(*g), 0), l=layer)))
        args.append(w_f32)
        out_specs.append(pl.BlockSpec((r, cols), lambda *g: (step_of(*g), 0)))
        out_shape.append(jax.ShapeDtypeStruct((rows, cols), BF16))
    return in_specs, args, out_specs, out_shape


def _side_cast(src_refs, dst_refs):
    for src_ref, dst_ref in zip(src_refs, dst_refs):
        dst_ref[...] = src_ref[...].astype(BF16)


def _attn0_lat_kernel(zc_ref, zp_ref, zn_ref, q_ref, kv_ref, ck_ref, cv_ref, cw_ref, *rest, nqb):
    n_cast = len(rest) // 2
    o_ref = rest[n_cast]
    _side_cast(rest[:n_cast], rest[n_cast + 1:])
    qb = pl.program_id(1)
    c = CONV_WIDTH
    up = zp_ref[:, c:2 * c].astype(F32) * zp_ref[:, 2 * c:3 * c].astype(F32)
    un = zn_ref[:, c:2 * c].astype(F32) * zn_ref[:, 2 * c:3 * c].astype(F32)
    up = jnp.where(qb == 0, 0.0, up[_HALO - 1:_HALO, :])
    un = jnp.where(qb == nqb - 1, 0.0, un[0:1, :])
    a = _gated_conv(zc_ref[...], up, un, cw_ref[...])
    o_ref[:, 0:c] = a.astype(o_ref.dtype)
    for g in range(GQA_KV_HEADS):
        ks = slice(g * HEAD_DIM, (g + 1) * HEAD_DIM)
        vs = slice(KV_W + g * HEAD_DIM, KV_W + (g + 1) * HEAD_DIM)
        k = kv_ref[:, ks]
        v = kv_ref[:, vs]
        ck = ck_ref[:, ks].astype(BF16)
        cv = cv_ref[:, ks].astype(BF16)
        for hh in range(GQA_GROUP):
            h = g * GQA_GROUP + hh
            q = q_ref[:, h * HEAD_DIM:(h + 1) * HEAD_DIM]
            o = _softmax_pv([_nt_dot(q, ck), _nt_dot(q, k)], [cv, v])
            o_ref[:, c + h * HEAD_DIM:c + (h + 1) * HEAD_DIM] = o.astype(o_ref.dtype)


def _attn0_lat(zc, q, kv, ck, cv, cw, tq, cast_list):
    nb, t, _ = zc.shape
    nqb = t // tq
    hb = tq // _HALO
    c_in, c_args, c_out, c_shape = _side_cast_specs(cast_list, nb * nqb, lambda b, i: b * nqb + i)
    return pl.pallas_call(
        functools.partial(_attn0_lat_kernel, nqb=nqb),
        grid=(nb, nqb),
        in_specs=[
            pl.BlockSpec((None, tq, 3 * CONV_WIDTH), lambda b, i: (b, i, 0)),
            pl.BlockSpec((None, _HALO, 3 * CONV_WIDTH), lambda b, i: (b, jnp.maximum(i * hb - 1, 0), 0)),
            pl.BlockSpec((None, _HALO, 3 * CONV_WIDTH),
                         lambda b, i: (b, jnp.minimum((i + 1) * hb, t // _HALO - 1), 0)),
            pl.BlockSpec((None, tq, GQA_HEADS * HEAD_DIM), lambda b, i: (b, i, 0)),
            pl.BlockSpec((None, t, 2 * KV_W), lambda b, i: (b, 0, 0)),
            pl.BlockSpec((None, PAST_LEN, KV_W), lambda b, i: (b, 0, 0)),
            pl.BlockSpec((None, PAST_LEN, KV_W), lambda b, i: (b, 0, 0)),
            pl.BlockSpec((3, CONV_WIDTH), lambda b, i: (0, 0)),
        ] + c_in,
        out_specs=[pl.BlockSpec((None, tq, D_MODEL), lambda b, i: (b, i, 0))] + c_out,
        out_shape=[jax.ShapeDtypeStruct((nb, t, D_MODEL), BF16)] + c_shape,
        compiler_params=_cparams(2),
        name="mixer0_lat",
    )(zc, zc, zc, q, kv, ck, cv, cw, *c_args)


def _attn1_ctx_kernel(q_ref, k_ref, v_ref, o_ref):
    for h in range(NA_HEADS):
        sl = slice(h * HEAD_DIM, (h + 1) * HEAD_DIM)
        o = _softmax_pv([_nt_dot(q_ref[:, sl], k_ref[:, sl])], [v_ref[:, sl]])
        o_ref[:, sl] = o.astype(o_ref.dtype)


def _attn1_ctx(q, k, v):
    t = q.shape[0]
    blk = pl.BlockSpec((SEQ, D_MODEL), lambda b: (b, 0))
    return pl.pallas_call(
        _attn1_ctx_kernel,
        grid=(t // SEQ,),
        in_specs=[blk, blk, blk],
        out_specs=blk,
        out_shape=jax.ShapeDtypeStruct((t, D_MODEL), BF16),
        compiler_params=_cparams(1),
        name="mixer1_ctx",
    )(q, k, v)


def _na_slab_row(blk):
    return min(max(NA_QROWS * blk - NA_WIN_ROWS // 2, 0), GRID_H - NA_SLAB_ROWS)


NA_NROW = 2 * NA_WIN_ROWS - 1
NA_MASKED = NA_NROW
NA_FIRST_IN = NA_WIN_ROWS - 1 - NA_WIN_ROWS // 2
NA_LAST_IN = NA_FIRST_IN + NA_WIN_ROWS - 1
NA_PIECE_PAIRS = ([(m, m + 1) for m in range(NA_NROW - 1)]
                  + [(NA_MASKED, NA_FIRST_IN), (NA_LAST_IN, NA_MASKED)])


def _na_piece(blk, dr, jp):
    r = NA_QROWS * blk + dr
    ks = min(max(r - NA_WIN_ROWS // 2, 0), GRID_H - NA_WIN_ROWS)
    pair = []
    for j in (2 * jp, 2 * jp + 1):
        krow = _na_slab_row(blk) + j
        pair.append(krow - r + NA_WIN_ROWS - 1 if ks <= krow < ks + NA_WIN_ROWS else NA_MASKED)
    pair = tuple(pair)
    if pair == (NA_MASKED, NA_MASKED):
        return None
    return NA_PIECE_PAIRS.index(pair)


NA_PAD = GRID_W - NA_WIN_COLS


def _na_bias_table(rel_bias):
    return jnp.pad(rel_bias, ((0, 0), (0, 1), (NA_PAD, 2 * GRID_W - (2 * NA_WIN_COLS - 1) - NA_PAD)),
                   constant_values=MASK_NEG)


def _na_fill_pieces(table_ref, pieces_ref):
    shape = (GRID_W, 2 * GRID_W)
    c = lax.broadcasted_iota(jnp.int32, shape, 0)
    lane = lax.broadcasted_iota(jnp.int32, shape, 1)
    kc = lane & (GRID_W - 1)
    cs = jnp.clip(c - NA_WIN_COLS // 2, 0, GRID_W - NA_WIN_COLS)
    in_window = (kc >= cs) & (kc < cs + NA_WIN_COLS)

    def expand(a, shift):
        row = jnp.broadcast_to(table_ref[a:a + 1, :], shape)
        return pltpu.roll(row, shift, 1, stride=1, stride_axis=0)

    left_shift = 2 * GRID_W - (GRID_W - 1)
    right_shift = 1
    for m, (a_left, a_right) in enumerate(NA_PIECE_PAIRS):
        both = jnp.where(lane < GRID_W, expand(a_left, left_shift), expand(a_right, right_shift))
        pieces_ref[m] = jnp.where(in_window, both * LOG2E, MASK_NEG)


def _na_bias_block(pieces_ref, blk):
    rows = []
    for dr in range(NA_QROWS):
        cols = []
        for jp in range(NA_SLAB_ROWS // 2):
            m = _na_piece(blk, dr, jp)
            cols.append(jnp.full((GRID_W, 2 * GRID_W), MASK_NEG, F32) if m is None else pieces_ref[m])
        rows.append(jnp.concatenate(cols, axis=1))
    return jnp.concatenate(rows, axis=0)


def _attn1_lat_kernel(q_ref, k_ref, v_ref, ck_ref, cv_ref, table_ref, o_ref, pieces_ref):
    _na_fill_pieces(table_ref, pieces_ref)
    ck = ck_ref[...].astype(BF16)
    cv = cv_ref[...].astype(BF16)
    for blk in range(NA_NBLK):
        rows = slice(blk * NA_QBLK, (blk + 1) * NA_QBLK)
        s0 = _na_slab_row(blk) * GRID_W
        slab = slice(s0, s0 + NA_SLAB)
        q = q_ref[rows, :]
        s_loc = _nt_dot(q, k_ref[slab, :]) + _na_bias_block(pieces_ref, blk)
        o = _softmax_pv([s_loc, _nt_dot(q, ck)], [v_ref[slab, :], cv])
        o_ref[rows, :] = o.astype(o_ref.dtype)


def _attn1_lat(qkv, ck, cv, table):
    _, nb, t, _ = qkv.shape
    part = lambda p: pl.BlockSpec((None, None, t, HEAD_DIM), lambda b, h: (p, b, 0, h))
    head = pl.BlockSpec((None, t, HEAD_DIM), lambda b, h: (b, 0, h))
    ctx = pl.BlockSpec((None, PAST_LEN, HEAD_DIM), lambda b, h: (b, 0, h))
    return pl.pallas_call(
        _attn1_lat_kernel,
        grid=(nb, NA_HEADS),
        in_specs=[part(0), part(1), part(2), ctx, ctx,
                  pl.BlockSpec((None, NA_NROW + 1, 2 * GRID_W), lambda b, h: (h, 0, 0))],
        out_specs=head,
        out_shape=jax.ShapeDtypeStruct((nb, t, D_MODEL), BF16),
        scratch_shapes=[pltpu.VMEM((len(NA_PIECE_PAIRS), GRID_W, 2 * GRID_W), F32)],
        compiler_params=_cparams(2),
        name="mixer1_lat",
    )(qkv, qkv, qkv, ck, cv, table)


def _out_proj_kernel(m_ref, w_ref, x_ref, gate_ref, o_ref):
    acc = jnp.dot(m_ref[...], w_ref[...], preferred_element_type=F32)
    o_ref[...] = x_ref[...] + gate_ref[...] * acc


def _out_proj(mix, w, x, mods, row_of_tile, tm, tn):
    t, k = mix.shape
    return pl.pallas_call(
        _out_proj_kernel,
        grid=(t // tm, D_MODEL // tn),
        in_specs=[
            pl.BlockSpec((tm, k), lambda i, j: (i, 0)),
            pl.BlockSpec((k, tn), lambda i, j: (0, j)),
            pl.BlockSpec((tm, tn), lambda i, j: (i, j)),
            pl.BlockSpec((None, None, 1, tn), lambda i, j: (row_of_tile(i), 2, 0, j)),
        ],
        out_specs=pl.BlockSpec((tm, tn), lambda i, j: (i, j)),
        out_shape=jax.ShapeDtypeStruct((t, D_MODEL), F32),
        compiler_params=_cparams(2),
        name="out_proj",
    )(mix, w, x, mods)


def _mlp_kernel(x_ref, g_ref, shift_ref, scale_ref, gate_ref, w1_ref, w2_ref, fg_ref, *rest, nf, final):
    n_cast = (len(rest) - 2) // 2
    o_ref, h_scr = rest[n_cast], rest[-1]
    _side_cast(rest[:n_cast], rest[n_cast + 1:-1])
    f = pl.program_id(1)

    @pl.when(f == 0)
    def _():
        _ada_norm_rows(h_scr, x_ref, g_ref, shift_ref, scale_ref)
        o_ref[...] = jnp.zeros_like(o_ref)

    u = jnp.dot(h_scr[...], w1_ref[...], preferred_element_type=F32)
    u = jnp.square(jnp.maximum(u, 0.0)).astype(BF16)
    o_ref[...] += jnp.dot(u, w2_ref[...], preferred_element_type=F32)

    @pl.when(f == nf - 1)
    def _():
        y = x_ref[...] + gate_ref[...] * o_ref[...]
        if final:
            ms = jnp.mean(y * y, axis=-1, keepdims=True)
            y = y * lax.rsqrt(ms + NORM_EPS) * fg_ref[...]
        o_ref[...] = y


def _mlp(x, norm_g, mods, row_of_tile, w1, w2, final_g, final, tm, tf, cast_next=None):
    t = x.shape[0]
    nf = D_FF // tf
    nt = t // tm

    def mod(which):
        return pl.BlockSpec((None, None, 1, D_MODEL), lambda i, f: (row_of_tile(i), which, 0, 0))

    in_specs = [
        pl.BlockSpec((tm, D_MODEL), lambda i, f: (i, 0)),
        pl.BlockSpec((1, D_MODEL), lambda i, f: (0, 0)),
        mod(3), mod(4), mod(5),
        pl.BlockSpec((D_MODEL, tf), lambda i, f: (0, f)),
        pl.BlockSpec((tf, D_MODEL), lambda i, f: (f, 0)),
        pl.BlockSpec((1, D_MODEL), lambda i, f: (0, 0)),
    ]
    args = [x, norm_g, mods, mods, mods, w1, w2, final_g]
    out_specs = [pl.BlockSpec((tm, D_MODEL), lambda i, f: (i, 0))]
    out_shape = [jax.ShapeDtypeStruct((t, D_MODEL), F32)]
    c_in, c_args, c_out, c_shape = _side_cast_specs(cast_next or (), nt * nf, lambda i, f: i * nf + f)
    in_specs += c_in
    args += c_args
    out_specs += c_out
    out_shape += c_shape
    res = pl.pallas_call(
        functools.partial(_mlp_kernel, nf=nf, final=final),
        grid=(nt, nf),
        in_specs=in_specs,
        out_specs=out_specs,
        out_shape=out_shape,
        scratch_shapes=[pltpu.VMEM((tm, D_MODEL), BF16)],
        compiler_params=_cparams(2),
        name="mlp",
    )(*args)
    return res if cast_next else res[0]


def _rope_tables():
    t = np.arange(DEC_SEQ)
    half = HEAD_DIM // 4
    inv = ROPE_THETA ** (-np.arange(half, dtype=np.float32) / half)
    ang_r = (t // GRID_W).astype(np.float32)[:, None] * inv
    ang_c = (t % GRID_W).astype(np.float32)[:, None] * inv
    zero = np.zeros_like(ang_r)
    cos = np.concatenate([np.cos(ang_r)] * 2 + [np.cos(ang_c)] * 2, axis=-1)
    sin_lo = np.concatenate([-np.sin(ang_r), zero, -np.sin(ang_c), zero], axis=-1)
    sin_hi = np.concatenate([zero, np.sin(ang_r), zero, np.sin(ang_c)], axis=-1)
    return tuple(jnp.asarray(a, F32) for a in (cos, sin_lo, sin_hi))


def kernel(x_prompt, x_sample, cache_attn_k, cache_attn_v, cache_na_k, cache_na_v, c, c_ctx, mod_w, mod_b,
           norm1_g, norm2_g, ab_w_in, ab_conv_w, ab_q_norm, ab_k_norm, ab_w_out, na_w_qkv, na_rel_bias,
           na_w_out, mlp_w1, mlp_w2, final_norm_g):
    n_ctx = BATCH * SEQ
    xp = x_prompt.reshape(n_ctx, D_MODEL)
    xs = x_sample.reshape(DEC_BATCH * DEC_SEQ, D_MODEL)

    cond = jnp.concatenate([c_ctx[None, :], c, jnp.zeros((8 - 1 - DEC_BATCH, D_MODEL), F32)], axis=0)
    mods = _modulation(cond, mod_w, mod_b).reshape(2, 8, 6, 1, D_MODEL)

    tm_in, tm_mlp, tf = 512, 512, 1024
    tm_out, tn_out = 512, D_MODEL

    def ctx_row(i):
        return 0

    def lat_row(tm):
        return lambda i: 1 + i // (DEC_SEQ // tm)

    rope_tabs = _rope_tables()
    fg = final_norm_g.reshape(1, D_MODEL)

    m0 = mods[0]
    g1 = norm1_g[0].reshape(1, D_MODEL)
    w_in = ab_w_in[0].astype(BF16)
    qn = ab_q_norm[0].reshape(1, HEAD_DIM)
    kn = ab_k_norm[0].reshape(1, HEAD_DIM)
    cw = ab_conv_w[0]
    w_out = ab_w_out[0].astype(BF16)

    zc_p, q_p, kv_p = _in_proj0(xp, g1, m0, ctx_row, w_in, qn, kn, None, F32, tm_in)
    zc_s, q_s, kv_s = _in_proj0(xs, g1, m0, lat_row(tm_in), w_in, qn, kn, rope_tabs, BF16, tm_in)
    new_attn_k = kv_p[:, :KV_W].reshape(BATCH, 1, SEQ, GQA_KV_HEADS, HEAD_DIM)
    new_attn_v = kv_p[:, KV_W:].reshape(BATCH, 1, SEQ, GQA_KV_HEADS, HEAD_DIM)

    mix_p = _attn0_ctx(zc_p, q_p, kv_p, cw)
    lat3 = lambda a: a.reshape(DEC_BATCH, DEC_SEQ, a.shape[-1])
    mix_s, w1, w2 = _attn0_lat(lat3(zc_s), lat3(q_s), lat3(kv_s),
                               cache_attn_k[:, 0].reshape(DEC_BATCH, PAST_LEN, KV_W),
                               cache_attn_v[:, 0].reshape(DEC_BATCH, PAST_LEN, KV_W), cw, 256,
                               [(mlp_w1, 0), (mlp_w2, 0)])
    mix_s = mix_s.reshape(DEC_BATCH * DEC_SEQ, D_MODEL)

    xp = _out_proj(mix_p, w_out, xp, m0, ctx_row, tm_out, tn_out)
    xs = _out_proj(mix_s, w_out, xs, m0, lat_row(tm_out), tm_out, tn_out)

    g2 = norm2_g[0].reshape(1, D_MODEL)
    xp, w1_next, w2_next = _mlp(xp, g2, m0, ctx_row, w1, w2, fg, False, tm_mlp, tf,
                                cast_next=[(mlp_w1, 1), (mlp_w2, 1)])
    xs, w_qkv, w_out = _mlp(xs, g2, m0, lat_row(tm_mlp), w1, w2, fg, False, tm_mlp, tf,
                            cast_next=[(na_w_qkv, 0), (na_w_out, 0)])

    m1 = mods[1]
    g1 = norm1_g[1].reshape(1, D_MODEL)

    q_p, h_p = _in_proj1_q(xp, g1, m1, ctx_row, w_qkv, 1024, 1024)
    k_p, k_heads = _in_proj1_kv(h_p, w_qkv, 1, 1024, 1024)
    v_p, v_heads = _in_proj1_kv(h_p, w_qkv, 2, 1024, 1024)
    qkv_s = _in_proj1(xs, g1, m1, lat_row(1024), w_qkv, BF16, 1024, 1024)
    new_na_k = k_heads.reshape(BATCH, 1, SEQ, NA_HEADS, HEAD_DIM)
    new_na_v = v_heads.reshape(BATCH, 1, SEQ, NA_HEADS, HEAD_DIM)

    mix_p = _attn1_ctx(q_p, k_p, v_p)
    mix_s = _attn1_lat(qkv_s.reshape(3, DEC_BATCH, DEC_SEQ, D_MODEL),
                       cache_na_k[:, 0].reshape(DEC_BATCH, PAST_LEN, D_MODEL),
                       cache_na_v[:, 0].reshape(DEC_BATCH, PAST_LEN, D_MODEL),
                       _na_bias_table(na_rel_bias[0]))
    mix_s = mix_s.reshape(DEC_BATCH * DEC_SEQ, D_MODEL)

    xp = _out_proj(mix_p, w_out, xp, m1, ctx_row, tm_out, tn_out)
    xs = _out_proj(mix_s, w_out, xs, m1, lat_row(tm_out), tm_out, tn_out)

    g2 = norm2_g[1].reshape(1, D_MODEL)
    yp = _mlp(xp, g2, m1, ctx_row, w1_next, w2_next, fg, True, tm_mlp, tf)
    ys = _mlp(xs, g2, m1, lat_row(tm_mlp), w1_next, w2_next, fg, True, tm_mlp, tf)

    return (yp.reshape(BATCH, SEQ, D_MODEL), ys.reshape(DEC_BATCH, DEC_SEQ, D_MODEL),
            new_attn_k, new_attn_v, new_na_k, new_na_v)
```

```python
import functools
import math

import numpy as np
import jax
import jax.numpy as jnp
from jax import lax
from jax.experimental import pallas as pl
from jax.experimental.pallas import tpu as pltpu

D_MODEL = 2048
BATCH = 32
SEQ = 256
DEC_BATCH = 2
DEC_SEQ = 2048
PAST_LEN = 512
GRID_W = 64
GRID_H = DEC_SEQ // GRID_W
HEAD_DIM = 128
CONV_WIDTH = D_MODEL // 2
GQA_HEADS = 8
GQA_KV_HEADS = 2
GQA_GROUP = GQA_HEADS // GQA_KV_HEADS
NA_HEADS = 16
NA_WIN_ROWS = 8
NA_WIN_COLS = 16
D_FF = 4 * D_MODEL
ROPE_THETA = 10000.0
NORM_EPS = 1e-6
AB_IN = 3 * CONV_WIDTH + (GQA_HEADS + 2 * GQA_KV_HEADS) * HEAD_DIM
KV_W = GQA_KV_HEADS * HEAD_DIM
LOG2E = math.log2(math.e)
ATTN_SCALE = HEAD_DIM ** -0.5 * LOG2E

NA_QROWS = 4
NA_SLAB_ROWS = NA_QROWS + NA_WIN_ROWS
NA_QBLK = NA_QROWS * GRID_W
NA_SLAB = NA_SLAB_ROWS * GRID_W
NA_NBLK = GRID_H // NA_QROWS
MASK_NEG = -1e30

V7X_VMEM_LIMIT = 56 * 1024 * 1024

BF16 = jnp.bfloat16
F32 = jnp.float32


def _cparams(n_axes):
    return pltpu.CompilerParams(
        dimension_semantics=("parallel",) + ("arbitrary",) * (n_axes - 1),
        vmem_limit_bytes=V7X_VMEM_LIMIT)


def _nt_dot(a, b):
    return lax.dot_general(a, b, (((1,), (1,)), ((), ())), preferred_element_type=F32)


NORM_ROWS = 16


def _ada_norm_rows(h_ref, x_ref, g_ref, shift_ref, scale_ref):
    gain = g_ref[...] * (1.0 + scale_ref[...])
    shift = shift_ref[...]

    def body(c, carry):
        rows = pl.ds(pl.multiple_of(c * NORM_ROWS, NORM_ROWS), NORM_ROWS)
        x = x_ref[rows, :]
        ms = jnp.mean(x * x, axis=-1, keepdims=True)
        h_ref[rows, :] = (x * lax.rsqrt(ms + NORM_EPS) * gain + shift).astype(h_ref.dtype)
        return carry

    lax.fori_loop(0, x_ref.shape[0] // NORM_ROWS, body, 0, unroll=8)


def _head_norm(x, g):
    ms = jnp.mean(x * x, axis=-1, keepdims=True)
    return x * lax.rsqrt(ms + NORM_EPS) * g


def _rope(x, cos, sin_lo, sin_hi):
    return x * cos + pltpu.roll(x, HEAD_DIM - 32, 1) * sin_lo + pltpu.roll(x, 32, 1) * sin_hi


def _softmax_pv(scores, values):
    m = scores[0].max(axis=-1, keepdims=True)
    for s in scores[1:]:
        m = jnp.maximum(m, s.max(axis=-1, keepdims=True))
    l = None
    o = None
    for s, v in zip(scores, values):
        p = jnp.exp2(s - m)
        ls = jnp.sum(p, axis=-1, keepdims=True)
        os_ = jnp.dot(p.astype(BF16), v, preferred_element_type=F32)
        l = ls if l is None else l + ls
        o = os_ if o is None else o + os_
    return o / l


def _mod_kernel(c_ref, w_ref, b_ref, o_ref):
    c = c_ref[...]
    s = (c * jax.nn.sigmoid(c)).astype(BF16)
    o_ref[...] = jnp.dot(s, w_ref[...].astype(BF16), preferred_element_type=F32) + b_ref[...]


def _modulation(cond, mod_w, mod_b):
    depth, d, n = mod_w.shape
    tn = 1024
    rows = cond.shape[0]
    return pl.pallas_call(
        _mod_kernel,
        grid=(depth, n // tn),
        in_specs=[
            pl.BlockSpec((rows, d), lambda l, j: (0, 0)),
            pl.BlockSpec((None, d, tn), lambda l, j: (l, 0, j)),
            pl.BlockSpec((None, 1, tn), lambda l, j: (l, 0, j)),
        ],
        out_specs=pl.BlockSpec((None, rows, tn), lambda l, j: (l, 0, j)),
        out_shape=jax.ShapeDtypeStruct((depth, rows, n), F32),
        compiler_params=_cparams(2),
        name="modulation",
    )(cond, mod_w, mod_b.reshape(depth, 1, n))


def _mod_spec(which, row_of_tile):
    return pl.BlockSpec((None, None, 1, D_MODEL), lambda i, j: (row_of_tile(i), which, 0, 0))


Q_W = GQA_HEADS * HEAD_DIM
IN0_TN = Q_W + 2 * KV_W


def _in0_kernel(x_ref, g_ref, shift_ref, scale_ref, w_ref, qn_ref, kn_ref, *rest, rope):
    if rope:
        cos_ref, slo_ref, shi_ref, zc_ref, q_ref, kv_ref, h_scr = rest
    else:
        zc_ref, q_ref, kv_ref, h_scr = rest
    j = pl.program_id(1)
    n_zc = 3 * CONV_WIDTH // IN0_TN

    @pl.when(j == 0)
    def _():
        _ada_norm_rows(h_scr, x_ref, g_ref, shift_ref, scale_ref)

    acc = jnp.dot(h_scr[...], w_ref[...], preferred_element_type=F32)
    zc_ref[...] = acc.astype(zc_ref.dtype)

    def normed(blk, gain):
        y = _head_norm(blk, gain)
        if rope:
            y = _rope(y, cos_ref[...], slo_ref[...], shi_ref[...])
        return y

    @pl.when(j == n_zc)
    def _():
        for hh in range(GQA_HEADS):
            sl = slice(hh * HEAD_DIM, (hh + 1) * HEAD_DIM)
            q_ref[:, sl] = (normed(acc[:, sl], qn_ref[...]) * ATTN_SCALE).astype(q_ref.dtype)
        for hh in range(GQA_KV_HEADS):
            src = slice(Q_W + hh * HEAD_DIM, Q_W + (hh + 1) * HEAD_DIM)
            kv_ref[:, hh * HEAD_DIM:(hh + 1) * HEAD_DIM] = normed(acc[:, src], kn_ref[...]).astype(kv_ref.dtype)
        kv_ref[:, KV_W:] = acc[:, Q_W + KV_W:].astype(kv_ref.dtype)


def _in_proj0(x, norm_g, mods, row_of_tile, w, qn, kn, rope_tabs, kv_dtype, tm):
    t = x.shape[0]
    tn = IN0_TN
    rope = rope_tabs is not None
    vec = pl.BlockSpec((1, HEAD_DIM), lambda i, j: (0, 0))
    in_specs = [
        pl.BlockSpec((tm, D_MODEL), lambda i, j: (i, 0)),
        pl.BlockSpec((1, D_MODEL), lambda i, j: (0, 0)),
        _mod_spec(0, row_of_tile),
        _mod_spec(1, row_of_tile),
        pl.BlockSpec((D_MODEL, tn), lambda i, j: (0, j)),
        vec, vec,
    ]
    args = [x, norm_g, mods, mods, w, qn, kn]
    if rope:
        nt = DEC_SEQ // tm
        tab = pl.BlockSpec((tm, HEAD_DIM), lambda i, j: (i % nt, 0))
        in_specs += [tab, tab, tab]
        args += list(rope_tabs)
    return pl.pallas_call(
        functools.partial(_in0_kernel, rope=rope),
        grid=(t // tm, AB_IN // tn),
        in_specs=in_specs,
        out_specs=[
            pl.BlockSpec((tm, tn), lambda i, j: (i, j)),
            pl.BlockSpec((tm, Q_W), lambda i, j: (i, 0)),
            pl.BlockSpec((tm, 2 * KV_W), lambda i, j: (i, 0)),
        ],
        out_shape=[
            jax.ShapeDtypeStruct((t, AB_IN), BF16),
            jax.ShapeDtypeStruct((t, GQA_HEADS * HEAD_DIM), BF16),
            jax.ShapeDtypeStruct((t, 2 * KV_W), kv_dtype),
        ],
        scratch_shapes=[pltpu.VMEM((tm, D_MODEL), BF16)],
        compiler_params=_cparams(2),
        name="in_proj0",
    )(*args)


def _in1_kernel(x_ref, g_ref, shift_ref, scale_ref, w_ref, o_ref, h_scr, *, nb):
    j = pl.program_id(1)

    @pl.when(j == 0)
    def _():
        _ada_norm_rows(h_scr, x_ref, g_ref, shift_ref, scale_ref)

    acc = jnp.dot(h_scr[...], w_ref[...], preferred_element_type=F32)
    o_ref[...] = (acc * jnp.where(j < nb, ATTN_SCALE, 1.0)).astype(o_ref.dtype)


def _in_proj1(x, norm_g, mods, row_of_tile, w, out_dtype, tm, tn):
    t = x.shape[0]
    nb = D_MODEL // tn
    return pl.pallas_call(
        functools.partial(_in1_kernel, nb=nb),
        grid=(t // tm, 3 * nb),
        in_specs=[
            pl.BlockSpec((tm, D_MODEL), lambda i, j: (i, 0)),
            pl.BlockSpec((1, D_MODEL), lambda i, j: (0, 0)),
            _mod_spec(0, row_of_tile),
            _mod_spec(1, row_of_tile),
            pl.BlockSpec((D_MODEL, tn), lambda i, j: (0, j)),
        ],
        out_specs=pl.BlockSpec((None, tm, tn), lambda i, j: (j // nb, i, j % nb)),
        out_shape=jax.ShapeDtypeStruct((3, t, D_MODEL), out_dtype),
        scratch_shapes=[pltpu.VMEM((tm, D_MODEL), BF16)],
        compiler_params=_cparams(2),
        name="in_proj1",
    )(x, norm_g, mods, mods, w)


def _in1q_kernel(x_ref, g_ref, shift_ref, scale_ref, w_ref, q_ref, h_ref):
    @pl.when(pl.program_id(1) == 0)
    def _():
        _ada_norm_rows(h_ref, x_ref, g_ref, shift_ref, scale_ref)

    acc = jnp.dot(h_ref[...], w_ref[...], preferred_element_type=F32)
    q_ref[...] = (acc * ATTN_SCALE).astype(q_ref.dtype)


def _in_proj1_q(x, norm_g, mods, row_of_tile, w, tm, tn):
    t = x.shape[0]
    return pl.pallas_call(
        _in1q_kernel,
        grid=(t // tm, D_MODEL // tn),
        in_specs=[
            pl.BlockSpec((tm, D_MODEL), lambda i, j: (i, 0)),
            pl.BlockSpec((1, D_MODEL), lambda i, j: (0, 0)),
            _mod_spec(0, row_of_tile),
            _mod_spec(1, row_of_tile),
            pl.BlockSpec((D_MODEL, tn), lambda i, j: (0, j)),
        ],
        out_specs=[pl.BlockSpec((tm, tn), lambda i, j: (i, j)),
                   pl.BlockSpec((tm, D_MODEL), lambda i, j: (i, 0))],
        out_shape=[jax.ShapeDtypeStruct((t, D_MODEL), BF16), jax.ShapeDtypeStruct((t, D_MODEL), BF16)],
        compiler_params=_cparams(2),
        name="in_proj1_q",
    )(x, norm_g, mods, mods, w)


def _in1kv_kernel(h_ref, w_ref, flat_ref, heads_ref):
    acc = jnp.dot(h_ref[...], w_ref[...], preferred_element_type=F32)
    flat_ref[...] = acc.astype(flat_ref.dtype)
    heads_ref[...] = acc.reshape(heads_ref.shape)


def _in_proj1_kv(h, w, part, tm, tn):
    t = h.shape[0]
    nb = D_MODEL // tn
    hb = tn // HEAD_DIM
    return pl.pallas_call(
        _in1kv_kernel,
        grid=(t // tm, nb),
        in_specs=[
            pl.BlockSpec((tm, D_MODEL), lambda i, j: (i, 0)),
            pl.BlockSpec((D_MODEL, tn), lambda i, j: (0, part * nb + j)),
        ],
        out_specs=[pl.BlockSpec((tm, tn), lambda i, j: (i, j)),
                   pl.BlockSpec((tm, hb, HEAD_DIM), lambda i, j: (i, j, 0))],
        out_shape=[jax.ShapeDtypeStruct((t, D_MODEL), BF16),
                   jax.ShapeDtypeStruct((t, NA_HEADS, HEAD_DIM), F32)],
        compiler_params=_cparams(2),
        name="in_proj1_kv",
    )(h, w)


def _gated_conv(zc, prev_row, next_row, cw):
    c = CONV_WIDTH
    s = zc.shape[0]
    gb = zc[:, 0:c].astype(F32)
    u = zc[:, c:2 * c].astype(F32) * zc[:, 2 * c:3 * c].astype(F32)
    row = lax.broadcasted_iota(jnp.int32, u.shape, 0)
    prev = jnp.where(row == 0, prev_row, pltpu.roll(u, 1, 0))
    nxt = jnp.where(row == s - 1, next_row, pltpu.roll(u, s - 1, 0))
    return gb * (prev * cw[0:1] + u * cw[1:2] + nxt * cw[2:3])


def _attn0_ctx_kernel(zc_ref, q_ref, kv_ref, cw_ref, *rest):
    n_cast = len(rest) // 2
    o_ref = rest[n_cast]
    _side_cast(rest[:n_cast], rest[n_cast + 1:])
    zero = jnp.zeros((1, CONV_WIDTH), F32)
    a = _gated_conv(zc_ref[...], zero, zero, cw_ref[...])
    o_ref[:, 0:CONV_WIDTH] = a.astype(o_ref.dtype)
    for g in range(GQA_KV_HEADS):
        k = kv_ref[:, g * HEAD_DIM:(g + 1) * HEAD_DIM].astype(BF16)
        v = kv_ref[:, KV_W + g * HEAD_DIM:KV_W + (g + 1) * HEAD_DIM].astype(BF16)
        for hh in range(GQA_GROUP):
            h = g * GQA_GROUP + hh
            q = q_ref[:, h * HEAD_DIM:(h + 1) * HEAD_DIM]
            o = _softmax_pv([_nt_dot(q, k)], [v])
            o_ref[:, CONV_WIDTH + h * HEAD_DIM:CONV_WIDTH + (h + 1) * HEAD_DIM] = o.astype(o_ref.dtype)


def _attn0_ctx(zc, q, kv, cw, cast_list):
    t = zc.shape[0]
    c_in, c_args, c_out, c_shape = _side_cast_specs(cast_list, t // SEQ, lambda b: b)
    return pl.pallas_call(
        _attn0_ctx_kernel,
        grid=(t // SEQ,),
        in_specs=[
            pl.BlockSpec((SEQ, 3 * CONV_WIDTH), lambda b: (b, 0)),
            pl.BlockSpec((SEQ, GQA_HEADS * HEAD_DIM), lambda b: (b, 0)),
            pl.BlockSpec((SEQ, 2 * KV_W), lambda b: (b, 0)),
            pl.BlockSpec((3, CONV_WIDTH), lambda b: (0, 0)),
        ] + c_in,
        out_specs=[pl.BlockSpec((SEQ, D_MODEL), lambda b: (b, 0))] + c_out,
        out_shape=[jax.ShapeDtypeStruct((t, D_MODEL), BF16)] + c_shape,
        compiler_params=_cparams(1),
        name="mixer0_ctx",
    )(zc, q, kv, cw, *c_args)


_HALO = 16


def _side_cast_specs(cast_list, n_steps, step_of):
    in_specs, args, out_specs, out_shape = [], [], [], []
    for w_f32, layer in cast_list:
        _, rows, cols = w_f32.shape
        r = rows // n_steps
        in_specs.append(pl.BlockSpec((None, r, cols),
                                     functools.partial(lambda *g, l: (l, step_of(*g), 0), l=layer)))
        args.append(w_f32)
        out_specs.append(pl.BlockSpec((r, cols), lambda *g: (step_of(*g), 0)))
        out_shape.append(jax.ShapeDtypeStruct((rows, cols), BF16))
    return in_specs, args, out_specs, out_shape


def _side_cast(src_refs, dst_refs):
    for src_ref, dst_ref in zip(src_refs, dst_refs):
        dst_ref[...] = src_ref[...].astype(BF16)


def _attn0_lat_kernel(zc_ref, zp_ref, zn_ref, q_ref, kv_ref, ck_ref, cv_ref, cw_ref, *rest, nqb):
    n_cast = len(rest) // 2
    o_ref = rest[n_cast]
    _side_cast(rest[:n_cast], rest[n_cast + 1:])
    qb = pl.program_id(1)
    c = CONV_WIDTH
    up = zp_ref[:, c:2 * c].astype(F32) * zp_ref[:, 2 * c:3 * c].astype(F32)
    un = zn_ref[:, c:2 * c].astype(F32) * zn_ref[:, 2 * c:3 * c].astype(F32)
    up = jnp.where(qb == 0, 0.0, up[_HALO - 1:_HALO, :])
    un = jnp.where(qb == nqb - 1, 0.0, un[0:1, :])
    a = _gated_conv(zc_ref[...], up, un, cw_ref[...])
    o_ref[:, 0:c] = a.astype(o_ref.dtype)
    for g in range(GQA_KV_HEADS):
        ks = slice(g * HEAD_DIM, (g + 1) * HEAD_DIM)
        vs = slice(KV_W + g * HEAD_DIM, KV_W + (g + 1) * HEAD_DIM)
        k = kv_ref[:, ks]
        v = kv_ref[:, vs]
        ck = ck_ref[:, ks].astype(BF16)
        cv = cv_ref[:, ks].astype(BF16)
        for hh in range(GQA_GROUP):
            h = g * GQA_GROUP + hh
            q = q_ref[:, h * HEAD_DIM:(h + 1) * HEAD_DIM]
            o = _softmax_pv([_nt_dot(q, ck), _nt_dot(q, k)], [cv, v])
            o_ref[:, c + h * HEAD_DIM:c + (h + 1) * HEAD_DIM] = o.astype(o_ref.dtype)


def _attn0_lat(zc, q, kv, ck, cv, cw, tq, cast_list):
    nb, t, _ = zc.shape
    nqb = t // tq
    hb = tq // _HALO
    c_in, c_args, c_out, c_shape = _side_cast_specs(cast_list, nb * nqb, lambda b, i: b * nqb + i)
    return pl.pallas_call(
        functools.partial(_attn0_lat_kernel, nqb=nqb),
        grid=(nb, nqb),
        in_specs=[
            pl.BlockSpec((None, tq, 3 * CONV_WIDTH), lambda b, i: (b, i, 0)),
            pl.BlockSpec((None, _HALO, 3 * CONV_WIDTH), lambda b, i: (b, jnp.maximum(i * hb - 1, 0), 0)),
            pl.BlockSpec((None, _HALO, 3 * CONV_WIDTH),
                         lambda b, i: (b, jnp.minimum((i + 1) * hb, t // _HALO - 1), 0)),
            pl.BlockSpec((None, tq, GQA_HEADS * HEAD_DIM), lambda b, i: (b, i, 0)),
            pl.BlockSpec((None, t, 2 * KV_W), lambda b, i: (b, 0, 0)),
            pl.BlockSpec((None, PAST_LEN, KV_W), lambda b, i: (b, 0, 0)),
            pl.BlockSpec((None, PAST_LEN, KV_W), lambda b, i: (b, 0, 0)),
            pl.BlockSpec((3, CONV_WIDTH), lambda b, i: (0, 0)),
        ] + c_in,
        out_specs=[pl.BlockSpec((None, tq, D_MODEL), lambda b, i: (b, i, 0))] + c_out,
        out_shape=[jax.ShapeDtypeStruct((nb, t, D_MODEL), BF16)] + c_shape,
        compiler_params=_cparams(2),
        name="mixer0_lat",
    )(zc, zc, zc, q, kv, ck, cv, cw, *c_args)


def _attn1_ctx_kernel(q_ref, k_ref, v_ref, o_ref):
    for h in range(NA_HEADS):
        sl = slice(h * HEAD_DIM, (h + 1) * HEAD_DIM)
        o = _softmax_pv([_nt_dot(q_ref[:, sl], k_ref[:, sl])], [v_ref[:, sl]])
        o_ref[:, sl] = o.astype(o_ref.dtype)


def _attn1_ctx(q, k, v):
    t = q.shape[0]
    blk = pl.BlockSpec((SEQ, D_MODEL), lambda b: (b, 0))
    return pl.pallas_call(
        _attn1_ctx_kernel,
        grid=(t // SEQ,),
        in_specs=[blk, blk, blk],
        out_specs=blk,
        out_shape=jax.ShapeDtypeStruct((t, D_MODEL), BF16),
        compiler_params=_cparams(1),
        name="mixer1_ctx",
    )(q, k, v)


def _na_slab_row(blk):
    return min(max(NA_QROWS * blk - NA_WIN_ROWS // 2, 0), GRID_H - NA_SLAB_ROWS)


NA_NROW = 2 * NA_WIN_ROWS - 1
NA_MASKED = NA_NROW
NA_FIRST_IN = NA_WIN_ROWS - 1 - NA_WIN_ROWS // 2
NA_LAST_IN = NA_FIRST_IN + NA_WIN_ROWS - 1
NA_PIECE_PAIRS = ([(m, m + 1) for m in range(NA_NROW - 1)]
                  + [(NA_MASKED, NA_FIRST_IN), (NA_LAST_IN, NA_MASKED)])


def _na_piece(blk, dr, jp):
    r = NA_QROWS * blk + dr
    ks = min(max(r - NA_WIN_ROWS // 2, 0), GRID_H - NA_WIN_ROWS)
    pair = []
    for j in (2 * jp, 2 * jp + 1):
        krow = _na_slab_row(blk) + j
        pair.append(krow - r + NA_WIN_ROWS - 1 if ks <= krow < ks + NA_WIN_ROWS else NA_MASKED)
    pair = tuple(pair)
    if pair == (NA_MASKED, NA_MASKED):
        return None
    return NA_PIECE_PAIRS.index(pair)


NA_PAD = GRID_W - NA_WIN_COLS


def _na_bias_table(rel_bias):
    return jnp.pad(rel_bias, ((0, 0), (0, 1), (NA_PAD, 2 * GRID_W - (2 * NA_WIN_COLS - 1) - NA_PAD)),
                   constant_values=MASK_NEG)


def _na_fill_pieces(table_ref, pieces_ref):
    shape = (GRID_W, 2 * GRID_W)
    c = lax.broadcasted_iota(jnp.int32, shape, 0)
    lane = lax.broadcasted_iota(jnp.int32, shape, 1)
    kc = lane & (GRID_W - 1)
    cs = jnp.clip(c - NA_WIN_COLS // 2, 0, GRID_W - NA_WIN_COLS)
    in_window = (kc >= cs) & (kc < cs + NA_WIN_COLS)

    def expand(a, shift):
        row = jnp.broadcast_to(table_ref[a:a + 1, :], shape)
        return pltpu.roll(row, shift, 1, stride=1, stride_axis=0)

    left_shift = 2 * GRID_W - (GRID_W - 1)
    right_shift = 1
    for m, (a_left, a_right) in enumerate(NA_PIECE_PAIRS):
        both = jnp.where(lane < GRID_W, expand(a_left, left_shift), expand(a_right, right_shift))
        pieces_ref[m] = jnp.where(in_window, both * LOG2E, MASK_NEG)


def _na_bias_block(pieces_ref, blk):
    rows = []
    for dr in range(NA_QROWS):
        cols = []
        for jp in range(NA_SLAB_ROWS // 2):
            m = _na_piece(blk, dr, jp)
            cols.append(jnp.full((GRID_W, 2 * GRID_W), MASK_NEG, F32) if m is None else pieces_ref[m])
        rows.append(jnp.concatenate(cols, axis=1))
    return jnp.concatenate(rows, axis=0)


def _attn1_lat_kernel(q_ref, k_ref, v_ref, ck_ref, cv_ref, table_ref, o_ref, pieces_ref):
    _na_fill_pieces(table_ref, pieces_ref)
    ck = ck_ref[...].astype(BF16)
    cv = cv_ref[...].astype(BF16)
    for blk in range(NA_NBLK):
        rows = slice(blk * NA_QBLK, (blk + 1) * NA_QBLK)
        s0 = _na_slab_row(blk) * GRID_W
        slab = slice(s0, s0 + NA_SLAB)
        q = q_ref[rows, :]
        s_loc = _nt_dot(q, k_ref[slab, :]) + _na_bias_block(pieces_ref, blk)
        o = _softmax_pv([s_loc, _nt_dot(q, ck)], [v_ref[slab, :], cv])
        o_ref[rows, :] = o.astype(o_ref.dtype)


def _attn1_lat(qkv, ck, cv, table):
    _, nb, t, _ = qkv.shape
    part = lambda p: pl.BlockSpec((None, None, t, HEAD_DIM), lambda b, h: (p, b, 0, h))
    head = pl.BlockSpec((None, t, HEAD_DIM), lambda b, h: (b, 0, h))
    ctx = pl.BlockSpec((None, PAST_LEN, HEAD_DIM), lambda b, h: (b, 0, h))
    return pl.pallas_call(
        _attn1_lat_kernel,
        grid=(nb, NA_HEADS),
        in_specs=[part(0), part(1), part(2), ctx, ctx,
                  pl.BlockSpec((None, NA_NROW + 1, 2 * GRID_W), lambda b, h: (h, 0, 0))],
        out_specs=head,
        out_shape=jax.ShapeDtypeStruct((nb, t, D_MODEL), BF16),
        scratch_shapes=[pltpu.VMEM((len(NA_PIECE_PAIRS), GRID_W, 2 * GRID_W), F32)],
        compiler_params=_cparams(2),
        name="mixer1_lat",
    )(qkv, qkv, qkv, ck, cv, table)


def _out_proj_kernel(m_ref, w_ref, x_ref, gate_ref, o_ref):
    acc = jnp.dot(m_ref[...], w_ref[...], preferred_element_type=F32)
    o_ref[...] = x_ref[...] + gate_ref[...] * acc


def _out_proj(mix, w, x, mods, row_of_tile, tm, tn):
    t, k = mix.shape
    return pl.pallas_call(
        _out_proj_kernel,
        grid=(t // tm, D_MODEL // tn),
        in_specs=[
            pl.BlockSpec((tm, k), lambda i, j: (i, 0)),
            pl.BlockSpec((k, tn), lambda i, j: (0, j)),
            pl.BlockSpec((tm, tn), lambda i, j: (i, j)),
            pl.BlockSpec((None, None, 1, tn), lambda i, j: (row_of_tile(i), 2, 0, j)),
        ],
        out_specs=pl.BlockSpec((tm, tn), lambda i, j: (i, j)),
        out_shape=jax.ShapeDtypeStruct((t, D_MODEL), F32),
        compiler_params=_cparams(2),
        name="out_proj",
    )(mix, w, x, mods)


def _mlp_kernel(x_ref, g_ref, shift_ref, scale_ref, gate_ref, w1_ref, w2_ref, fg_ref, *rest, nf, final):
    n_cast = (len(rest) - 2) // 2
    o_ref, h_scr = rest[n_cast], rest[-1]
    _side_cast(rest[:n_cast], rest[n_cast + 1:-1])
    f = pl.program_id(1)

    @pl.when(f == 0)
    def _():
        _ada_norm_rows(h_scr, x_ref, g_ref, shift_ref, scale_ref)
        o_ref[...] = jnp.zeros_like(o_ref)

    u = jnp.dot(h_scr[...], w1_ref[...], preferred_element_type=F32)
    u = jnp.square(jnp.maximum(u, 0.0)).astype(BF16)
    o_ref[...] += jnp.dot(u, w2_ref[...], preferred_element_type=F32)

    @pl.when(f == nf - 1)
    def _():
        y = x_ref[...] + gate_ref[...] * o_ref[...]
        if final:
            ms = jnp.mean(y * y, axis=-1, keepdims=True)
            y = y * lax.rsqrt(ms + NORM_EPS) * fg_ref[...]
        o_ref[...] = y


def _mlp(x, norm_g, mods, row_of_tile, w1, w2, final_g, final, tm, tf, cast_next=None):
    t = x.shape[0]
    nf = D_FF // tf
    nt = t // tm

    def mod(which):
        return pl.BlockSpec((None, None, 1, D_MODEL), lambda i, f: (row_of_tile(i), which, 0, 0))

    in_specs = [
        pl.BlockSpec((tm, D_MODEL), lambda i, f: (i, 0)),
        pl.BlockSpec((1, D_MODEL), lambda i, f: (0, 0)),
        mod(3), mod(4), mod(5),
        pl.BlockSpec((D_MODEL, tf), lambda i, f: (0, f)),
        pl.BlockSpec((tf, D_MODEL), lambda i, f: (f, 0)),
        pl.BlockSpec((1, D_MODEL), lambda i, f: (0, 0)),
    ]
    args = [x, norm_g, mods, mods, mods, w1, w2, final_g]
    out_specs = [pl.BlockSpec((tm, D_MODEL), lambda i, f: (i, 0))]
    out_shape = [jax.ShapeDtypeStruct((t, D_MODEL), F32)]
    c_in, c_args, c_out, c_shape = _side_cast_specs(cast_next or (), nt * nf, lambda i, f: i * nf + f)
    in_specs += c_in
    args += c_args
    out_specs += c_out
    out_shape += c_shape
    res = pl.pallas_call(
        functools.partial(_mlp_kernel, nf=nf, final=final),
        grid=(nt, nf),
        in_specs=in_specs,
        out_specs=out_specs,
        out_shape=out_shape,
        scratch_shapes=[pltpu.VMEM((tm, D_MODEL), BF16)],
        compiler_params=_cparams(2),
        name="mlp",
    )(*args)
    return res if cast_next else res[0]


def _rope_tables():
    t = np.arange(DEC_SEQ)
    half = HEAD_DIM // 4
    inv = ROPE_THETA ** (-np.arange(half, dtype=np.float32) / half)
    ang_r = (t // GRID_W).astype(np.float32)[:, None] * inv
    ang_c = (t % GRID_W).astype(np.float32)[:, None] * inv
    zero = np.zeros_like(ang_r)
    cos = np.concatenate([np.cos(ang_r)] * 2 + [np.cos(ang_c)] * 2, axis=-1)
    sin_lo = np.concatenate([-np.sin(ang_r), zero, -np.sin(ang_c), zero], axis=-1)
    sin_hi = np.concatenate([zero, np.sin(ang_r), zero, np.sin(ang_c)], axis=-1)
    return tuple(jnp.asarray(a, F32) for a in (cos, sin_lo, sin_hi))


def kernel(x_prompt, x_sample, cache_attn_k, cache_attn_v, cache_na_k, cache_na_v, c, c_ctx, mod_w, mod_b,
           norm1_g, norm2_g, ab_w_in, ab_conv_w, ab_q_norm, ab_k_norm, ab_w_out, na_w_qkv, na_rel_bias,
           na_w_out, mlp_w1, mlp_w2, final_norm_g):
    n_ctx = BATCH * SEQ
    xp = x_prompt.reshape(n_ctx, D_MODEL)
    xs = x_sample.reshape(DEC_BATCH * DEC_SEQ, D_MODEL)

    cond = jnp.concatenate([c_ctx[None, :], c, jnp.zeros((8 - 1 - DEC_BATCH, D_MODEL), F32)], axis=0)
    mods = _modulation(cond, mod_w, mod_b).reshape(2, 8, 6, 1, D_MODEL)

    tm_in, tm_mlp, tf = 512, 512, 1024
    tm_out, tn_out = 512, D_MODEL

    def ctx_row(i):
        return 0

    def lat_row(tm):
        return lambda i: 1 + i // (DEC_SEQ // tm)

    rope_tabs = _rope_tables()
    fg = final_norm_g.reshape(1, D_MODEL)

    m0 = mods[0]
    g1 = norm1_g[0].reshape(1, D_MODEL)
    w_in = ab_w_in[0].astype(BF16)
    qn = ab_q_norm[0].reshape(1, HEAD_DIM)
    kn = ab_k_norm[0].reshape(1, HEAD_DIM)
    cw = ab_conv_w[0]

    zc_p, q_p, kv_p = _in_proj0(xp, g1, m0, ctx_row, w_in, qn, kn, None, F32, tm_in)
    zc_s, q_s, kv_s = _in_proj0(xs, g1, m0, lat_row(tm_in), w_in, qn, kn, rope_tabs, BF16, tm_in)
    new_attn_k = kv_p[:, :KV_W].reshape(BATCH, 1, SEQ, GQA_KV_HEADS, HEAD_DIM)
    new_attn_v = kv_p[:, KV_W:].reshape(BATCH, 1, SEQ, GQA_KV_HEADS, HEAD_DIM)

    mix_p, w_out = _attn0_ctx(zc_p, q_p, kv_p, cw, [(ab_w_out, 0)])
    lat3 = lambda a: a.reshape(DEC_BATCH, DEC_SEQ, a.shape[-1])
    mix_s, w1, w2 = _attn0_lat(lat3(zc_s), lat3(q_s), lat3(kv_s),
                               cache_attn_k[:, 0].reshape(DEC_BATCH, PAST_LEN, KV_W),
                               cache_attn_v[:, 0].reshape(DEC_BATCH, PAST_LEN, KV_W), cw, 256,
                               [(mlp_w1, 0), (mlp_w2, 0)])
    mix_s = mix_s.reshape(DEC_BATCH * DEC_SEQ, D_MODEL)

    xp = _out_proj(mix_p, w_out, xp, m0, ctx_row, tm_out, tn_out)
    xs = _out_proj(mix_s, w_out, xs, m0, lat_row(tm_out), tm_out, tn_out)

    g2 = norm2_g[0].reshape(1, D_MODEL)
    xp, w1_next, w2_next = _mlp(xp, g2, m0, ctx_row, w1, w2, fg, False, tm_mlp, tf,
                                cast_next=[(mlp_w1, 1), (mlp_w2, 1)])
    xs, w_qkv, w_out = _mlp(xs, g2, m0, lat_row(tm_mlp), w1, w2, fg, False, tm_mlp, tf,
                            cast_next=[(na_w_qkv, 0), (na_w_out, 0)])

    m1 = mods[1]
    g1 = norm1_g[1].reshape(1, D_MODEL)

    q_p, h_p = _in_proj1_q(xp, g1, m1, ctx_row, w_qkv, 512, D_MODEL)
    k_p, k_heads = _in_proj1_kv(h_p, w_qkv, 1, 512, D_MODEL)
    v_p, v_heads = _in_proj1_kv(h_p, w_qkv, 2, 512, D_MODEL)
    qkv_s = _in_proj1(xs, g1, m1, lat_row(1024), w_qkv, BF16, 1024, 1024)
    new_na_k = k_heads.reshape(BATCH, 1, SEQ, NA_HEADS, HEAD_DIM)
    new_na_v = v_heads.reshape(BATCH, 1, SEQ, NA_HEADS, HEAD_DIM)

    mix_p = _attn1_ctx(q_p, k_p, v_p)
    mix_s = _attn1_lat(qkv_s.reshape(3, DEC_BATCH, DEC_SEQ, D_MODEL),
                       cache_na_k[:, 0].reshape(DEC_BATCH, PAST_LEN, D_MODEL),
                       cache_na_v[:, 0].reshape(DEC_BATCH, PAST_LEN, D_MODEL),
                       _na_bias_table(na_rel_bias[0]))
    mix_s = mix_s.reshape(DEC_BATCH * DEC_SEQ, D_MODEL)

    xp = _out_proj(mix_p, w_out, xp, m1, ctx_row, tm_out, tn_out)
    xs = _out_proj(mix_s, w_out, xs, m1, lat_row(tm_out), tm_out, tn_out)

    g2 = norm2_g[1].reshape(1, D_MODEL)
    yp = _mlp(xp, g2, m1, ctx_row, w1_next, w2_next, fg, True, tm_mlp, tf)
    ys = _mlp(xs, g2, m1, lat_row(tm_mlp), w1_next, w2_next, fg, True, tm_mlp, tf)

    return (yp.reshape(BATCH, SEQ, D_MODEL), ys.reshape(DEC_BATCH, DEC_SEQ, D_MODEL),
            new_attn_k, new_attn_v, new_na_k, new_na_v)
```

```python
import functools
import math

import numpy as np
import jax
import jax.numpy as jnp
from jax import lax
from jax.experimental import pallas as pl
from jax.experimental.pallas import tpu as pltpu

D_MODEL = 2048
BATCH = 32
SEQ = 256
DEC_BATCH = 2
DEC_SEQ = 2048
PAST_LEN = 512
GRID_W = 64
GRID_H = DEC_SEQ // GRID_W
HEAD_DIM = 128
CONV_WIDTH = D_MODEL // 2
GQA_HEADS = 8
GQA_KV_HEADS = 2
GQA_GROUP = GQA_HEADS // GQA_KV_HEADS
NA_HEADS = 16
NA_WIN_ROWS = 8
NA_WIN_COLS = 16
D_FF = 4 * D_MODEL
ROPE_THETA = 10000.0
NORM_EPS = 1e-6
AB_IN = 3 * CONV_WIDTH + (GQA_HEADS + 2 * GQA_KV_HEADS) * HEAD_DIM
KV_W = GQA_KV_HEADS * HEAD_DIM
LOG2E = math.log2(math.e)
ATTN_SCALE = HEAD_DIM ** -0.5 * LOG2E

NA_QROWS = 4
NA_SLAB_ROWS = NA_QROWS + NA_WIN_ROWS
NA_QBLK = NA_QROWS * GRID_W
NA_SLAB = NA_SLAB_ROWS * GRID_W
NA_NBLK = GRID_H // NA_QROWS
MASK_NEG = -1e30

V7X_VMEM_LIMIT = 56 * 1024 * 1024

BF16 = jnp.bfloat16
F32 = jnp.float32


def _cparams(n_axes, ordered=False):
    first = "arbitrary" if ordered else "parallel"
    return pltpu.CompilerParams(
        dimension_semantics=(first,) + ("arbitrary",) * (n_axes - 1),
        vmem_limit_bytes=V7X_VMEM_LIMIT)


def _nt_dot(a, b):
    return lax.dot_general(a, b, (((1,), (1,)), ((), ())), preferred_element_type=F32)


NORM_ROWS = 16


def _ada_norm_rows(h_ref, x_ref, g_ref, shift_ref, scale_ref):
    gain = g_ref[...] * (1.0 + scale_ref[...])
    shift = shift_ref[...]

    def body(c, carry):
        rows = pl.ds(pl.multiple_of(c * NORM_ROWS, NORM_ROWS), NORM_ROWS)
        x = x_ref[rows, :]
        ms = jnp.mean(x * x, axis=-1, keepdims=True)
        h_ref[rows, :] = (x * lax.rsqrt(ms + NORM_EPS) * gain + shift).astype(h_ref.dtype)
        return carry

    lax.fori_loop(0, x_ref.shape[0] // NORM_ROWS, body, 0, unroll=8)


def _ada_norm_slice(h_ref, x_ref, g_ref, shift_ref, scale_ref, row0, n_rows):
    gain = g_ref[...] * (1.0 + scale_ref[...])
    shift = shift_ref[...]
    for c in range(n_rows // NORM_ROWS):
        rows = pl.ds(pl.multiple_of(row0 + c * NORM_ROWS, NORM_ROWS), NORM_ROWS)
        x = x_ref[rows, :]
        ms = jnp.mean(x * x, axis=-1, keepdims=True)
        h_ref[rows, :] = (x * lax.rsqrt(ms + NORM_EPS) * gain + shift).astype(h_ref.dtype)


def _head_norm(x, g):
    ms = jnp.mean(x * x, axis=-1, keepdims=True)
    return x * lax.rsqrt(ms + NORM_EPS) * g


def _rope(x, cos, sin_lo, sin_hi):
    return x * cos + pltpu.roll(x, HEAD_DIM - 32, 1) * sin_lo + pltpu.roll(x, 32, 1) * sin_hi


def _softmax_pv(scores, values):
    m = scores[0].max(axis=-1, keepdims=True)
    for s in scores[1:]:
        m = jnp.maximum(m, s.max(axis=-1, keepdims=True))
    l = None
    o = None
    for s, v in zip(scores, values):
        p = jnp.exp2(s - m)
        ls = jnp.sum(p, axis=-1, keepdims=True)
        os_ = jnp.dot(p.astype(BF16), v, preferred_element_type=F32)
        l = ls if l is None else l + ls
        o = os_ if o is None else o + os_
    return o / l


def _mod_kernel(c_ref, w_ref, b_ref, o_ref):
    c = c_ref[...]
    s = (c * jax.nn.sigmoid(c)).astype(BF16)
    o_ref[...] = jnp.dot(s, w_ref[...].astype(BF16), preferred_element_type=F32) + b_ref[...]


def _modulation(cond, mod_w, mod_b):
    depth, d, n = mod_w.shape
    tn = 1024
    rows = cond.shape[0]
    return pl.pallas_call(
        _mod_kernel,
        grid=(depth, n // tn),
        in_specs=[
            pl.BlockSpec((rows, d), lambda l, j: (0, 0)),
            pl.BlockSpec((None, d, tn), lambda l, j: (l, 0, j)),
            pl.BlockSpec((None, 1, tn), lambda l, j: (l, 0, j)),
        ],
        out_specs=pl.BlockSpec((None, rows, tn), lambda l, j: (l, 0, j)),
        out_shape=jax.ShapeDtypeStruct((depth, rows, n), F32),
        compiler_params=_cparams(2),
        name="modulation",
    )(cond, mod_w, mod_b.reshape(depth, 1, n))


def _mod_spec(which, row_of_tile):
    return pl.BlockSpec((None, None, 1, D_MODEL), lambda i, j: (row_of_tile(i), which, 0, 0))


Q_W = GQA_HEADS * HEAD_DIM
IN0_TN = Q_W + 2 * KV_W


def _in0_kernel(x_ref, xn_ref, g_ref, shift_ref, scale_ref, shiftn_ref, scalen_ref, w_ref, qn_ref, kn_ref, *rest,
                rope):
    if rope:
        cos_ref, slo_ref, shi_ref, zc_ref, q_ref, kv_ref, h0_scr, h1_scr = rest
    else:
        zc_ref, q_ref, kv_ref, h0_scr, h1_scr = rest
    i = pl.program_id(0)
    j = pl.program_id(1)
    n_zc = 3 * CONV_WIDTH // IN0_TN
    n_steps = AB_IN // IN0_TN
    tm = x_ref.shape[0]
    slice_rows = -(-tm // (n_steps * NORM_ROWS)) * NORM_ROWS

    @pl.when((i == 0) & (j == 0))
    def _():
        _ada_norm_rows(h0_scr, x_ref, g_ref, shift_ref, scale_ref)

    def normed(blk, gain):
        y = _head_norm(blk, gain)
        if rope:
            y = _rope(y, cos_ref[...], slo_ref[...], shi_ref[...])
        return y

    def step(h_cur, h_nxt):
        _ada_norm_slice(h_nxt, xn_ref, g_ref, shiftn_ref, scalen_ref,
                        jnp.minimum(j * slice_rows, tm - slice_rows), slice_rows)
        acc = jnp.dot(h_cur[...], w_ref[...], preferred_element_type=F32)
        zc_ref[...] = acc.astype(zc_ref.dtype)

        @pl.when(j == n_zc)
        def _():
            for hh in range(GQA_HEADS):
                sl = slice(hh * HEAD_DIM, (hh + 1) * HEAD_DIM)
                q_ref[:, sl] = (normed(acc[:, sl], qn_ref[...]) * ATTN_SCALE).astype(q_ref.dtype)
            for hh in range(GQA_KV_HEADS):
                src = slice(Q_W + hh * HEAD_DIM, Q_W + (hh + 1) * HEAD_DIM)
                kv_ref[:, hh * HEAD_DIM:(hh + 1) * HEAD_DIM] = normed(acc[:, src], kn_ref[...]).astype(kv_ref.dtype)
            kv_ref[:, KV_W:] = acc[:, Q_W + KV_W:].astype(kv_ref.dtype)

    pl.when(i % 2 == 0)(functools.partial(step, h0_scr, h1_scr))
    pl.when(i % 2 == 1)(functools.partial(step, h1_scr, h0_scr))


def _in_proj0(x, norm_g, mods, row_of_tile, w, qn, kn, rope_tabs, kv_dtype, tm):
    t = x.shape[0]
    tn = IN0_TN
    rope = rope_tabs is not None
    n_tiles = t // tm
    nxt = lambda i: jnp.minimum(i + 1, n_tiles - 1)
    row_of_next = lambda i: row_of_tile(nxt(i))
    vec = pl.BlockSpec((1, HEAD_DIM), lambda i, j: (0, 0))
    in_specs = [
        pl.BlockSpec((tm, D_MODEL), lambda i, j: (i, 0)),
        pl.BlockSpec((tm, D_MODEL), lambda i, j: (nxt(i), 0)),
        pl.BlockSpec((1, D_MODEL), lambda i, j: (0, 0)),
        _mod_spec(0, row_of_tile),
        _mod_spec(1, row_of_tile),
        _mod_spec(0, row_of_next),
        _mod_spec(1, row_of_next),
        pl.BlockSpec((D_MODEL, tn), lambda i, j: (0, j)),
        vec, vec,
    ]
    args = [x, x, norm_g, mods, mods, mods, mods, w, qn, kn]
    if rope:
        nt = DEC_SEQ // tm
        tab = pl.BlockSpec((tm, HEAD_DIM), lambda i, j: (i % nt, 0))
        in_specs += [tab, tab, tab]
        args += list(rope_tabs)
    return pl.pallas_call(
        functools.partial(_in0_kernel, rope=rope),
        grid=(t // tm, AB_IN // tn),
        in_specs=in_specs,
        out_specs=[
            pl.BlockSpec((tm, tn), lambda i, j: (i, j)),
            pl.BlockSpec((tm, Q_W), lambda i, j: (i, 0)),
            pl.BlockSpec((tm, 2 * KV_W), lambda i, j: (i, 0)),
        ],
        out_shape=[
            jax.ShapeDtypeStruct((t, AB_IN), BF16),
            jax.ShapeDtypeStruct((t, GQA_HEADS * HEAD_DIM), BF16),
            jax.ShapeDtypeStruct((t, 2 * KV_W), kv_dtype),
        ],
        scratch_shapes=[pltpu.VMEM((tm, D_MODEL), BF16), pltpu.VMEM((tm, D_MODEL), BF16)],
        compiler_params=_cparams(2, ordered=True),
        name="in_proj0",
    )(*args)


def _in1_kernel(x_ref, g_ref, shift_ref, scale_ref, w_ref, o_ref, h_scr, *, nb):
    j = pl.program_id(1)

    @pl.when(j == 0)
    def _():
        _ada_norm_rows(h_scr, x_ref, g_ref, shift_ref, scale_ref)

    acc = jnp.dot(h_scr[...], w_ref[...], preferred_element_type=F32)
    o_ref[...] = (acc * jnp.where(j < nb, ATTN_SCALE, 1.0)).astype(o_ref.dtype)


def _in_proj1(x, norm_g, mods, row_of_tile, w, out_dtype, tm, tn):
    t = x.shape[0]
    nb = D_MODEL // tn
    return pl.pallas_call(
        functools.partial(_in1_kernel, nb=nb),
        grid=(t // tm, 3 * nb),
        in_specs=[
            pl.BlockSpec((tm, D_MODEL), lambda i, j: (i, 0)),
            pl.BlockSpec((1, D_MODEL), lambda i, j: (0, 0)),
            _mod_spec(0, row_of_tile),
            _mod_spec(1, row_of_tile),
            pl.BlockSpec((D_MODEL, tn), lambda i, j: (0, j)),
        ],
        out_specs=pl.BlockSpec((None, tm, tn), lambda i, j: (j // nb, i, j % nb)),
        out_shape=jax.ShapeDtypeStruct((3, t, D_MODEL), out_dtype),
        scratch_shapes=[pltpu.VMEM((tm, D_MODEL), BF16)],
        compiler_params=_cparams(2),
        name="in_proj1",
    )(x, norm_g, mods, mods, w)


def _in1q_kernel(x_ref, g_ref, shift_ref, scale_ref, w_ref, q_ref, h_ref):
    @pl.when(pl.program_id(1) == 0)
    def _():
        _ada_norm_rows(h_ref, x_ref, g_ref, shift_ref, scale_ref)

    acc = jnp.dot(h_ref[...], w_ref[...], preferred_element_type=F32)
    q_ref[...] = (acc * ATTN_SCALE).astype(q_ref.dtype)


def _in_proj1_q(x, norm_g, mods, row_of_tile, w, tm, tn):
    t = x.shape[0]
    return pl.pallas_call(
        _in1q_kernel,
        grid=(t // tm, D_MODEL // tn),
        in_specs=[
            pl.BlockSpec((tm, D_MODEL), lambda i, j: (i, 0)),
            pl.BlockSpec((1, D_MODEL), lambda i, j: (0, 0)),
            _mod_spec(0, row_of_tile),
            _mod_spec(1, row_of_tile),
            pl.BlockSpec((D_MODEL, tn), lambda i, j: (0, j)),
        ],
        out_specs=[pl.BlockSpec((tm, tn), lambda i, j: (i, j)),
                   pl.BlockSpec((tm, D_MODEL), lambda i, j: (i, 0))],
        out_shape=[jax.ShapeDtypeStruct((t, D_MODEL), BF16), jax.ShapeDtypeStruct((t, D_MODEL), BF16)],
        compiler_params=_cparams(2),
        name="in_proj1_q",
    )(x, norm_g, mods, mods, w)


def _in1kv_kernel(h_ref, w_ref, flat_ref, heads_ref):
    acc = jnp.dot(h_ref[...], w_ref[...], preferred_element_type=F32)
    flat_ref[...] = acc.astype(flat_ref.dtype)
    heads_ref[...] = acc.reshape(heads_ref.shape)


def _in_proj1_kv(h, w, part, tm, tn):
    t = h.shape[0]
    nb = D_MODEL // tn
    hb = tn // HEAD_DIM
    return pl.pallas_call(
        _in1kv_kernel,
        grid=(t // tm, nb),
        in_specs=[
            pl.BlockSpec((tm, D_MODEL), lambda i, j: (i, 0)),
            pl.BlockSpec((D_MODEL, tn), lambda i, j: (0, part * nb + j)),
        ],
        out_specs=[pl.BlockSpec((tm, tn), lambda i, j: (i, j)),
                   pl.BlockSpec((tm, hb, HEAD_DIM), lambda i, j: (i, j, 0))],
        out_shape=[jax.ShapeDtypeStruct((t, D_MODEL), BF16),
                   jax.ShapeDtypeStruct((t, NA_HEADS, HEAD_DIM), F32)],
        compiler_params=_cparams(2),
        name="in_proj1_kv",
    )(h, w)


def _gated_conv(zc, prev_row, next_row, cw):
    c = CONV_WIDTH
    s = zc.shape[0]
    gb = zc[:, 0:c].astype(F32)
    u = zc[:, c:2 * c].astype(F32) * zc[:, 2 * c:3 * c].astype(F32)
    row = lax.broadcasted_iota(jnp.int32, u.shape, 0)
    prev = jnp.where(row == 0, prev_row, pltpu.roll(u, 1, 0))
    nxt = jnp.where(row == s - 1, next_row, pltpu.roll(u, s - 1, 0))
    return gb * (prev * cw[0:1] + u * cw[1:2] + nxt * cw[2:3])


def _attn0_ctx_kernel(zc_ref, q_ref, kv_ref, cw_ref, *rest):
    n_cast = len(rest) // 2
    o_ref = rest[n_cast]
    _side_cast(rest[:n_cast], rest[n_cast + 1:])
    zero = jnp.zeros((1, CONV_WIDTH), F32)
    a = _gated_conv(zc_ref[...], zero, zero, cw_ref[...])
    o_ref[:, 0:CONV_WIDTH] = a.astype(o_ref.dtype)
    for g in range(GQA_KV_HEADS):
        k = kv_ref[:, g * HEAD_DIM:(g + 1) * HEAD_DIM].astype(BF16)
        v = kv_ref[:, KV_W + g * HEAD_DIM:KV_W + (g + 1) * HEAD_DIM].astype(BF16)
        for hh in range(GQA_GROUP):
            h = g * GQA_GROUP + hh
            q = q_ref[:, h * HEAD_DIM:(h + 1) * HEAD_DIM]
            o = _softmax_pv([_nt_dot(q, k)], [v])
            o_ref[:, CONV_WIDTH + h * HEAD_DIM:CONV_WIDTH + (h + 1) * HEAD_DIM] = o.astype(o_ref.dtype)


def _attn0_ctx(zc, q, kv, cw, cast_list):
    t = zc.shape[0]
    c_in, c_args, c_out, c_shape = _side_cast_specs(cast_list, t // SEQ, lambda b: b)
    return pl.pallas_call(
        _attn0_ctx_kernel,
        grid=(t // SEQ,),
        in_specs=[
            pl.BlockSpec((SEQ, 3 * CONV_WIDTH), lambda b: (b, 0)),
            pl.BlockSpec((SEQ, GQA_HEADS * HEAD_DIM), lambda b: (b, 0)),
            pl.BlockSpec((SEQ, 2 * KV_W), lambda b: (b, 0)),
            pl.BlockSpec((3, CONV_WIDTH), lambda b: (0, 0)),
        ] + c_in,
        out_specs=[pl.BlockSpec((SEQ, D_MODEL), lambda b: (b, 0))] + c_out,
        out_shape=[jax.ShapeDtypeStruct((t, D_MODEL), BF16)] + c_shape,
        compiler_params=_cparams(1),
        name="mixer0_ctx",
    )(zc, q, kv, cw, *c_args)


_HALO = 16


def _side_cast_specs(cast_list, n_steps, step_of):
    in_specs, args, out_specs, out_shape = [], [], [], []
    for w_f32, layer in cast_list:
        _, rows, cols = w_f32.shape
        r = rows // n_steps
        in_specs.append(pl.BlockSpec((None, r, cols),
                                     functools.partial(lambda *g, l: (l, step_of(*g), 0), l=layer)))
        args.append(w_f32)
        out_specs.append(pl.BlockSpec((r, cols), lambda *g: (step_of(*g), 0)))
        out_shape.append(jax.ShapeDtypeStruct((rows, cols), BF16))
    return in_specs, args, out_specs, out_shape


def _side_cast(src_refs, dst_refs):
    for src_ref, dst_ref in zip(src_refs, dst_refs):
        dst_ref[...] = src_ref[...].astype(BF16)


def _attn0_lat_kernel(zc_ref, zp_ref, zn_ref, q_ref, kv_ref, ck_ref, cv_ref, cw_ref, *rest, nqb):
    n_cast = len(rest) // 2
    o_ref = rest[n_cast]
    _side_cast(rest[:n_cast], rest[n_cast + 1:])
    qb = pl.program_id(1)
    c = CONV_WIDTH
    up = zp_ref[:, c:2 * c].astype(F32) * zp_ref[:, 2 * c:3 * c].astype(F32)
    un = zn_ref[:, c:2 * c].astype(F32) * zn_ref[:, 2 * c:3 * c].astype(F32)
    up = jnp.where(qb == 0, 0.0, up[_HALO - 1:_HALO, :])
    un = jnp.where(qb == nqb - 1, 0.0, un[0:1, :])
    a = _gated_conv(zc_ref[...], up, un, cw_ref[...])
    o_ref[:, 0:c] = a.astype(o_ref.dtype)
    for g in range(GQA_KV_HEADS):
        ks = slice(g * HEAD_DIM, (g + 1) * HEAD_DIM)
        vs = slice(KV_W + g * HEAD_DIM, KV_W + (g + 1) * HEAD_DIM)
        k = kv_ref[:, ks]
        v = kv_ref[:, vs]
        ck = ck_ref[:, ks].astype(BF16)
        cv = cv_ref[:, ks].astype(BF16)
        for hh in range(GQA_GROUP):
            h = g * GQA_GROUP + hh
            q = q_ref[:, h * HEAD_DIM:(h + 1) * HEAD_DIM]
            o = _softmax_pv([_nt_dot(q, ck), _nt_dot(q, k)], [cv, v])
            o_ref[:, c + h * HEAD_DIM:c + (h + 1) * HEAD_DIM] = o.astype(o_ref.dtype)


def _attn0_lat(zc, q, kv, ck, cv, cw, tq, cast_list):
    nb, t, _ = zc.shape
    nqb = t // tq
    hb = tq // _HALO
    c_in, c_args, c_out, c_shape = _side_cast_specs(cast_list, nb * nqb, lambda b, i: b * nqb + i)
    return pl.pallas_call(
        functools.partial(_attn0_lat_kernel, nqb=nqb),
        grid=(nb, nqb),
        in_specs=[
            pl.BlockSpec((None, tq, 3 * CONV_WIDTH), lambda b, i: (b, i, 0)),
            pl.BlockSpec((None, _HALO, 3 * CONV_WIDTH), lambda b, i: (b, jnp.maximum(i * hb - 1, 0), 0)),
            pl.BlockSpec((None, _HALO, 3 * CONV_WIDTH),
                         lambda b, i: (b, jnp.minimum((i + 1) * hb, t // _HALO - 1), 0)),
            pl.BlockSpec((None, tq, GQA_HEADS * HEAD_DIM), lambda b, i: (b, i, 0)),
            pl.BlockSpec((None, t, 2 * KV_W), lambda b, i: (b, 0, 0)),
            pl.BlockSpec((None, PAST_LEN, KV_W), lambda b, i: (b, 0, 0)),
            pl.BlockSpec((None, PAST_LEN, KV_W), lambda b, i: (b, 0, 0)),
            pl.BlockSpec((3, CONV_WIDTH), lambda b, i: (0, 0)),
        ] + c_in,
        out_specs=[pl.BlockSpec((None, tq, D_MODEL), lambda b, i: (b, i, 0))] + c_out,
        out_shape=[jax.ShapeDtypeStruct((nb, t, D_MODEL), BF16)] + c_shape,
        compiler_params=_cparams(2),
        name="mixer0_lat",
    )(zc, zc, zc, q, kv, ck, cv, cw, *c_args)


def _attn1_ctx_kernel(q_ref, k_ref, v_ref, o_ref):
    for h in range(NA_HEADS):
        sl = slice(h * HEAD_DIM, (h + 1) * HEAD_DIM)
        o = _softmax_pv([_nt_dot(q_ref[:, sl], k_ref[:, sl])], [v_ref[:, sl]])
        o_ref[:, sl] = o.astype(o_ref.dtype)


def _attn1_ctx(q, k, v):
    t = q.shape[0]
    blk = pl.BlockSpec((SEQ, D_MODEL), lambda b: (b, 0))
    return pl.pallas_call(
        _attn1_ctx_kernel,
        grid=(t // SEQ,),
        in_specs=[blk, blk, blk],
        out_specs=blk,
        out_shape=jax.ShapeDtypeStruct((t, D_MODEL), BF16),
        compiler_params=_cparams(1),
        name="mixer1_ctx",
    )(q, k, v)


def _na_slab_row(blk):
    return min(max(NA_QROWS * blk - NA_WIN_ROWS // 2, 0), GRID_H - NA_SLAB_ROWS)


NA_NROW = 2 * NA_WIN_ROWS - 1
NA_MASKED = NA_NROW
NA_FIRST_IN = NA_WIN_ROWS - 1 - NA_WIN_ROWS // 2
NA_LAST_IN = NA_FIRST_IN + NA_WIN_ROWS - 1
NA_PIECE_PAIRS = ([(m, m + 1) for m in range(NA_NROW - 1)]
                  + [(NA_MASKED, NA_FIRST_IN), (NA_LAST_IN, NA_MASKED)])


def _na_piece(blk, dr, jp):
    r = NA_QROWS * blk + dr
    ks = min(max(r - NA_WIN_ROWS // 2, 0), GRID_H - NA_WIN_ROWS)
    pair = []
    for j in (2 * jp, 2 * jp + 1):
        krow = _na_slab_row(blk) + j
        pair.append(krow - r + NA_WIN_ROWS - 1 if ks <= krow < ks + NA_WIN_ROWS else NA_MASKED)
    pair = tuple(pair)
    if pair == (NA_MASKED, NA_MASKED):
        return None
    return NA_PIECE_PAIRS.index(pair)


NA_PAD = GRID_W - NA_WIN_COLS


def _na_bias_table(rel_bias):
    return jnp.pad(rel_bias, ((0, 0), (0, 1), (NA_PAD, 2 * GRID_W - (2 * NA_WIN_COLS - 1) - NA_PAD)),
                   constant_values=MASK_NEG)


def _na_fill_pieces(table_ref, pieces_ref):
    shape = (GRID_W, 2 * GRID_W)
    c = lax.broadcasted_iota(jnp.int32, shape, 0)
    lane = lax.broadcasted_iota(jnp.int32, shape, 1)
    kc = lane & (GRID_W - 1)
    cs = jnp.clip(c - NA_WIN_COLS // 2, 0, GRID_W - NA_WIN_COLS)
    in_window = (kc >= cs) & (kc < cs + NA_WIN_COLS)

    def expand(a, shift):
        row = jnp.broadcast_to(table_ref[a:a + 1, :], shape)
        return pltpu.roll(row, shift, 1, stride=1, stride_axis=0)

    left_shift = 2 * GRID_W - (GRID_W - 1)
    right_shift = 1
    for m, (a_left, a_right) in enumerate(NA_PIECE_PAIRS):
        both = jnp.where(lane < GRID_W, expand(a_left, left_shift), expand(a_right, right_shift))
        pieces_ref[m] = jnp.where(in_window, both * LOG2E, MASK_NEG)


def _na_bias_block(pieces_ref, blk):
    rows = []
    for dr in range(NA_QROWS):
        cols = []
        for jp in range(NA_SLAB_ROWS // 2):
            m = _na_piece(blk, dr, jp)
            cols.append(jnp.full((GRID_W, 2 * GRID_W), MASK_NEG, F32) if m is None else pieces_ref[m])
        rows.append(jnp.concatenate(cols, axis=1))
    return jnp.concatenate(rows, axis=0)


def _attn1_lat_kernel(q_ref, k_ref, v_ref, ck_ref, cv_ref, table_ref, o_ref, pieces_ref):
    _na_fill_pieces(table_ref, pieces_ref)
    ck = ck_ref[...].astype(BF16)
    cv = cv_ref[...].astype(BF16)
    for blk in range(NA_NBLK):
        rows = slice(blk * NA_QBLK, (blk + 1) * NA_QBLK)
        s0 = _na_slab_row(blk) * GRID_W
        slab = slice(s0, s0 + NA_SLAB)
        q = q_ref[rows, :]
        s_loc = _nt_dot(q, k_ref[slab, :]) + _na_bias_block(pieces_ref, blk)
        o = _softmax_pv([s_loc, _nt_dot(q, ck)], [v_ref[slab, :], cv])
        o_ref[rows, :] = o.astype(o_ref.dtype)


def _attn1_lat(qkv, ck, cv, table):
    _, nb, t, _ = qkv.shape
    part = lambda p: pl.BlockSpec((None, None, t, HEAD_DIM), lambda b, h: (p, b, 0, h))
    head = pl.BlockSpec((None, t, HEAD_DIM), lambda b, h: (b, 0, h))
    ctx = pl.BlockSpec((None, PAST_LEN, HEAD_DIM), lambda b, h: (b, 0, h))
    return pl.pallas_call(
        _attn1_lat_kernel,
        grid=(nb, NA_HEADS),
        in_specs=[part(0), part(1), part(2), ctx, ctx,
                  pl.BlockSpec((None, NA_NROW + 1, 2 * GRID_W), lambda b, h: (h, 0, 0))],
        out_specs=head,
        out_shape=jax.ShapeDtypeStruct((nb, t, D_MODEL), BF16),
        scratch_shapes=[pltpu.VMEM((len(NA_PIECE_PAIRS), GRID_W, 2 * GRID_W), F32)],
        compiler_params=_cparams(2),
        name="mixer1_lat",
    )(qkv, qkv, qkv, ck, cv, table)


def _out_proj_kernel(m_ref, w_ref, x_ref, gate_ref, o_ref):
    acc = jnp.dot(m_ref[...], w_ref[...], preferred_element_type=F32)
    o_ref[...] = x_ref[...] + gate_ref[...] * acc


def _out_proj(mix, w, x, mods, row_of_tile, tm, tn):
    t, k = mix.shape
    return pl.pallas_call(
        _out_proj_kernel,
        grid=(t // tm, D_MODEL // tn),
        in_specs=[
            pl.BlockSpec((tm, k), lambda i, j: (i, 0)),
            pl.BlockSpec((k, tn), lambda i, j: (0, j)),
            pl.BlockSpec((tm, tn), lambda i, j: (i, j)),
            pl.BlockSpec((None, None, 1, tn), lambda i, j: (row_of_tile(i), 2, 0, j)),
        ],
        out_specs=pl.BlockSpec((tm, tn), lambda i, j: (i, j)),
        out_shape=jax.ShapeDtypeStruct((t, D_MODEL), F32),
        compiler_params=_cparams(2),
        name="out_proj",
    )(mix, w, x, mods)


def _mlp_kernel(x_ref, xn_ref, g_ref, shift_ref, scale_ref, shiftn_ref, scalen_ref, gate_ref, w1_ref, w2_ref,
                fg_ref, *rest, nf, final):
    n_cast = (len(rest) - 3) // 2
    o_ref, h0_scr, h1_scr = rest[n_cast], rest[-2], rest[-1]
    _side_cast(rest[:n_cast], rest[n_cast + 1:-2])
    i = pl.program_id(0)
    f = pl.program_id(1)
    rows_per_step = x_ref.shape[0] // nf

    @pl.when((i == 0) & (f == 0))
    def _():
        _ada_norm_rows(h0_scr, x_ref, g_ref, shift_ref, scale_ref)

    @pl.when(f == 0)
    def _():
        o_ref[...] = jnp.zeros_like(o_ref)

    def step(h_cur, h_nxt):
        _ada_norm_slice(h_nxt, xn_ref, g_ref, shiftn_ref, scalen_ref, f * rows_per_step, rows_per_step)
        u = jnp.dot(h_cur[...], w1_ref[...], preferred_element_type=F32)
        u = jnp.square(jnp.maximum(u, 0.0)).astype(BF16)
        o_ref[...] += jnp.dot(u, w2_ref[...], preferred_element_type=F32)

    pl.when(i % 2 == 0)(functools.partial(step, h0_scr, h1_scr))
    pl.when(i % 2 == 1)(functools.partial(step, h1_scr, h0_scr))

    @pl.when(f == nf - 1)
    def _():
        y = x_ref[...] + gate_ref[...] * o_ref[...]
        if final:
            ms = jnp.mean(y * y, axis=-1, keepdims=True)
            y = y * lax.rsqrt(ms + NORM_EPS) * fg_ref[...]
        o_ref[...] = y


def _mlp(x, norm_g, mods, row_of_tile, w1, w2, final_g, final, tm, tf, cast_next=None):
    t = x.shape[0]
    nf = D_FF // tf
    nt = t // tm
    same = lambda i: i
    nxt = lambda i: jnp.minimum(i + 1, nt - 1)

    def mod(which, tile):
        return pl.BlockSpec((None, None, 1, D_MODEL), lambda i, f: (row_of_tile(tile(i)), which, 0, 0))

    in_specs = [
        pl.BlockSpec((tm, D_MODEL), lambda i, f: (i, 0)),
        pl.BlockSpec((tm, D_MODEL), lambda i, f: (nxt(i), 0)),
        pl.BlockSpec((1, D_MODEL), lambda i, f: (0, 0)),
        mod(3, same), mod(4, same), mod(3, nxt), mod(4, nxt), mod(5, same),
        pl.BlockSpec((D_MODEL, tf), lambda i, f: (0, f)),
        pl.BlockSpec((tf, D_MODEL), lambda i, f: (f, 0)),
        pl.BlockSpec((1, D_MODEL), lambda i, f: (0, 0)),
    ]
    args = [x, x, norm_g, mods, mods, mods, mods, mods, w1, w2, final_g]
    out_specs = [pl.BlockSpec((tm, D_MODEL), lambda i, f: (i, 0))]
    out_shape = [jax.ShapeDtypeStruct((t, D_MODEL), F32)]
    c_in, c_args, c_out, c_shape = _side_cast_specs(cast_next or (), nt * nf, lambda i, f: i * nf + f)
    in_specs += c_in
    args += c_args
    out_specs += c_out
    out_shape += c_shape
    res = pl.pallas_call(
        functools.partial(_mlp_kernel, nf=nf, final=final),
        grid=(nt, nf),
        in_specs=in_specs,
        out_specs=out_specs,
        out_shape=out_shape,
        scratch_shapes=[pltpu.VMEM((tm, D_MODEL), BF16), pltpu.VMEM((tm, D_MODEL), BF16)],
        compiler_params=_cparams(2, ordered=True),
        name="mlp",
    )(*args)
    return res if cast_next else res[0]


def _rope_tables():
    t = np.arange(DEC_SEQ)
    half = HEAD_DIM // 4
    inv = ROPE_THETA ** (-np.arange(half, dtype=np.float32) / half)
    ang_r = (t // GRID_W).astype(np.float32)[:, None] * inv
    ang_c = (t % GRID_W).astype(np.float32)[:, None] * inv
    zero = np.zeros_like(ang_r)
    cos = np.concatenate([np.cos(ang_r)] * 2 + [np.cos(ang_c)] * 2, axis=-1)
    sin_lo = np.concatenate([-np.sin(ang_r), zero, -np.sin(ang_c), zero], axis=-1)
    sin_hi = np.concatenate([zero, np.sin(ang_r), zero, np.sin(ang_c)], axis=-1)
    return tuple(jnp.asarray(a, F32) for a in (cos, sin_lo, sin_hi))


def kernel(x_prompt, x_sample, cache_attn_k, cache_attn_v, cache_na_k, cache_na_v, c, c_ctx, mod_w, mod_b,
           norm1_g, norm2_g, ab_w_in, ab_conv_w, ab_q_norm, ab_k_norm, ab_w_out, na_w_qkv, na_rel_bias,
           na_w_out, mlp_w1, mlp_w2, final_norm_g):
    n_ctx = BATCH * SEQ
    xp = x_prompt.reshape(n_ctx, D_MODEL)
    xs = x_sample.reshape(DEC_BATCH * DEC_SEQ, D_MODEL)

    cond = jnp.concatenate([c_ctx[None, :], c, jnp.zeros((8 - 1 - DEC_BATCH, D_MODEL), F32)], axis=0)
    mods = _modulation(cond, mod_w, mod_b).reshape(2, 8, 6, 1, D_MODEL)

    tm_in, tm_mlp, tf = 512, 512, 1024
    tm_out, tn_out = 512, D_MODEL

    def ctx_row(i):
        return 0

    def lat_row(tm):
        return lambda i: 1 + i // (DEC_SEQ // tm)

    rope_tabs = _rope_tables()
    fg = final_norm_g.reshape(1, D_MODEL)

    m0 = mods[0]
    g1 = norm1_g[0].reshape(1, D_MODEL)
    w_in = ab_w_in[0].astype(BF16)
    qn = ab_q_norm[0].reshape(1, HEAD_DIM)
    kn = ab_k_norm[0].reshape(1, HEAD_DIM)
    cw = ab_conv_w[0]

    zc_p, q_p, kv_p = _in_proj0(xp, g1, m0, ctx_row, w_in, qn, kn, None, F32, tm_in)
    zc_s, q_s, kv_s = _in_proj0(xs, g1, m0, lat_row(tm_in), w_in, qn, kn, rope_tabs, BF16, tm_in)
    new_attn_k = kv_p[:, :KV_W].reshape(BATCH, 1, SEQ, GQA_KV_HEADS, HEAD_DIM)
    new_attn_v = kv_p[:, KV_W:].reshape(BATCH, 1, SEQ, GQA_KV_HEADS, HEAD_DIM)

    mix_p, w_out = _attn0_ctx(zc_p, q_p, kv_p, cw, [(ab_w_out, 0)])
    lat3 = lambda a: a.reshape(DEC_BATCH, DEC_SEQ, a.shape[-1])
    mix_s, w1, w2 = _attn0_lat(lat3(zc_s), lat3(q_s), lat3(kv_s),
                               cache_attn_k[:, 0].reshape(DEC_BATCH, PAST_LEN, KV_W),
                               cache_attn_v[:, 0].reshape(DEC_BATCH, PAST_LEN, KV_W), cw, 256,
                               [(mlp_w1, 0), (mlp_w2, 0)])
    mix_s = mix_s.reshape(DEC_BATCH * DEC_SEQ, D_MODEL)

    xp = _out_proj(mix_p, w_out, xp, m0, ctx_row, tm_out, tn_out)
    xs = _out_proj(mix_s, w_out, xs, m0, lat_row(tm_out), tm_out, tn_out)

    g2 = norm2_g[0].reshape(1, D_MODEL)
    xp, w1_next, w2_next = _mlp(xp, g2, m0, ctx_row, w1, w2, fg, False, tm_mlp, tf,
                                cast_next=[(mlp_w1, 1), (mlp_w2, 1)])
    xs, w_qkv, w_out = _mlp(xs, g2, m0, lat_row(tm_mlp), w1, w2, fg, False, tm_mlp, tf,
                            cast_next=[(na_w_qkv, 0), (na_w_out, 0)])

    m1 = mods[1]
    g1 = norm1_g[1].reshape(1, D_MODEL)

    q_p, h_p = _in_proj1_q(xp, g1, m1, ctx_row, w_qkv, 512, D_MODEL)
    k_p, k_heads = _in_proj1_kv(h_p, w_qkv, 1, 512, D_MODEL)
    v_p, v_heads = _in_proj1_kv(h_p, w_qkv, 2, 512, D_MODEL)
    qkv_s = _in_proj1(xs, g1, m1, lat_row(1024), w_qkv, BF16, 1024, 1024)
    new_na_k = k_heads.reshape(BATCH, 1, SEQ, NA_HEADS, HEAD_DIM)
    new_na_v = v_heads.reshape(BATCH, 1, SEQ, NA_HEADS, HEAD_DIM)

    mix_p = _attn1_ctx(q_p, k_p, v_p)
    mix_s = _attn1_lat(qkv_s.reshape(3, DEC_BATCH, DEC_SEQ, D_MODEL),
                       cache_na_k[:, 0].reshape(DEC_BATCH, PAST_LEN, D_MODEL),
                       cache_na_v[:, 0].reshape(DEC_BATCH, PAST_LEN, D_MODEL),
                       _na_bias_table(na_rel_bias[0]))
    mix_s = mix_s.reshape(DEC_BATCH * DEC_SEQ, D_MODEL)

    xp = _out_proj(mix_p, w_out, xp, m1, ctx_row, tm_out, tn_out)
    xs = _out_proj(mix_s, w_out, xs, m1, lat_row(tm_out), tm_out, tn_out)

    g2 = norm2_g[1].reshape(1, D_MODEL)
    yp = _mlp(xp, g2, m1, ctx_row, w1_next, w2_next, fg, True, tm_mlp, tf)
    ys = _mlp(xs, g2, m1, lat_row(tm_mlp), w1_next, w2_next, fg, True, tm_mlp, tf)

    return (yp.reshape(BATCH, SEQ, D_MODEL), ys.reshape(DEC_BATCH, DEC_SEQ, D_MODEL),
            new_attn_k, new_attn_v, new_na_k, new_na_v)
```

```python
import functools
import math

import numpy as np
import jax
import jax.numpy as jnp
from jax import lax
from jax.experimental import pallas as pl
from jax.experimental.pallas import tpu as pltpu

D_MODEL = 2048
BATCH = 32
SEQ = 256
DEC_BATCH = 2
DEC_SEQ = 2048
PAST_LEN = 512
GRID_W = 64
GRID_H = DEC_SEQ // GRID_W
HEAD_DIM = 128
CONV_WIDTH = D_MODEL // 2
GQA_HEADS = 8
GQA_KV_HEADS = 2
GQA_GROUP = GQA_HEADS // GQA_KV_HEADS
NA_HEADS = 16
NA_WIN_ROWS = 8
NA_WIN_COLS = 16
D_FF = 4 * D_MODEL
ROPE_THETA = 10000.0
NORM_EPS = 1e-6
AB_IN = 3 * CONV_WIDTH + (GQA_HEADS + 2 * GQA_KV_HEADS) * HEAD_DIM
KV_W = GQA_KV_HEADS * HEAD_DIM
LOG2E = math.log2(math.e)
ATTN_SCALE = HEAD_DIM ** -0.5 * LOG2E

NA_QROWS = 4
NA_SLAB_ROWS = NA_QROWS + NA_WIN_ROWS
NA_QBLK = NA_QROWS * GRID_W
NA_SLAB = NA_SLAB_ROWS * GRID_W
NA_NBLK = GRID_H // NA_QROWS
MASK_NEG = -1e30

V7X_VMEM_LIMIT = 56 * 1024 * 1024

MOD_TN = 1024
IN0_TM = 512
IN1_TM, IN1_TN = 1024, 1024
IN1_CTX_TM = 512
LAT0_TQ = 256
OUT_TM = 512
MLP_TM, MLP_TF = 512, 1024

BF16 = jnp.bfloat16
F32 = jnp.float32


def _cparams(n_axes):
    return pltpu.CompilerParams(
        dimension_semantics=("parallel",) + ("arbitrary",) * (n_axes - 1),
        vmem_limit_bytes=V7X_VMEM_LIMIT)


def _nt_dot(a, b):
    return lax.dot_general(a, b, (((1,), (1,)), ((), ())), preferred_element_type=F32)


NORM_ROWS = 16


def _ada_norm_rows(h_ref, x_ref, g_ref, shift_ref, scale_ref):
    gain = g_ref[...] * (1.0 + scale_ref[...])
    shift = shift_ref[...]

    def body(c, carry):
        rows = pl.ds(pl.multiple_of(c * NORM_ROWS, NORM_ROWS), NORM_ROWS)
        x = x_ref[rows, :]
        ms = jnp.mean(x * x, axis=-1, keepdims=True)
        h_ref[rows, :] = (x * lax.rsqrt(ms + NORM_EPS) * gain + shift).astype(h_ref.dtype)
        return carry

    lax.fori_loop(0, x_ref.shape[0] // NORM_ROWS, body, 0, unroll=8)


def _head_norm(x, g):
    ms = jnp.mean(x * x, axis=-1, keepdims=True)
    return x * lax.rsqrt(ms + NORM_EPS) * g


def _rope(x, cos, sin_lo, sin_hi):
    return x * cos + pltpu.roll(x, HEAD_DIM - 32, 1) * sin_lo + pltpu.roll(x, 32, 1) * sin_hi


def _softmax_pv(scores, values):
    m = scores[0].max(axis=-1, keepdims=True)
    for s in scores[1:]:
        m = jnp.maximum(m, s.max(axis=-1, keepdims=True))
    l = None
    o = None
    for s, v in zip(scores, values):
        p = jnp.exp2(s - m)
        ls = jnp.sum(p, axis=-1, keepdims=True)
        os_ = jnp.dot(p.astype(BF16), v, preferred_element_type=F32)
        l = ls if l is None else l + ls
        o = os_ if o is None else o + os_
    return o / l


_HALO = 16


def _side_cast_specs(cast_list, n_steps, step_of):
    in_specs, args, out_specs, out_shape = [], [], [], []
    for w_f32, layer in cast_list:
        _, rows, cols = w_f32.shape
        r = rows // n_steps
        in_specs.append(pl.BlockSpec((None, r, cols),
                                     functools.partial(lambda *g, l: (l, step_of(*g), 0), l=layer)))
        args.append(w_f32)
        out_specs.append(pl.BlockSpec((r, cols), lambda *g: (step_of(*g), 0)))
        out_shape.append(jax.ShapeDtypeStruct((rows, cols), BF16))
    return in_specs, args, out_specs, out_shape


def _side_cast(src_refs, dst_refs):
    for src_ref, dst_ref in zip(src_refs, dst_refs):
        dst_ref[...] = src_ref[...].astype(BF16)


def _mod_kernel(c_ref, w_ref, b_ref, o_ref):
    c = c_ref[...]
    s = (c * jax.nn.sigmoid(c)).astype(BF16)
    o_ref[...] = jnp.dot(s, w_ref[...].astype(BF16), preferred_element_type=F32) + b_ref[...]


def _modulation(cond, mod_w, mod_b):
    depth, d, n = mod_w.shape
    tn = MOD_TN
    rows = cond.shape[0]
    return pl.pallas_call(
        _mod_kernel,
        grid=(depth, n // tn),
        in_specs=[
            pl.BlockSpec((rows, d), lambda l, j: (0, 0)),
            pl.BlockSpec((None, d, tn), lambda l, j: (l, 0, j)),
            pl.BlockSpec((None, 1, tn), lambda l, j: (l, 0, j)),
        ],
        out_specs=pl.BlockSpec((None, rows, tn), lambda l, j: (l, 0, j)),
        out_shape=jax.ShapeDtypeStruct((depth, rows, n), F32),
        compiler_params=_cparams(2),
        name="modulation",
    )(cond, mod_w, mod_b.reshape(depth, 1, n))


def _mod_spec(which, row_of_tile):
    return pl.BlockSpec((None, None, 1, D_MODEL), lambda i, j: (row_of_tile(i), which, 0, 0))


Q_W = GQA_HEADS * HEAD_DIM
IN0_TN = Q_W + 2 * KV_W


def _in0_kernel(x_ref, g_ref, shift_ref, scale_ref, w_ref, qn_ref, kn_ref, *rest, rope):
    if rope:
        cos_ref, slo_ref, shi_ref, zc_ref, q_ref, kv_ref, h_scr = rest
    else:
        zc_ref, q_ref, kv_ref, h_scr = rest
    j = pl.program_id(1)
    n_zc = 3 * CONV_WIDTH // IN0_TN

    @pl.when(j == 0)
    def _():
        _ada_norm_rows(h_scr, x_ref, g_ref, shift_ref, scale_ref)

    acc = jnp.dot(h_scr[...], w_ref[...], preferred_element_type=F32)
    zc_ref[...] = acc.astype(zc_ref.dtype)

    def normed(blk, gain):
        y = _head_norm(blk, gain)
        if rope:
            y = _rope(y, cos_ref[...], slo_ref[...], shi_ref[...])
        return y

    @pl.when(j == n_zc)
    def _():
        for hh in range(GQA_HEADS):
            sl = slice(hh * HEAD_DIM, (hh + 1) * HEAD_DIM)
            q_ref[:, sl] = (normed(acc[:, sl], qn_ref[...]) * ATTN_SCALE).astype(q_ref.dtype)
        for hh in range(GQA_KV_HEADS):
            src = slice(Q_W + hh * HEAD_DIM, Q_W + (hh + 1) * HEAD_DIM)
            kv_ref[:, hh * HEAD_DIM:(hh + 1) * HEAD_DIM] = normed(acc[:, src], kn_ref[...]).astype(kv_ref.dtype)
        kv_ref[:, KV_W:] = acc[:, Q_W + KV_W:].astype(kv_ref.dtype)


def _in_proj0(x, norm_g, mods, row_of_tile, w, qn, kn, rope_tabs, kv_dtype, tm):
    t = x.shape[0]
    tn = IN0_TN
    rope = rope_tabs is not None
    vec = pl.BlockSpec((1, HEAD_DIM), lambda i, j: (0, 0))
    in_specs = [
        pl.BlockSpec((tm, D_MODEL), lambda i, j: (i, 0)),
        pl.BlockSpec((1, D_MODEL), lambda i, j: (0, 0)),
        _mod_spec(0, row_of_tile),
        _mod_spec(1, row_of_tile),
        pl.BlockSpec((D_MODEL, tn), lambda i, j: (0, j)),
        vec, vec,
    ]
    args = [x, norm_g, mods, mods, w, qn, kn]
    if rope:
        nt = DEC_SEQ // tm
        tab = pl.BlockSpec((tm, HEAD_DIM), lambda i, j: (i % nt, 0))
        in_specs += [tab, tab, tab]
        args += list(rope_tabs)
    return pl.pallas_call(
        functools.partial(_in0_kernel, rope=rope),
        grid=(t // tm, AB_IN // tn),
        in_specs=in_specs,
        out_specs=[
            pl.BlockSpec((tm, tn), lambda i, j: (i, j)),
            pl.BlockSpec((tm, Q_W), lambda i, j: (i, 0)),
            pl.BlockSpec((tm, 2 * KV_W), lambda i, j: (i, 0)),
        ],
        out_shape=[
            jax.ShapeDtypeStruct((t, AB_IN), BF16),
            jax.ShapeDtypeStruct((t, GQA_HEADS * HEAD_DIM), BF16),
            jax.ShapeDtypeStruct((t, 2 * KV_W), kv_dtype),
        ],
        scratch_shapes=[pltpu.VMEM((tm, D_MODEL), BF16)],
        compiler_params=_cparams(2),
        name="in_proj0",
    )(*args)


def _in1_kernel(x_ref, g_ref, shift_ref, scale_ref, w_ref, o_ref, h_scr, *, nb):
    j = pl.program_id(1)

    @pl.when(j == 0)
    def _():
        _ada_norm_rows(h_scr, x_ref, g_ref, shift_ref, scale_ref)

    acc = jnp.dot(h_scr[...], w_ref[...], preferred_element_type=F32)
    o_ref[...] = (acc * jnp.where(j < nb, ATTN_SCALE, 1.0)).astype(o_ref.dtype)


def _in_proj1(x, norm_g, mods, row_of_tile, w, out_dtype, tm, tn):
    t = x.shape[0]
    nb = D_MODEL // tn
    return pl.pallas_call(
        functools.partial(_in1_kernel, nb=nb),
        grid=(t // tm, 3 * nb),
        in_specs=[
            pl.BlockSpec((tm, D_MODEL), lambda i, j: (i, 0)),
            pl.BlockSpec((1, D_MODEL), lambda i, j: (0, 0)),
            _mod_spec(0, row_of_tile),
            _mod_spec(1, row_of_tile),
            pl.BlockSpec((D_MODEL, tn), lambda i, j: (0, j)),
        ],
        out_specs=pl.BlockSpec((None, tm, tn), lambda i, j: (j // nb, i, j % nb)),
        out_shape=jax.ShapeDtypeStruct((3, t, D_MODEL), out_dtype),
        scratch_shapes=[pltpu.VMEM((tm, D_MODEL), BF16)],
        compiler_params=_cparams(2),
        name="in_proj1",
    )(x, norm_g, mods, mods, w)


def _in1q_kernel(x_ref, g_ref, shift_ref, scale_ref, w_ref, q_ref, h_ref):
    @pl.when(pl.program_id(1) == 0)
    def _():
        _ada_norm_rows(h_ref, x_ref, g_ref, shift_ref, scale_ref)

    acc = jnp.dot(h_ref[...], w_ref[...], preferred_element_type=F32)
    q_ref[...] = (acc * ATTN_SCALE).astype(q_ref.dtype)


def _in_proj1_q(x, norm_g, mods, row_of_tile, w, tm, tn):
    t = x.shape[0]
    return pl.pallas_call(
        _in1q_kernel,
        grid=(t // tm, D_MODEL // tn),
        in_specs=[
            pl.BlockSpec((tm, D_MODEL), lambda i, j: (i, 0)),
            pl.BlockSpec((1, D_MODEL), lambda i, j: (0, 0)),
            _mod_spec(0, row_of_tile),
            _mod_spec(1, row_of_tile),
            pl.BlockSpec((D_MODEL, tn), lambda i, j: (0, j)),
        ],
        out_specs=[pl.BlockSpec((tm, tn), lambda i, j: (i, j)),
                   pl.BlockSpec((tm, D_MODEL), lambda i, j: (i, 0))],
        out_shape=[jax.ShapeDtypeStruct((t, D_MODEL), BF16), jax.ShapeDtypeStruct((t, D_MODEL), BF16)],
        compiler_params=_cparams(2),
        name="in_proj1_q",
    )(x, norm_g, mods, mods, w)


def _in1kv_kernel(h_ref, w_ref, flat_ref, heads_ref):
    acc = jnp.dot(h_ref[...], w_ref[...], preferred_element_type=F32)
    flat_ref[...] = acc.astype(flat_ref.dtype)
    heads_ref[...] = acc.reshape(heads_ref.shape)


def _in_proj1_kv(h, w, part, tm, tn):
    t = h.shape[0]
    nb = D_MODEL // tn
    hb = tn // HEAD_DIM
    return pl.pallas_call(
        _in1kv_kernel,
        grid=(t // tm, nb),
        in_specs=[
            pl.BlockSpec((tm, D_MODEL), lambda i, j: (i, 0)),
            pl.BlockSpec((D_MODEL, tn), lambda i, j: (0, part * nb + j)),
        ],
        out_specs=[pl.BlockSpec((tm, tn), lambda i, j: (i, j)),
                   pl.BlockSpec((tm, hb, HEAD_DIM), lambda i, j: (i, j, 0))],
        out_shape=[jax.ShapeDtypeStruct((t, D_MODEL), BF16),
                   jax.ShapeDtypeStruct((t, NA_HEADS, HEAD_DIM), F32)],
        compiler_params=_cparams(2),
        name="in_proj1_kv",
    )(h, w)


def _gated_conv(zc, prev_row, next_row, cw):
    c = CONV_WIDTH
    s = zc.shape[0]
    gb = zc[:, 0:c].astype(F32)
    u = zc[:, c:2 * c].astype(F32) * zc[:, 2 * c:3 * c].astype(F32)
    row = lax.broadcasted_iota(jnp.int32, u.shape, 0)
    prev = jnp.where(row == 0, prev_row, pltpu.roll(u, 1, 0))
    nxt = jnp.where(row == s - 1, next_row, pltpu.roll(u, s - 1, 0))
    return gb * (prev * cw[0:1] + u * cw[1:2] + nxt * cw[2:3])


def _attn0_ctx_kernel(zc_ref, q_ref, kv_ref, cw_ref, *rest):
    n_cast = len(rest) // 2
    o_ref = rest[n_cast]
    _side_cast(rest[:n_cast], rest[n_cast + 1:])
    zero = jnp.zeros((1, CONV_WIDTH), F32)
    a = _gated_conv(zc_ref[...], zero, zero, cw_ref[...])
    o_ref[:, 0:CONV_WIDTH] = a.astype(o_ref.dtype)
    for g in range(GQA_KV_HEADS):
        k = kv_ref[:, g * HEAD_DIM:(g + 1) * HEAD_DIM].astype(BF16)
        v = kv_ref[:, KV_W + g * HEAD_DIM:KV_W + (g + 1) * HEAD_DIM].astype(BF16)
        for hh in range(GQA_GROUP):
            h = g * GQA_GROUP + hh
            q = q_ref[:, h * HEAD_DIM:(h + 1) * HEAD_DIM]
            o = _softmax_pv([_nt_dot(q, k)], [v])
            o_ref[:, CONV_WIDTH + h * HEAD_DIM:CONV_WIDTH + (h + 1) * HEAD_DIM] = o.astype(o_ref.dtype)


def _attn0_ctx(zc, q, kv, cw, cast_list):
    t = zc.shape[0]
    c_in, c_args, c_out, c_shape = _side_cast_specs(cast_list, t // SEQ, lambda b: b)
    return pl.pallas_call(
        _attn0_ctx_kernel,
        grid=(t // SEQ,),
        in_specs=[
            pl.BlockSpec((SEQ, 3 * CONV_WIDTH), lambda b: (b, 0)),
            pl.BlockSpec((SEQ, GQA_HEADS * HEAD_DIM), lambda b: (b, 0)),
            pl.BlockSpec((SEQ, 2 * KV_W), lambda b: (b, 0)),
            pl.BlockSpec((3, CONV_WIDTH), lambda b: (0, 0)),
        ] + c_in,
        out_specs=[pl.BlockSpec((SEQ, D_MODEL), lambda b: (b, 0))] + c_out,
        out_shape=[jax.ShapeDtypeStruct((t, D_MODEL), BF16)] + c_shape,
        compiler_params=_cparams(1),
        name="mixer0_ctx",
    )(zc, q, kv, cw, *c_args)


def _attn0_lat_kernel(zc_ref, zp_ref, zn_ref, q_ref, kv_ref, ck_ref, cv_ref, cw_ref, *rest, nqb):
    n_cast = len(rest) // 2
    o_ref = rest[n_cast]
    _side_cast(rest[:n_cast], rest[n_cast + 1:])
    qb = pl.program_id(1)
    c = CONV_WIDTH
    up = zp_ref[:, c:2 * c].astype(F32) * zp_ref[:, 2 * c:3 * c].astype(F32)
    un = zn_ref[:, c:2 * c].astype(F32) * zn_ref[:, 2 * c:3 * c].astype(F32)
    up = jnp.where(qb == 0, 0.0, up[_HALO - 1:_HALO, :])
    un = jnp.where(qb == nqb - 1, 0.0, un[0:1, :])
    a = _gated_conv(zc_ref[...], up, un, cw_ref[...])
    o_ref[:, 0:c] = a.astype(o_ref.dtype)
    for g in range(GQA_KV_HEADS):
        ks = slice(g * HEAD_DIM, (g + 1) * HEAD_DIM)
        vs = slice(KV_W + g * HEAD_DIM, KV_W + (g + 1) * HEAD_DIM)
        k = kv_ref[:, ks]
        v = kv_ref[:, vs]
        ck = ck_ref[:, ks].astype(BF16)
        cv = cv_ref[:, ks].astype(BF16)
        for hh in range(GQA_GROUP):
            h = g * GQA_GROUP + hh
            q = q_ref[:, h * HEAD_DIM:(h + 1) * HEAD_DIM]
            o = _softmax_pv([_nt_dot(q, ck), _nt_dot(q, k)], [cv, v])
            o_ref[:, c + h * HEAD_DIM:c + (h + 1) * HEAD_DIM] = o.astype(o_ref.dtype)


def _attn0_lat(zc, q, kv, ck, cv, cw, tq, cast_list):
    nb, t, _ = zc.shape
    nqb = t // tq
    hb = tq // _HALO
    c_in, c_args, c_out, c_shape = _side_cast_specs(cast_list, nb * nqb, lambda b, i: b * nqb + i)
    return pl.pallas_call(
        functools.partial(_attn0_lat_kernel, nqb=nqb),
        grid=(nb, nqb),
        in_specs=[
            pl.BlockSpec((None, tq, 3 * CONV_WIDTH), lambda b, i: (b, i, 0)),
            pl.BlockSpec((None, _HALO, 3 * CONV_WIDTH), lambda b, i: (b, jnp.maximum(i * hb - 1, 0), 0)),
            pl.BlockSpec((None, _HALO, 3 * CONV_WIDTH),
                         lambda b, i: (b, jnp.minimum((i + 1) * hb, t // _HALO - 1), 0)),
            pl.BlockSpec((None, tq, GQA_HEADS * HEAD_DIM), lambda b, i: (b, i, 0)),
            pl.BlockSpec((None, t, 2 * KV_W), lambda b, i: (b, 0, 0)),
            pl.BlockSpec((None, PAST_LEN, KV_W), lambda b, i: (b, 0, 0)),
            pl.BlockSpec((None, PAST_LEN, KV_W), lambda b, i: (b, 0, 0)),
            pl.BlockSpec((3, CONV_WIDTH), lambda b, i: (0, 0)),
        ] + c_in,
        out_specs=[pl.BlockSpec((None, tq, D_MODEL), lambda b, i: (b, i, 0))] + c_out,
        out_shape=[jax.ShapeDtypeStruct((nb, t, D_MODEL), BF16)] + c_shape,
        compiler_params=_cparams(2),
        name="mixer0_lat",
    )(zc, zc, zc, q, kv, ck, cv, cw, *c_args)


def _attn1_ctx_kernel(q_ref, k_ref, v_ref, o_ref):
    for h in range(NA_HEADS):
        sl = slice(h * HEAD_DIM, (h + 1) * HEAD_DIM)
        o = _softmax_pv([_nt_dot(q_ref[:, sl], k_ref[:, sl])], [v_ref[:, sl]])
        o_ref[:, sl] = o.astype(o_ref.dtype)


def _attn1_ctx(q, k, v):
    t = q.shape[0]
    blk = pl.BlockSpec((SEQ, D_MODEL), lambda b: (b, 0))
    return pl.pallas_call(
        _attn1_ctx_kernel,
        grid=(t // SEQ,),
        in_specs=[blk, blk, blk],
        out_specs=blk,
        out_shape=jax.ShapeDtypeStruct((t, D_MODEL), BF16),
        compiler_params=_cparams(1),
        name="mixer1_ctx",
    )(q, k, v)


def _na_slab_row(blk):
    return min(max(NA_QROWS * blk - NA_WIN_ROWS // 2, 0), GRID_H - NA_SLAB_ROWS)


NA_NROW = 2 * NA_WIN_ROWS - 1
NA_MASKED = NA_NROW
NA_FIRST_IN = NA_WIN_ROWS - 1 - NA_WIN_ROWS // 2
NA_LAST_IN = NA_FIRST_IN + NA_WIN_ROWS - 1
NA_PIECE_PAIRS = ([(m, m + 1) for m in range(NA_NROW - 1)]
                  + [(NA_MASKED, NA_FIRST_IN), (NA_LAST_IN, NA_MASKED)])


def _na_piece(blk, dr, jp):
    r = NA_QROWS * blk + dr
    ks = min(max(r - NA_WIN_ROWS // 2, 0), GRID_H - NA_WIN_ROWS)
    pair = []
    for j in (2 * jp, 2 * jp + 1):
        krow = _na_slab_row(blk) + j
        pair.append(krow - r + NA_WIN_ROWS - 1 if ks <= krow < ks + NA_WIN_ROWS else NA_MASKED)
    pair = tuple(pair)
    if pair == (NA_MASKED, NA_MASKED):
        return None
    return NA_PIECE_PAIRS.index(pair)


NA_PAD = GRID_W - NA_WIN_COLS


def _na_bias_table(rel_bias):
    return jnp.pad(rel_bias, ((0, 0), (0, 1), (NA_PAD, 2 * GRID_W - (2 * NA_WIN_COLS - 1) - NA_PAD)),
                   constant_values=MASK_NEG)


def _na_fill_pieces(table_ref, pieces_ref):
    shape = (GRID_W, 2 * GRID_W)
    c = lax.broadcasted_iota(jnp.int32, shape, 0)
    lane = lax.broadcasted_iota(jnp.int32, shape, 1)
    kc = lane & (GRID_W - 1)
    cs = jnp.clip(c - NA_WIN_COLS // 2, 0, GRID_W - NA_WIN_COLS)
    in_window = (kc >= cs) & (kc < cs + NA_WIN_COLS)

    def expand(a, shift):
        row = jnp.broadcast_to(table_ref[a:a + 1, :], shape)
        return pltpu.roll(row, shift, 1, stride=1, stride_axis=0)

    left_shift = 2 * GRID_W - (GRID_W - 1)
    right_shift = 1
    for m, (a_left, a_right) in enumerate(NA_PIECE_PAIRS):
        both = jnp.where(lane < GRID_W, expand(a_left, left_shift), expand(a_right, right_shift))
        pieces_ref[m] = jnp.where(in_window, both * LOG2E, MASK_NEG)


def _na_bias_block(pieces_ref, blk):
    rows = []
    for dr in range(NA_QROWS):
        cols = []
        for jp in range(NA_SLAB_ROWS // 2):
            m = _na_piece(blk, dr, jp)
            cols.append(jnp.full((GRID_W, 2 * GRID_W), MASK_NEG, F32) if m is None else pieces_ref[m])
        rows.append(jnp.concatenate(cols, axis=1))
    return jnp.concatenate(rows, axis=0)


def _attn1_lat_kernel(q_ref, k_ref, v_ref, ck_ref, cv_ref, table_ref, o_ref, pieces_ref):
    _na_fill_pieces(table_ref, pieces_ref)
    ck = ck_ref[...].astype(BF16)
    cv = cv_ref[...].astype(BF16)
    for blk in range(NA_NBLK):
        rows = slice(blk * NA_QBLK, (blk + 1) * NA_QBLK)
        s0 = _na_slab_row(blk) * GRID_W
        slab = slice(s0, s0 + NA_SLAB)
        q = q_ref[rows, :]
        s_loc = _nt_dot(q, k_ref[slab, :]) + _na_bias_block(pieces_ref, blk)
        o = _softmax_pv([s_loc, _nt_dot(q, ck)], [v_ref[slab, :], cv])
        o_ref[rows, :] = o.astype(o_ref.dtype)


def _attn1_lat(qkv, ck, cv, table):
    _, nb, t, _ = qkv.shape
    part = lambda p: pl.BlockSpec((None, None, t, HEAD_DIM), lambda b, h: (p, b, 0, h))
    head = pl.BlockSpec((None, t, HEAD_DIM), lambda b, h: (b, 0, h))
    ctx = pl.BlockSpec((None, PAST_LEN, HEAD_DIM), lambda b, h: (b, 0, h))
    return pl.pallas_call(
        _attn1_lat_kernel,
        grid=(nb, NA_HEADS),
        in_specs=[part(0), part(1), part(2), ctx, ctx,
                  pl.BlockSpec((None, NA_NROW + 1, 2 * GRID_W), lambda b, h: (h, 0, 0))],
        out_specs=head,
        out_shape=jax.ShapeDtypeStruct((nb, t, D_MODEL), BF16),
        scratch_shapes=[pltpu.VMEM((len(NA_PIECE_PAIRS), GRID_W, 2 * GRID_W), F32)],
        compiler_params=_cparams(2),
        name="mixer1_lat",
    )(qkv, qkv, qkv, ck, cv, table)


def _out_proj_kernel(m_ref, w_ref, x_ref, gate_ref, o_ref):
    acc = jnp.dot(m_ref[...], w_ref[...], preferred_element_type=F32)
    o_ref[...] = x_ref[...] + gate_ref[...] * acc


def _out_proj(mix, w, x, mods, row_of_tile, tm, tn):
    t, k = mix.shape
    return pl.pallas_call(
        _out_proj_kernel,
        grid=(t // tm, D_MODEL // tn),
        in_specs=[
            pl.BlockSpec((tm, k), lambda i, j: (i, 0)),
            pl.BlockSpec((k, tn), lambda i, j: (0, j)),
            pl.BlockSpec((tm, tn), lambda i, j: (i, j)),
            pl.BlockSpec((None, None, 1, tn), lambda i, j: (row_of_tile(i), 2, 0, j)),
        ],
        out_specs=pl.BlockSpec((tm, tn), lambda i, j: (i, j)),
        out_shape=jax.ShapeDtypeStruct((t, D_MODEL), F32),
        compiler_params=_cparams(2),
        name="out_proj",
    )(mix, w, x, mods)


def _mlp_kernel(x_ref, g_ref, shift_ref, scale_ref, gate_ref, w1_ref, w2_ref, fg_ref, *rest, nf, final):
    n_cast = (len(rest) - 2) // 2
    o_ref, h_scr = rest[n_cast], rest[-1]
    _side_cast(rest[:n_cast], rest[n_cast + 1:-1])
    f = pl.program_id(1)

    @pl.when(f == 0)
    def _():
        _ada_norm_rows(h_scr, x_ref, g_ref, shift_ref, scale_ref)
        o_ref[...] = jnp.zeros_like(o_ref)

    u = jnp.dot(h_scr[...], w1_ref[...], preferred_element_type=F32)
    u = jnp.square(jnp.maximum(u, 0.0)).astype(BF16)
    o_ref[...] += jnp.dot(u, w2_ref[...], preferred_element_type=F32)

    @pl.when(f == nf - 1)
    def _():
        y = x_ref[...] + gate_ref[...] * o_ref[...]
        if final:
            ms = jnp.mean(y * y, axis=-1, keepdims=True)
            y = y * lax.rsqrt(ms + NORM_EPS) * fg_ref[...]
        o_ref[...] = y


def _mlp(x, norm_g, mods, row_of_tile, w1, w2, final_g, final, tm, tf, cast_next=None):
    t = x.shape[0]
    nf = D_FF // tf
    nt = t // tm

    def mod(which):
        return pl.BlockSpec((None, None, 1, D_MODEL), lambda i, f: (row_of_tile(i), which, 0, 0))

    in_specs = [
        pl.BlockSpec((tm, D_MODEL), lambda i, f: (i, 0)),
        pl.BlockSpec((1, D_MODEL), lambda i, f: (0, 0)),
        mod(3), mod(4), mod(5),
        pl.BlockSpec((D_MODEL, tf), lambda i, f: (0, f)),
        pl.BlockSpec((tf, D_MODEL), lambda i, f: (f, 0)),
        pl.BlockSpec((1, D_MODEL), lambda i, f: (0, 0)),
    ]
    args = [x, norm_g, mods, mods, mods, w1, w2, final_g]
    out_specs = [pl.BlockSpec((tm, D_MODEL), lambda i, f: (i, 0))]
    out_shape = [jax.ShapeDtypeStruct((t, D_MODEL), F32)]
    c_in, c_args, c_out, c_shape = _side_cast_specs(cast_next or (), nt * nf, lambda i, f: i * nf + f)
    in_specs += c_in
    args += c_args
    out_specs += c_out
    out_shape += c_shape
    res = pl.pallas_call(
        functools.partial(_mlp_kernel, nf=nf, final=final),
        grid=(nt, nf),
        in_specs=in_specs,
        out_specs=out_specs,
        out_shape=out_shape,
        scratch_shapes=[pltpu.VMEM((tm, D_MODEL), BF16)],
        compiler_params=_cparams(2),
        name="mlp",
    )(*args)
    return res if cast_next else res[0]


def _rope_tables():
    t = np.arange(DEC_SEQ)
    half = HEAD_DIM // 4
    inv = ROPE_THETA ** (-np.arange(half, dtype=np.float32) / half)
    ang_r = (t // GRID_W).astype(np.float32)[:, None] * inv
    ang_c = (t % GRID_W).astype(np.float32)[:, None] * inv
    zero = np.zeros_like(ang_r)
    cos = np.concatenate([np.cos(ang_r)] * 2 + [np.cos(ang_c)] * 2, axis=-1)
    sin_lo = np.concatenate([-np.sin(ang_r), zero, -np.sin(ang_c), zero], axis=-1)
    sin_hi = np.concatenate([zero, np.sin(ang_r), zero, np.sin(ang_c)], axis=-1)
    return tuple(jnp.asarray(a, F32) for a in (cos, sin_lo, sin_hi))


def kernel(x_prompt, x_sample, cache_attn_k, cache_attn_v, cache_na_k, cache_na_v, c, c_ctx, mod_w, mod_b,
           norm1_g, norm2_g, ab_w_in, ab_conv_w, ab_q_norm, ab_k_norm, ab_w_out, na_w_qkv, na_rel_bias,
           na_w_out, mlp_w1, mlp_w2, final_norm_g):
    n_ctx = BATCH * SEQ
    xp = x_prompt.reshape(n_ctx, D_MODEL)
    xs = x_sample.reshape(DEC_BATCH * DEC_SEQ, D_MODEL)

    cond = jnp.concatenate([c_ctx[None, :], c, jnp.zeros((8 - 1 - DEC_BATCH, D_MODEL), F32)], axis=0)
    mods = _modulation(cond, mod_w, mod_b).reshape(2, 8, 6, 1, D_MODEL)

    def ctx_row(i):
        return 0

    def lat_row(tm):
        return lambda i: 1 + i // (DEC_SEQ // tm)

    rope_tabs = _rope_tables()
    fg = final_norm_g.reshape(1, D_MODEL)

    m0 = mods[0]
    g1 = norm1_g[0].reshape(1, D_MODEL)
    w_in = ab_w_in[0].astype(BF16)
    qn = ab_q_norm[0].reshape(1, HEAD_DIM)
    kn = ab_k_norm[0].reshape(1, HEAD_DIM)
    cw = ab_conv_w[0]

    zc_p, q_p, kv_p = _in_proj0(xp, g1, m0, ctx_row, w_in, qn, kn, None, F32, IN0_TM)
    zc_s, q_s, kv_s = _in_proj0(xs, g1, m0, lat_row(IN0_TM), w_in, qn, kn, rope_tabs, BF16, IN0_TM)
    new_attn_k = kv_p[:, :KV_W].reshape(BATCH, 1, SEQ, GQA_KV_HEADS, HEAD_DIM)
    new_attn_v = kv_p[:, KV_W:].reshape(BATCH, 1, SEQ, GQA_KV_HEADS, HEAD_DIM)

    mix_p, w_out = _attn0_ctx(zc_p, q_p, kv_p, cw, [(ab_w_out, 0)])
    lat3 = lambda a: a.reshape(DEC_BATCH, DEC_SEQ, a.shape[-1])
    mix_s, w1, w2 = _attn0_lat(lat3(zc_s), lat3(q_s), lat3(kv_s),
                               cache_attn_k[:, 0].reshape(DEC_BATCH, PAST_LEN, KV_W),
                               cache_attn_v[:, 0].reshape(DEC_BATCH, PAST_LEN, KV_W), cw, LAT0_TQ,
                               [(mlp_w1, 0), (mlp_w2, 0)])
    mix_s = mix_s.reshape(DEC_BATCH * DEC_SEQ, D_MODEL)

    xp = _out_proj(mix_p, w_out, xp, m0, ctx_row, OUT_TM, D_MODEL)
    xs = _out_proj(mix_s, w_out, xs, m0, lat_row(OUT_TM), OUT_TM, D_MODEL)

    g2 = norm2_g[0].reshape(1, D_MODEL)
    xp, w1_next, w2_next = _mlp(xp, g2, m0, ctx_row, w1, w2, fg, False, MLP_TM, MLP_TF,
                                cast_next=[(mlp_w1, 1), (mlp_w2, 1)])
    xs, w_qkv, w_out = _mlp(xs, g2, m0, lat_row(MLP_TM), w1, w2, fg, False, MLP_TM, MLP_TF,
                            cast_next=[(na_w_qkv, 0), (na_w_out, 0)])

    m1 = mods[1]
    g1 = norm1_g[1].reshape(1, D_MODEL)

    q_p, h_p = _in_proj1_q(xp, g1, m1, ctx_row, w_qkv, IN1_CTX_TM, D_MODEL)
    k_p, k_heads = _in_proj1_kv(h_p, w_qkv, 1, IN1_CTX_TM, D_MODEL)
    v_p, v_heads = _in_proj1_kv(h_p, w_qkv, 2, IN1_CTX_TM, D_MODEL)
    qkv_s = _in_proj1(xs, g1, m1, lat_row(IN1_TM), w_qkv, BF16, IN1_TM, IN1_TN)
    new_na_k = k_heads.reshape(BATCH, 1, SEQ, NA_HEADS, HEAD_DIM)
    new_na_v = v_heads.reshape(BATCH, 1, SEQ, NA_HEADS, HEAD_DIM)

    mix_p = _attn1_ctx(q_p, k_p, v_p)
    mix_s = _attn1_lat(qkv_s.reshape(3, DEC_BATCH, DEC_SEQ, D_MODEL),
                       cache_na_k[:, 0].reshape(DEC_BATCH, PAST_LEN, D_MODEL),
                       cache_na_v[:, 0].reshape(DEC_BATCH, PAST_LEN, D_MODEL),
                       _na_bias_table(na_rel_bias[0]))
    mix_s = mix_s.reshape(DEC_BATCH * DEC_SEQ, D_MODEL)

    xp = _out_proj(mix_p, w_out, xp, m1, ctx_row, OUT_TM, D_MODEL)
    xs = _out_proj(mix_s, w_out, xs, m1, lat_row(OUT_TM), OUT_TM, D_MODEL)

    g2 = norm2_g[1].reshape(1, D_MODEL)
    yp = _mlp(xp, g2, m1, ctx_row, w1_next, w2_next, fg, True, MLP_TM, MLP_TF)
    ys = _mlp(xs, g2, m1, lat_row(MLP_TM), w1_next, w2_next, fg, True, MLP_TM, MLP_TF)

    return (yp.reshape(BATCH, SEQ, D_MODEL), ys.reshape(DEC_BATCH, DEC_SEQ, D_MODEL),
            new_attn_k, new_attn_v, new_na_k, new_na_v)
```

```python
import functools
import math

import numpy as np
import jax
import jax.numpy as jnp
from jax import lax
from jax.experimental import pallas as pl
from jax.experimental.pallas import tpu as pltpu

D_MODEL = 2048
BATCH = 32
SEQ = 256
DEC_BATCH = 2
DEC_SEQ = 2048
PAST_LEN = 512
GRID_W = 64
GRID_H = DEC_SEQ // GRID_W
HEAD_DIM = 128
CONV_WIDTH = D_MODEL // 2
GQA_HEADS = 8
GQA_KV_HEADS = 2
GQA_GROUP = GQA_HEADS // GQA_KV_HEADS
NA_HEADS = 16
NA_WIN_ROWS = 8
NA_WIN_COLS = 16
D_FF = 4 * D_MODEL
ROPE_THETA = 10000.0
NORM_EPS = 1e-6
AB_IN = 3 * CONV_WIDTH + (GQA_HEADS + 2 * GQA_KV_HEADS) * HEAD_DIM
KV_W = GQA_KV_HEADS * HEAD_DIM
LOG2E = math.log2(math.e)
ATTN_SCALE = HEAD_DIM ** -0.5 * LOG2E

NA_QROWS = 4
NA_SLAB_ROWS = NA_QROWS + NA_WIN_ROWS
NA_QBLK = NA_QROWS * GRID_W
NA_SLAB = NA_SLAB_ROWS * GRID_W
NA_NBLK = GRID_H // NA_QROWS
MASK_NEG = -1e30

V7X_VMEM_LIMIT = 56 * 1024 * 1024

MOD_TN = 1024
IN0_TM = 512
IN1_TM, IN1_TN = 1024, 1024
IN1_CTX_TM = 512
LAT0_TQ = 256
OUT_TM = 512
MLP_TM, MLP_TF = 512, 1024

BF16 = jnp.bfloat16
F32 = jnp.float32


def _cparams(n_axes):
    return pltpu.CompilerParams(
        dimension_semantics=("parallel",) + ("arbitrary",) * (n_axes - 1),
        vmem_limit_bytes=V7X_VMEM_LIMIT)


def _nt_dot(a, b):
    return lax.dot_general(a, b, (((1,), (1,)), ((), ())), preferred_element_type=F32)


NORM_ROWS = 16


def _ada_norm_rows(h_ref, x_ref, g_ref, shift_ref, scale_ref):
    gain = g_ref[...] * (1.0 + scale_ref[...])
    shift = shift_ref[...]

    def body(c, carry):
        rows = pl.ds(pl.multiple_of(c * NORM_ROWS, NORM_ROWS), NORM_ROWS)
        x = x_ref[rows, :]
        ms = jnp.mean(x * x, axis=-1, keepdims=True)
        h_ref[rows, :] = (x * lax.rsqrt(ms + NORM_EPS) * gain + shift).astype(h_ref.dtype)
        return carry

    lax.fori_loop(0, x_ref.shape[0] // NORM_ROWS, body, 0, unroll=8)


def _head_norm(x, g):
    ms = jnp.mean(x * x, axis=-1, keepdims=True)
    return x * lax.rsqrt(ms + NORM_EPS) * g


def _rope(x, cos, sin_lo, sin_hi):
    return x * cos + pltpu.roll(x, HEAD_DIM - 32, 1) * sin_lo + pltpu.roll(x, 32, 1) * sin_hi


def _softmax_pv(scores, values):
    m = scores[0].max(axis=-1, keepdims=True)
    for s in scores[1:]:
        m = jnp.maximum(m, s.max(axis=-1, keepdims=True))
    l = None
    o = None
    for s, v in zip(scores, values):
        p = jnp.exp2(s - m)
        ls = jnp.sum(p, axis=-1, keepdims=True)
        os_ = jnp.dot(p.astype(BF16), v, preferred_element_type=F32)
        l = ls if l is None else l + ls
        o = os_ if o is None else o + os_
    return o / l


_HALO = 16


def _side_cast_specs(cast_list, n_steps, step_of):
    in_specs, args, out_specs, out_shape = [], [], [], []
    for w_f32, layer in cast_list:
        _, rows, cols = w_f32.shape
        r = rows // n_steps
        in_specs.append(pl.BlockSpec((None, r, cols),
                                     functools.partial(lambda *g, l: (l, step_of(*g), 0), l=layer)))
        args.append(w_f32)
        out_specs.append(pl.BlockSpec((r, cols), lambda *g: (step_of(*g), 0)))
        out_shape.append(jax.ShapeDtypeStruct((rows, cols), BF16))
    return in_specs, args, out_specs, out_shape


def _side_cast(src_refs, dst_refs):
    for src_ref, dst_ref in zip(src_refs, dst_refs):
        dst_ref[...] = src_ref[...].astype(BF16)


def _mod_kernel(c_ref, w_ref, b_ref, o_ref):
    c = c_ref[...]
    s = (c * jax.nn.sigmoid(c)).astype(BF16)
    o_ref[...] = jnp.dot(s, w_ref[...].astype(BF16), preferred_element_type=F32) + b_ref[...]


def _modulation(cond, mod_w, mod_b):
    depth, d, n = mod_w.shape
    tn = MOD_TN
    rows = cond.shape[0]
    return pl.pallas_call(
        _mod_kernel,
        grid=(depth, n // tn),
        in_specs=[
            pl.BlockSpec((rows, d), lambda l, j: (0, 0)),
            pl.BlockSpec((None, d, tn), lambda l, j: (l, 0, j)),
            pl.BlockSpec((None, 1, tn), lambda l, j: (l, 0, j)),
        ],
        out_specs=pl.BlockSpec((None, rows, tn), lambda l, j: (l, 0, j)),
        out_shape=jax.ShapeDtypeStruct((depth, rows, n), F32),
        compiler_params=_cparams(2),
        name="modulation",
    )(cond, mod_w, mod_b.reshape(depth, 1, n))


def _mod_spec(which, row_of_tile):
    return pl.BlockSpec((None, None, 1, D_MODEL), lambda i, j: (row_of_tile(i), which, 0, 0))


Q_W = GQA_HEADS * HEAD_DIM
IN0_TN = Q_W + 2 * KV_W


def _in0_kernel(x_ref, g_ref, shift_ref, scale_ref, w_ref, qn_ref, kn_ref, *rest, rope):
    if rope:
        cos_ref, slo_ref, shi_ref, zc_ref, q_ref, kv_ref, h_scr = rest
    else:
        zc_ref, q_ref, kv_ref, h_scr = rest
    j = pl.program_id(1)
    n_zc = 3 * CONV_WIDTH // IN0_TN

    @pl.when(j == 0)
    def _():
        _ada_norm_rows(h_scr, x_ref, g_ref, shift_ref, scale_ref)

    acc = jnp.dot(h_scr[...], w_ref[...], preferred_element_type=F32)
    zc_ref[...] = acc.astype(zc_ref.dtype)

    def normed(blk, gain):
        y = _head_norm(blk, gain)
        if rope:
            y = _rope(y, cos_ref[...], slo_ref[...], shi_ref[...])
        return y

    @pl.when(j == n_zc)
    def _():
        for hh in range(GQA_HEADS):
            sl = slice(hh * HEAD_DIM, (hh + 1) * HEAD_DIM)
            q_ref[:, sl] = (normed(acc[:, sl], qn_ref[...]) * ATTN_SCALE).astype(q_ref.dtype)
        for hh in range(GQA_KV_HEADS):
            src = slice(Q_W + hh * HEAD_DIM, Q_W + (hh + 1) * HEAD_DIM)
            kv_ref[:, hh * HEAD_DIM:(hh + 1) * HEAD_DIM] = normed(acc[:, src], kn_ref[...]).astype(kv_ref.dtype)
        kv_ref[:, KV_W:] = acc[:, Q_W + KV_W:].astype(kv_ref.dtype)


def _in_proj0(x, norm_g, mods, row_of_tile, w, qn, kn, rope_tabs, kv_dtype, tm):
    t = x.shape[0]
    tn = IN0_TN
    rope = rope_tabs is not None
    vec = pl.BlockSpec((1, HEAD_DIM), lambda i, j: (0, 0))
    in_specs = [
        pl.BlockSpec((tm, D_MODEL), lambda i, j: (i, 0)),
        pl.BlockSpec((1, D_MODEL), lambda i, j: (0, 0)),
        _mod_spec(0, row_of_tile),
        _mod_spec(1, row_of_tile),
        pl.BlockSpec((D_MODEL, tn), lambda i, j: (0, j)),
        vec, vec,
    ]
    args = [x, norm_g, mods, mods, w, qn, kn]
    if rope:
        nt = DEC_SEQ // tm
        tab = pl.BlockSpec((tm, HEAD_DIM), lambda i, j: (i % nt, 0))
        in_specs += [tab, tab, tab]
        args += list(rope_tabs)
    return pl.pallas_call(
        functools.partial(_in0_kernel, rope=rope),
        grid=(t // tm, AB_IN // tn),
        in_specs=in_specs,
        out_specs=[
            pl.BlockSpec((tm, tn), lambda i, j: (i, j)),
            pl.BlockSpec((tm, Q_W), lambda i, j: (i, 0)),
            pl.BlockSpec((tm, 2 * KV_W), lambda i, j: (i, 0)),
        ],
        out_shape=[
            jax.ShapeDtypeStruct((t, AB_IN), BF16),
            jax.ShapeDtypeStruct((t, GQA_HEADS * HEAD_DIM), BF16),
            jax.ShapeDtypeStruct((t, 2 * KV_W), kv_dtype),
        ],
        scratch_shapes=[pltpu.VMEM((tm, D_MODEL), BF16)],
        compiler_params=_cparams(2),
        name="in_proj0",
    )(*args)


def _in1_kernel(x_ref, g_ref, shift_ref, scale_ref, w_ref, o_ref, h_scr, *, nb):
    j = pl.program_id(1)

    @pl.when(j == 0)
    def _():
        _ada_norm_rows(h_scr, x_ref, g_ref, shift_ref, scale_ref)

    acc = jnp.dot(h_scr[...], w_ref[...], preferred_element_type=F32)
    o_ref[...] = (acc * jnp.where(j < nb, ATTN_SCALE, 1.0)).astype(o_ref.dtype)


def _in_proj1(x, norm_g, mods, row_of_tile, w, out_dtype, tm, tn):
    t = x.shape[0]
    nb = D_MODEL // tn
    return pl.pallas_call(
        functools.partial(_in1_kernel, nb=nb),
        grid=(t // tm, 3 * nb),
        in_specs=[
            pl.BlockSpec((tm, D_MODEL), lambda i, j: (i, 0)),
            pl.BlockSpec((1, D_MODEL), lambda i, j: (0, 0)),
            _mod_spec(0, row_of_tile),
            _mod_spec(1, row_of_tile),
            pl.BlockSpec((D_MODEL, tn), lambda i, j: (0, j)),
        ],
        out_specs=pl.BlockSpec((None, tm, tn), lambda i, j: (j // nb, i, j % nb)),
        out_shape=jax.ShapeDtypeStruct((3, t, D_MODEL), out_dtype),
        scratch_shapes=[pltpu.VMEM((tm, D_MODEL), BF16)],
        compiler_params=_cparams(2),
        name="in_proj1",
    )(x, norm_g, mods, mods, w)


def _in1q_kernel(x_ref, g_ref, shift_ref, scale_ref, w_ref, q_ref, h_ref):
    @pl.when(pl.program_id(1) == 0)
    def _():
        _ada_norm_rows(h_ref, x_ref, g_ref, shift_ref, scale_ref)

    acc = jnp.dot(h_ref[...], w_ref[...], preferred_element_type=F32)
    q_ref[...] = (acc * ATTN_SCALE).astype(q_ref.dtype)


def _in_proj1_q(x, norm_g, mods, row_of_tile, w, tm, tn):
    t = x.shape[0]
    return pl.pallas_call(
        _in1q_kernel,
        grid=(t // tm, D_MODEL // tn),
        in_specs=[
            pl.BlockSpec((tm, D_MODEL), lambda i, j: (i, 0)),
            pl.BlockSpec((1, D_MODEL), lambda i, j: (0, 0)),
            _mod_spec(0, row_of_tile),
            _mod_spec(1, row_of_tile),
            pl.BlockSpec((D_MODEL, tn), lambda i, j: (0, j)),
        ],
        out_specs=[pl.BlockSpec((tm, tn), lambda i, j: (i, j)),
                   pl.BlockSpec((tm, D_MODEL), lambda i, j: (i, 0))],
        out_shape=[jax.ShapeDtypeStruct((t, D_MODEL), BF16), jax.ShapeDtypeStruct((t, D_MODEL), BF16)],
        compiler_params=_cparams(2),
        name="in_proj1_q",
    )(x, norm_g, mods, mods, w)


def _in1kv_kernel(h_ref, w_ref, flat_ref, heads_ref):
    acc = jnp.dot(h_ref[...], w_ref[...], preferred_element_type=F32)
    flat_ref[...] = acc.astype(flat_ref.dtype)
    heads_ref[...] = acc.reshape(heads_ref.shape)


def _in_proj1_kv(h, w, part, tm, tn):
    t = h.shape[0]
    nb = D_MODEL // tn
    hb = tn // HEAD_DIM
    return pl.pallas_call(
        _in1kv_kernel,
        grid=(t // tm, nb),
        in_specs=[
            pl.BlockSpec((tm, D_MODEL), lambda i, j: (i, 0)),
            pl.BlockSpec((D_MODEL, tn), lambda i, j: (0, part * nb + j)),
        ],
        out_specs=[pl.BlockSpec((tm, tn), lambda i, j: (i, j)),
                   pl.BlockSpec((tm, hb, HEAD_DIM), lambda i, j: (i, j, 0))],
        out_shape=[jax.ShapeDtypeStruct((t, D_MODEL), BF16),
                   jax.ShapeDtypeStruct((t, NA_HEADS, HEAD_DIM), F32)],
        compiler_params=_cparams(2),
        name="in_proj1_kv",
    )(h, w)


def _gated_conv(zc, prev_row, next_row, cw):
    c = CONV_WIDTH
    s = zc.shape[0]
    gb = zc[:, 0:c].astype(F32)
    u = zc[:, c:2 * c].astype(F32) * zc[:, 2 * c:3 * c].astype(F32)
    row = lax.broadcasted_iota(jnp.int32, u.shape, 0)
    prev = jnp.where(row == 0, prev_row, pltpu.roll(u, 1, 0))
    nxt = jnp.where(row == s - 1, next_row, pltpu.roll(u, s - 1, 0))
    return gb * (prev * cw[0:1] + u * cw[1:2] + nxt * cw[2:3])


def _attn0_ctx_kernel(zc_ref, q_ref, kv_ref, cw_ref, *rest):
    n_cast = len(rest) // 2
    o_ref = rest[n_cast]
    _side_cast(rest[:n_cast], rest[n_cast + 1:])
    zero = jnp.zeros((1, CONV_WIDTH), F32)
    a = _gated_conv(zc_ref[...], zero, zero, cw_ref[...])
    o_ref[:, 0:CONV_WIDTH] = a.astype(o_ref.dtype)
    for g in range(GQA_KV_HEADS):
        k = kv_ref[:, g * HEAD_DIM:(g + 1) * HEAD_DIM].astype(BF16)
        v = kv_ref[:, KV_W + g * HEAD_DIM:KV_W + (g + 1) * HEAD_DIM].astype(BF16)
        for hh in range(GQA_GROUP):
            h = g * GQA_GROUP + hh
            q = q_ref[:, h * HEAD_DIM:(h + 1) * HEAD_DIM]
            o = _softmax_pv([_nt_dot(q, k)], [v])
            o_ref[:, CONV_WIDTH + h * HEAD_DIM:CONV_WIDTH + (h + 1) * HEAD_DIM] = o.astype(o_ref.dtype)


def _attn0_ctx(zc, q, kv, cw, cast_list):
    t = zc.shape[0]
    c_in, c_args, c_out, c_shape = _side_cast_specs(cast_list, t // SEQ, lambda b: b)
    return pl.pallas_call(
        _attn0_ctx_kernel,
        grid=(t // SEQ,),
        in_specs=[
            pl.BlockSpec((SEQ, 3 * CONV_WIDTH), lambda b: (b, 0)),
            pl.BlockSpec((SEQ, GQA_HEADS * HEAD_DIM), lambda b: (b, 0)),
            pl.BlockSpec((SEQ, 2 * KV_W), lambda b: (b, 0)),
            pl.BlockSpec((3, CONV_WIDTH), lambda b: (0, 0)),
        ] + c_in,
        out_specs=[pl.BlockSpec((SEQ, D_MODEL), lambda b: (b, 0))] + c_out,
        out_shape=[jax.ShapeDtypeStruct((t, D_MODEL), BF16)] + c_shape,
        compiler_params=_cparams(1),
        name="mixer0_ctx",
    )(zc, q, kv, cw, *c_args)


def _attn0_lat_kernel(zc_ref, zp_ref, zn_ref, q_ref, kv_ref, ck_ref, cv_ref, cw_ref, *rest, nqb):
    n_cast = len(rest) // 2
    o_ref = rest[n_cast]
    _side_cast(rest[:n_cast], rest[n_cast + 1:])
    qb = pl.program_id(1)
    c = CONV_WIDTH
    up = zp_ref[:, c:2 * c].astype(F32) * zp_ref[:, 2 * c:3 * c].astype(F32)
    un = zn_ref[:, c:2 * c].astype(F32) * zn_ref[:, 2 * c:3 * c].astype(F32)
    up = jnp.where(qb == 0, 0.0, up[_HALO - 1:_HALO, :])
    un = jnp.where(qb == nqb - 1, 0.0, un[0:1, :])
    a = _gated_conv(zc_ref[...], up, un, cw_ref[...])
    o_ref[:, 0:c] = a.astype(o_ref.dtype)
    for g in range(GQA_KV_HEADS):
        ks = slice(g * HEAD_DIM, (g + 1) * HEAD_DIM)
        vs = slice(KV_W + g * HEAD_DIM, KV_W + (g + 1) * HEAD_DIM)
        k = kv_ref[:, ks]
        v = kv_ref[:, vs]
        ck = ck_ref[:, ks].astype(BF16)
        cv = cv_ref[:, ks].astype(BF16)
        for hh in range(GQA_GROUP):
            h = g * GQA_GROUP + hh
            q = q_ref[:, h * HEAD_DIM:(h + 1) * HEAD_DIM]
            o = _softmax_pv([_nt_dot(q, ck), _nt_dot(q, k)], [cv, v])
            o_ref[:, c + h * HEAD_DIM:c + (h + 1) * HEAD_DIM] = o.astype(o_ref.dtype)


def _attn0_lat(zc, q, kv, ck, cv, cw, tq, cast_list):
    nb, t, _ = zc.shape
    nqb = t // tq
    hb = tq // _HALO
    c_in, c_args, c_out, c_shape = _side_cast_specs(cast_list, nb * nqb, lambda b, i: b * nqb + i)
    return pl.pallas_call(
        functools.partial(_attn0_lat_kernel, nqb=nqb),
        grid=(nb, nqb),
        in_specs=[
            pl.BlockSpec((None, tq, 3 * CONV_WIDTH), lambda b, i: (b, i, 0)),
            pl.BlockSpec((None, _HALO, 3 * CONV_WIDTH), lambda b, i: (b, jnp.maximum(i * hb - 1, 0), 0)),
            pl.BlockSpec((None, _HALO, 3 * CONV_WIDTH),
                         lambda b, i: (b, jnp.minimum((i + 1) * hb, t // _HALO - 1), 0)),
            pl.BlockSpec((None, tq, GQA_HEADS * HEAD_DIM), lambda b, i: (b, i, 0)),
            pl.BlockSpec((None, t, 2 * KV_W), lambda b, i: (b, 0, 0)),
            pl.BlockSpec((None, PAST_LEN, KV_W), lambda b, i: (b, 0, 0)),
            pl.BlockSpec((None, PAST_LEN, KV_W), lambda b, i: (b, 0, 0)),
            pl.BlockSpec((3, CONV_WIDTH), lambda b, i: (0, 0)),
        ] + c_in,
        out_specs=[pl.BlockSpec((None, tq, D_MODEL), lambda b, i: (b, i, 0))] + c_out,
        out_shape=[jax.ShapeDtypeStruct((nb, t, D_MODEL), BF16)] + c_shape,
        compiler_params=_cparams(2),
        name="mixer0_lat",
    )(zc, zc, zc, q, kv, ck, cv, cw, *c_args)


def _attn1_ctx_kernel(q_ref, k_ref, v_ref, o_ref):
    for h in range(NA_HEADS):
        sl = slice(h * HEAD_DIM, (h + 1) * HEAD_DIM)
        o = _softmax_pv([_nt_dot(q_ref[:, sl], k_ref[:, sl])], [v_ref[:, sl]])
        o_ref[:, sl] = o.astype(o_ref.dtype)


def _attn1_ctx(q, k, v):
    t = q.shape[0]
    blk = pl.BlockSpec((SEQ, D_MODEL), lambda b: (b, 0))
    return pl.pallas_call(
        _attn1_ctx_kernel,
        grid=(t // SEQ,),
        in_specs=[blk, blk, blk],
        out_specs=blk,
        out_shape=jax.ShapeDtypeStruct((t, D_MODEL), BF16),
        compiler_params=_cparams(1),
        name="mixer1_ctx",
    )(q, k, v)


def _na_slab_row(blk):
    return min(max(NA_QROWS * blk - NA_WIN_ROWS // 2, 0), GRID_H - NA_SLAB_ROWS)


NA_NROW = 2 * NA_WIN_ROWS - 1
NA_MASKED = NA_NROW
NA_FIRST_IN = NA_WIN_ROWS - 1 - NA_WIN_ROWS // 2
NA_LAST_IN = NA_FIRST_IN + NA_WIN_ROWS - 1
NA_PIECE_PAIRS = ([(m, m + 1) for m in range(NA_NROW - 1)]
                  + [(NA_MASKED, NA_FIRST_IN), (NA_LAST_IN, NA_MASKED)])


def _na_piece(blk, dr, jp):
    r = NA_QROWS * blk + dr
    ks = min(max(r - NA_WIN_ROWS // 2, 0), GRID_H - NA_WIN_ROWS)
    pair = []
    for j in (2 * jp, 2 * jp + 1):
        krow = _na_slab_row(blk) + j
        pair.append(krow - r + NA_WIN_ROWS - 1 if ks <= krow < ks + NA_WIN_ROWS else NA_MASKED)
    pair = tuple(pair)
    if pair == (NA_MASKED, NA_MASKED):
        return None
    return NA_PIECE_PAIRS.index(pair)


NA_PAD = GRID_W - NA_WIN_COLS


def _na_bias_table(rel_bias):
    return jnp.pad(rel_bias, ((0, 0), (0, 1), (NA_PAD, 2 * GRID_W - (2 * NA_WIN_COLS - 1) - NA_PAD)),
                   constant_values=MASK_NEG)


def _na_fill_pieces(table_ref, pieces_ref):
    shape = (GRID_W, 2 * GRID_W)
    c = lax.broadcasted_iota(jnp.int32, shape, 0)
    lane = lax.broadcasted_iota(jnp.int32, shape, 1)
    kc = lane & (GRID_W - 1)
    cs = jnp.clip(c - NA_WIN_COLS // 2, 0, GRID_W - NA_WIN_COLS)
    in_window = (kc >= cs) & (kc < cs + NA_WIN_COLS)

    def expand(a, shift):
        row = jnp.broadcast_to(table_ref[a:a + 1, :], shape)
        return pltpu.roll(row, shift, 1, stride=1, stride_axis=0)

    left_shift = 2 * GRID_W - (GRID_W - 1)
    right_shift = 1
    for m, (a_left, a_right) in enumerate(NA_PIECE_PAIRS):
        both = jnp.where(lane < GRID_W, expand(a_left, left_shift), expand(a_right, right_shift))
        pieces_ref[m] = jnp.where(in_window, both * LOG2E, MASK_NEG)


def _na_bias_block(pieces_ref, blk):
    rows = []
    for dr in range(NA_QROWS):
        cols = []
        for jp in range(NA_SLAB_ROWS // 2):
            m = _na_piece(blk, dr, jp)
            cols.append(jnp.full((GRID_W, 2 * GRID_W), MASK_NEG, F32) if m is None else pieces_ref[m])
        rows.append(jnp.concatenate(cols, axis=1))
    return jnp.concatenate(rows, axis=0)


def _attn1_lat_kernel(q_ref, k_ref, v_ref, ck_ref, cv_ref, table_ref, o_ref, pieces_ref):
    _na_fill_pieces(table_ref, pieces_ref)
    head_rows = pl.ds(pl.program_id(1), PAST_LEN, stride=NA_HEADS)
    ck = ck_ref[head_rows, :].astype(BF16)
    cv = cv_ref[head_rows, :].astype(BF16)
    for blk in range(NA_NBLK):
        rows = slice(blk * NA_QBLK, (blk + 1) * NA_QBLK)
        s0 = _na_slab_row(blk) * GRID_W
        slab = slice(s0, s0 + NA_SLAB)
        q = q_ref[rows, :]
        s_loc = _nt_dot(q, k_ref[slab, :]) + _na_bias_block(pieces_ref, blk)
        o = _softmax_pv([s_loc, _nt_dot(q, ck)], [v_ref[slab, :], cv])
        o_ref[rows, :] = o.astype(o_ref.dtype)


def _attn1_lat(qkv, ck, cv, table):
    _, nb, t, _ = qkv.shape
    part = lambda p: pl.BlockSpec((None, None, t, HEAD_DIM), lambda b, h: (p, b, 0, h))
    head = pl.BlockSpec((None, t, HEAD_DIM), lambda b, h: (b, 0, h))
    ctx = pl.BlockSpec((None, PAST_LEN * NA_HEADS, HEAD_DIM), lambda b, h: (b, 0, 0))
    return pl.pallas_call(
        _attn1_lat_kernel,
        grid=(nb, NA_HEADS),
        in_specs=[part(0), part(1), part(2), ctx, ctx,
                  pl.BlockSpec((None, NA_NROW + 1, 2 * GRID_W), lambda b, h: (h, 0, 0))],
        out_specs=head,
        out_shape=jax.ShapeDtypeStruct((nb, t, D_MODEL), BF16),
        scratch_shapes=[pltpu.VMEM((len(NA_PIECE_PAIRS), GRID_W, 2 * GRID_W), F32)],
        compiler_params=_cparams(2),
        name="mixer1_lat",
    )(qkv, qkv, qkv, ck, cv, table)


def _out_proj_kernel(m_ref, w_ref, x_ref, gate_ref, o_ref):
    acc = jnp.dot(m_ref[...], w_ref[...], preferred_element_type=F32)
    o_ref[...] = x_ref[...] + gate_ref[...] * acc


def _out_proj(mix, w, x, mods, row_of_tile, tm, tn):
    t, k = mix.shape
    return pl.pallas_call(
        _out_proj_kernel,
        grid=(t // tm, D_MODEL // tn),
        in_specs=[
            pl.BlockSpec((tm, k), lambda i, j: (i, 0)),
            pl.BlockSpec((k, tn), lambda i, j: (0, j)),
            pl.BlockSpec((tm, tn), lambda i, j: (i, j)),
            pl.BlockSpec((None, None, 1, tn), lambda i, j: (row_of_tile(i), 2, 0, j)),
        ],
        out_specs=pl.BlockSpec((tm, tn), lambda i, j: (i, j)),
        out_shape=jax.ShapeDtypeStruct((t, D_MODEL), F32),
        compiler_params=_cparams(2),
        name="out_proj",
    )(mix, w, x, mods)


def _mlp_kernel(x_ref, g_ref, shift_ref, scale_ref, gate_ref, w1_ref, w2_ref, fg_ref, *rest, nf, final):
    n_cast = (len(rest) - 2) // 2
    o_ref, h_scr = rest[n_cast], rest[-1]
    _side_cast(rest[:n_cast], rest[n_cast + 1:-1])
    f = pl.program_id(1)

    @pl.when(f == 0)
    def _():
        _ada_norm_rows(h_scr, x_ref, g_ref, shift_ref, scale_ref)
        o_ref[...] = jnp.zeros_like(o_ref)

    u = jnp.dot(h_scr[...], w1_ref[...], preferred_element_type=F32)
    u = jnp.square(jnp.maximum(u, 0.0)).astype(BF16)
    o_ref[...] += jnp.dot(u, w2_ref[...], preferred_element_type=F32)

    @pl.when(f == nf - 1)
    def _():
        y = x_ref[...] + gate_ref[...] * o_ref[...]
        if final:
            ms = jnp.mean(y * y, axis=-1, keepdims=True)
            y = y * lax.rsqrt(ms + NORM_EPS) * fg_ref[...]
        o_ref[...] = y


def _mlp(x, norm_g, mods, row_of_tile, w1, w2, final_g, final, tm, tf, cast_next=None):
    t = x.shape[0]
    nf = D_FF // tf
    nt = t // tm

    def mod(which):
        return pl.BlockSpec((None, None, 1, D_MODEL), lambda i, f: (row_of_tile(i), which, 0, 0))

    in_specs = [
        pl.BlockSpec((tm, D_MODEL), lambda i, f: (i, 0)),
        pl.BlockSpec((1, D_MODEL), lambda i, f: (0, 0)),
        mod(3), mod(4), mod(5),
        pl.BlockSpec((D_MODEL, tf), lambda i, f: (0, f)),
        pl.BlockSpec((tf, D_MODEL), lambda i, f: (f, 0)),
        pl.BlockSpec((1, D_MODEL), lambda i, f: (0, 0)),
    ]
    args = [x, norm_g, mods, mods, mods, w1, w2, final_g]
    out_specs = [pl.BlockSpec((tm, D_MODEL), lambda i, f: (i, 0))]
    out_shape = [jax.ShapeDtypeStruct((t, D_MODEL), F32)]
    c_in, c_args, c_out, c_shape = _side_cast_specs(cast_next or (), nt * nf, lambda i, f: i * nf + f)
    in_specs += c_in
    args += c_args
    out_specs += c_out
    out_shape += c_shape
    res = pl.pallas_call(
        functools.partial(_mlp_kernel, nf=nf, final=final),
        grid=(nt, nf),
        in_specs=in_specs,
        out_specs=out_specs,
        out_shape=out_shape,
        scratch_shapes=[pltpu.VMEM((tm, D_MODEL), BF16)],
        compiler_params=_cparams(2),
        name="mlp",
    )(*args)
    return res if cast_next else res[0]


def _rope_tables():
    t = np.arange(DEC_SEQ)
    half = HEAD_DIM // 4
    inv = ROPE_THETA ** (-np.arange(half, dtype=np.float32) / half)
    ang_r = (t // GRID_W).astype(np.float32)[:, None] * inv
    ang_c = (t % GRID_W).astype(np.float32)[:, None] * inv
    zero = np.zeros_like(ang_r)
    cos = np.concatenate([np.cos(ang_r)] * 2 + [np.cos(ang_c)] * 2, axis=-1)
    sin_lo = np.concatenate([-np.sin(ang_r), zero, -np.sin(ang_c), zero], axis=-1)
    sin_hi = np.concatenate([zero, np.sin(ang_r), zero, np.sin(ang_c)], axis=-1)
    return tuple(jnp.asarray(a, F32) for a in (cos, sin_lo, sin_hi))


def kernel(x_prompt, x_sample, cache_attn_k, cache_attn_v, cache_na_k, cache_na_v, c, c_ctx, mod_w, mod_b,
           norm1_g, norm2_g, ab_w_in, ab_conv_w, ab_q_norm, ab_k_norm, ab_w_out, na_w_qkv, na_rel_bias,
           na_w_out, mlp_w1, mlp_w2, final_norm_g):
    n_ctx = BATCH * SEQ
    xp = x_prompt.reshape(n_ctx, D_MODEL)
    xs = x_sample.reshape(DEC_BATCH * DEC_SEQ, D_MODEL)

    cond = jnp.concatenate([c_ctx[None, :], c, jnp.zeros((8 - 1 - DEC_BATCH, D_MODEL), F32)], axis=0)
    mods = _modulation(cond, mod_w, mod_b).reshape(2, 8, 6, 1, D_MODEL)

    def ctx_row(i):
        return 0

    def lat_row(tm):
        return lambda i: 1 + i // (DEC_SEQ // tm)

    rope_tabs = _rope_tables()
    fg = final_norm_g.reshape(1, D_MODEL)

    m0 = mods[0]
    g1 = norm1_g[0].reshape(1, D_MODEL)
    w_in = ab_w_in[0].astype(BF16)
    qn = ab_q_norm[0].reshape(1, HEAD_DIM)
    kn = ab_k_norm[0].reshape(1, HEAD_DIM)
    cw = ab_conv_w[0]

    zc_p, q_p, kv_p = _in_proj0(xp, g1, m0, ctx_row, w_in, qn, kn, None, F32, IN0_TM)
    zc_s, q_s, kv_s = _in_proj0(xs, g1, m0, lat_row(IN0_TM), w_in, qn, kn, rope_tabs, BF16, IN0_TM)
    new_attn_k = kv_p[:, :KV_W].reshape(BATCH, 1, SEQ, GQA_KV_HEADS, HEAD_DIM)
    new_attn_v = kv_p[:, KV_W:].reshape(BATCH, 1, SEQ, GQA_KV_HEADS, HEAD_DIM)

    mix_p, w_out = _attn0_ctx(zc_p, q_p, kv_p, cw, [(ab_w_out, 0)])
    lat3 = lambda a: a.reshape(DEC_BATCH, DEC_SEQ, a.shape[-1])
    mix_s, w1, w2 = _attn0_lat(lat3(zc_s), lat3(q_s), lat3(kv_s),
                               cache_attn_k[:, 0].reshape(DEC_BATCH, PAST_LEN, KV_W),
                               cache_attn_v[:, 0].reshape(DEC_BATCH, PAST_LEN, KV_W), cw, LAT0_TQ,
                               [(mlp_w1, 0), (mlp_w2, 0)])
    mix_s = mix_s.reshape(DEC_BATCH * DEC_SEQ, D_MODEL)

    xp = _out_proj(mix_p, w_out, xp, m0, ctx_row, OUT_TM, D_MODEL)
    xs = _out_proj(mix_s, w_out, xs, m0, lat_row(OUT_TM), OUT_TM, D_MODEL)

    g2 = norm2_g[0].reshape(1, D_MODEL)
    xp, w1_next, w2_next = _mlp(xp, g2, m0, ctx_row, w1, w2, fg, False, MLP_TM, MLP_TF,
                                cast_next=[(mlp_w1, 1), (mlp_w2, 1)])
    xs, w_qkv, w_out = _mlp(xs, g2, m0, lat_row(MLP_TM), w1, w2, fg, False, MLP_TM, MLP_TF,
                            cast_next=[(na_w_qkv, 0), (na_w_out, 0)])

    m1 = mods[1]
    g1 = norm1_g[1].reshape(1, D_MODEL)

    q_p, h_p = _in_proj1_q(xp, g1, m1, ctx_row, w_qkv, IN1_CTX_TM, D_MODEL)
    k_p, k_heads = _in_proj1_kv(h_p, w_qkv, 1, IN1_CTX_TM, D_MODEL)
    v_p, v_heads = _in_proj1_kv(h_p, w_qkv, 2, IN1_CTX_TM, D_MODEL)
    qkv_s = _in_proj1(xs, g1, m1, lat_row(IN1_TM), w_qkv, BF16, IN1_TM, IN1_TN)
    new_na_k = k_heads.reshape(BATCH, 1, SEQ, NA_HEADS, HEAD_DIM)
    new_na_v = v_heads.reshape(BATCH, 1, SEQ, NA_HEADS, HEAD_DIM)

    mix_p = _attn1_ctx(q_p, k_p, v_p)
    mix_s = _attn1_lat(qkv_s.reshape(3, DEC_BATCH, DEC_SEQ, D_MODEL),
                       cache_na_k[:, 0].reshape(DEC_BATCH, PAST_LEN * NA_HEADS, HEAD_DIM),
                       cache_na_v[:, 0].reshape(DEC_BATCH, PAST_LEN * NA_HEADS, HEAD_DIM),
                       _na_bias_table(na_rel_bias[0]))
    mix_s = mix_s.reshape(DEC_BATCH * DEC_SEQ, D_MODEL)

    xp = _out_proj(mix_p, w_out, xp, m1, ctx_row, OUT_TM, D_MODEL)
    xs = _out_proj(mix_s, w_out, xs, m1, lat_row(OUT_TM), OUT_TM, D_MODEL)

    g2 = norm2_g[1].reshape(1, D_MODEL)
    yp = _mlp(xp, g2, m1, ctx_row, w1_next, w2_next, fg, True, MLP_TM, MLP_TF)
    ys = _mlp(xs, g2, m1, lat_row(MLP_TM), w1_next, w2_next, fg, True, MLP_TM, MLP_TF)

    return (yp.reshape(BATCH, SEQ, D_MODEL), ys.reshape(DEC_BATCH, DEC_SEQ, D_MODEL),
            new_attn_k, new_attn_v, new_na_k, new_na_v)
```

```python
import functools
import math

import numpy as np
import jax
import jax.numpy as jnp
from jax import lax
from jax.experimental import pallas as pl
from jax.experimental.pallas import tpu as pltpu

D_MODEL = 2048
BATCH = 32
SEQ = 256
DEC_BATCH = 2
DEC_SEQ = 2048
PAST_LEN = 512
GRID_W = 64
GRID_H = DEC_SEQ // GRID_W
HEAD_DIM = 128
CONV_WIDTH = D_MODEL // 2
GQA_HEADS = 8
GQA_KV_HEADS = 2
GQA_GROUP = GQA_HEADS // GQA_KV_HEADS
NA_HEADS = 16
NA_WIN_ROWS = 8
NA_WIN_COLS = 16
D_FF = 4 * D_MODEL
ROPE_THETA = 10000.0
NORM_EPS = 1e-6
AB_IN = 3 * CONV_WIDTH + (GQA_HEADS + 2 * GQA_KV_HEADS) * HEAD_DIM
KV_W = GQA_KV_HEADS * HEAD_DIM
LOG2E = math.log2(math.e)
ATTN_SCALE = HEAD_DIM ** -0.5 * LOG2E

NA_QROWS = 4
NA_SLAB_ROWS = NA_QROWS + NA_WIN_ROWS
NA_QBLK = NA_QROWS * GRID_W
NA_SLAB = NA_SLAB_ROWS * GRID_W
NA_NBLK = GRID_H // NA_QROWS
MASK_NEG = -1e30

V7X_VMEM_LIMIT = 56 * 1024 * 1024

MOD_TN = 1024
IN0_TM = 512
IN1_TM, IN1_TN = 1024, 1024
IN1_CTX_TM = 512
LAT0_TQ = 256
OUT_TM = 512
MLP_TM, MLP_TF = 512, 1024

BF16 = jnp.bfloat16
F32 = jnp.float32


def _cparams(n_axes):
    return pltpu.CompilerParams(
        dimension_semantics=("parallel",) + ("arbitrary",) * (n_axes - 1),
        vmem_limit_bytes=V7X_VMEM_LIMIT)


def _nt_dot(a, b):
    return lax.dot_general(a, b, (((1,), (1,)), ((), ())), preferred_element_type=F32)


NORM_ROWS = 16


def _ada_norm_rows(h_ref, x_ref, g_ref, shift_ref, scale_ref):
    gain = g_ref[...] * (1.0 + scale_ref[...])
    shift = shift_ref[...]

    def body(c, carry):
        rows = pl.ds(pl.multiple_of(c * NORM_ROWS, NORM_ROWS), NORM_ROWS)
        x = x_ref[rows, :]
        ms = jnp.mean(x * x, axis=-1, keepdims=True)
        h_ref[rows, :] = (x * lax.rsqrt(ms + NORM_EPS) * gain + shift).astype(h_ref.dtype)
        return carry

    lax.fori_loop(0, x_ref.shape[0] // NORM_ROWS, body, 0, unroll=8)


def _head_norm(x, g):
    ms = jnp.mean(x * x, axis=-1, keepdims=True)
    return x * lax.rsqrt(ms + NORM_EPS) * g


def _rope(x, cos, sin_lo, sin_hi):
    return x * cos + pltpu.roll(x, HEAD_DIM - 32, 1) * sin_lo + pltpu.roll(x, 32, 1) * sin_hi


def _softmax_pv(scores, values):
    m = scores[0].max(axis=-1, keepdims=True)
    for s in scores[1:]:
        m = jnp.maximum(m, s.max(axis=-1, keepdims=True))
    l = None
    o = None
    for s, v in zip(scores, values):
        p = jnp.exp2(s - m)
        ls = jnp.sum(p, axis=-1, keepdims=True)
        os_ = jnp.dot(p.astype(BF16), v, preferred_element_type=F32)
        l = ls if l is None else l + ls
        o = os_ if o is None else o + os_
    return o / l


_HALO = 16


def _side_cast_specs(cast_list, n_steps, step_of):
    in_specs, args, out_specs, out_shape = [], [], [], []
    for w_f32, layer in cast_list:
        _, rows, cols = w_f32.shape
        r = rows // n_steps
        in_specs.append(pl.BlockSpec((None, r, cols),
                                     functools.partial(lambda *g, l: (l, step_of(*g), 0), l=layer)))
        args.append(w_f32)
        out_specs.append(pl.BlockSpec((r, cols), lambda *g: (step_of(*g), 0)))
        out_shape.append(jax.ShapeDtypeStruct((rows, cols), BF16))
    return in_specs, args, out_specs, out_shape


def _side_cast(src_refs, dst_refs):
    for src_ref, dst_ref in zip(src_refs, dst_refs):
        dst_ref[...] = src_ref[...].astype(BF16)


def _mod_kernel(c_ref, w_ref, b_ref, o_ref):
    c = c_ref[...]
    s = (c * jax.nn.sigmoid(c)).astype(BF16)
    o_ref[...] = jnp.dot(s, w_ref[...].astype(BF16), preferred_element_type=F32) + b_ref[...]


def _modulation(cond, mod_w, mod_b):
    depth, d, n = mod_w.shape
    tn = MOD_TN
    rows = cond.shape[0]
    return pl.pallas_call(
        _mod_kernel,
        grid=(depth, n // tn),
        in_specs=[
            pl.BlockSpec((rows, d), lambda l, j: (0, 0)),
            pl.BlockSpec((None, d, tn), lambda l, j: (l, 0, j)),
            pl.BlockSpec((None, 1, tn), lambda l, j: (l, 0, j)),
        ],
        out_specs=pl.BlockSpec((None, rows, tn), lambda l, j: (l, 0, j)),
        out_shape=jax.ShapeDtypeStruct((depth, rows, n), F32),
        compiler_params=_cparams(2),
        name="modulation",
    )(cond, mod_w, mod_b.reshape(depth, 1, n))


def _mod_spec(which, row_of_tile):
    return pl.BlockSpec((None, None, 1, D_MODEL), lambda i, j: (row_of_tile(i), which, 0, 0))


Q_W = GQA_HEADS * HEAD_DIM
IN0_TN = Q_W + 2 * KV_W


def _in0_kernel(x_ref, g_ref, shift_ref, scale_ref, w_ref, qn_ref, kn_ref, *rest, rope):
    if rope:
        cos_ref, slo_ref, shi_ref, zc_ref, q_ref, kv_ref, h_scr = rest
    else:
        zc_ref, q_ref, kv_ref, kc_ref, vc_ref, h_scr = rest
    j = pl.program_id(1)
    n_zc = 3 * CONV_WIDTH // IN0_TN

    @pl.when(j == 0)
    def _():
        _ada_norm_rows(h_scr, x_ref, g_ref, shift_ref, scale_ref)

    acc = jnp.dot(h_scr[...], w_ref[...], preferred_element_type=F32)
    zc_ref[...] = acc.astype(zc_ref.dtype)

    def normed(blk, gain):
        y = _head_norm(blk, gain)
        if rope:
            y = _rope(y, cos_ref[...], slo_ref[...], shi_ref[...])
        return y

    @pl.when(j == n_zc)
    def _():
        for hh in range(GQA_HEADS):
            sl = slice(hh * HEAD_DIM, (hh + 1) * HEAD_DIM)
            q_ref[:, sl] = (normed(acc[:, sl], qn_ref[...]) * ATTN_SCALE).astype(q_ref.dtype)
        keys = []
        for hh in range(GQA_KV_HEADS):
            src = slice(Q_W + hh * HEAD_DIM, Q_W + (hh + 1) * HEAD_DIM)
            keys.append(normed(acc[:, src], kn_ref[...]))
            kv_ref[:, hh * HEAD_DIM:(hh + 1) * HEAD_DIM] = keys[hh].astype(kv_ref.dtype)
        kv_ref[:, KV_W:] = acc[:, Q_W + KV_W:].astype(kv_ref.dtype)
        if not rope:
            kc_ref[...] = jnp.stack(keys, axis=1)
            vc_ref[...] = acc[:, Q_W + KV_W:].reshape(vc_ref.shape)


def _in_proj0(x, norm_g, mods, row_of_tile, w, qn, kn, rope_tabs, tm):
    t = x.shape[0]
    tn = IN0_TN
    rope = rope_tabs is not None
    vec = pl.BlockSpec((1, HEAD_DIM), lambda i, j: (0, 0))
    in_specs = [
        pl.BlockSpec((tm, D_MODEL), lambda i, j: (i, 0)),
        pl.BlockSpec((1, D_MODEL), lambda i, j: (0, 0)),
        _mod_spec(0, row_of_tile),
        _mod_spec(1, row_of_tile),
        pl.BlockSpec((D_MODEL, tn), lambda i, j: (0, j)),
        vec, vec,
    ]
    args = [x, norm_g, mods, mods, w, qn, kn]
    if rope:
        nt = DEC_SEQ // tm
        tab = pl.BlockSpec((tm, HEAD_DIM), lambda i, j: (i % nt, 0))
        in_specs += [tab, tab, tab]
        args += list(rope_tabs)
    out_specs = [
        pl.BlockSpec((tm, tn), lambda i, j: (i, j)),
        pl.BlockSpec((tm, Q_W), lambda i, j: (i, 0)),
        pl.BlockSpec((tm, 2 * KV_W), lambda i, j: (i, 0)),
    ]
    out_shape = [
        jax.ShapeDtypeStruct((t, AB_IN), BF16),
        jax.ShapeDtypeStruct((t, GQA_HEADS * HEAD_DIM), BF16),
        jax.ShapeDtypeStruct((t, 2 * KV_W), BF16),
    ]
    if not rope:
        cache = pl.BlockSpec((tm, GQA_KV_HEADS, HEAD_DIM), lambda i, j: (i, 0, 0))
        out_specs += [cache, cache]
        out_shape += [jax.ShapeDtypeStruct((t, GQA_KV_HEADS, HEAD_DIM), F32)] * 2
    return pl.pallas_call(
        functools.partial(_in0_kernel, rope=rope),
        grid=(t // tm, AB_IN // tn),
        in_specs=in_specs,
        out_specs=out_specs,
        out_shape=out_shape,
        scratch_shapes=[pltpu.VMEM((tm, D_MODEL), BF16)],
        compiler_params=_cparams(2),
        name="in_proj0",
    )(*args)


def _in1_kernel(x_ref, g_ref, shift_ref, scale_ref, w_ref, o_ref, h_scr, *, nb):
    j = pl.program_id(1)

    @pl.when(j == 0)
    def _():
        _ada_norm_rows(h_scr, x_ref, g_ref, shift_ref, scale_ref)

    acc = jnp.dot(h_scr[...], w_ref[...], preferred_element_type=F32)
    o_ref[...] = (acc * jnp.where(j < nb, ATTN_SCALE, 1.0)).astype(o_ref.dtype)


def _in_proj1(x, norm_g, mods, row_of_tile, w, out_dtype, tm, tn):
    t = x.shape[0]
    nb = D_MODEL // tn
    return pl.pallas_call(
        functools.partial(_in1_kernel, nb=nb),
        grid=(t // tm, 3 * nb),
        in_specs=[
            pl.BlockSpec((tm, D_MODEL), lambda i, j: (i, 0)),
            pl.BlockSpec((1, D_MODEL), lambda i, j: (0, 0)),
            _mod_spec(0, row_of_tile),
            _mod_spec(1, row_of_tile),
            pl.BlockSpec((D_MODEL, tn), lambda i, j: (0, j)),
        ],
        out_specs=pl.BlockSpec((None, tm, tn), lambda i, j: (j // nb, i, j % nb)),
        out_shape=jax.ShapeDtypeStruct((3, t, D_MODEL), out_dtype),
        scratch_shapes=[pltpu.VMEM((tm, D_MODEL), BF16)],
        compiler_params=_cparams(2),
        name="in_proj1",
    )(x, norm_g, mods, mods, w)


def _in1q_kernel(x_ref, g_ref, shift_ref, scale_ref, w_ref, q_ref, h_ref):
    @pl.when(pl.program_id(1) == 0)
    def _():
        _ada_norm_rows(h_ref, x_ref, g_ref, shift_ref, scale_ref)

    acc = jnp.dot(h_ref[...], w_ref[...], preferred_element_type=F32)
    q_ref[...] = (acc * ATTN_SCALE).astype(q_ref.dtype)


def _in_proj1_q(x, norm_g, mods, row_of_tile, w, tm, tn):
    t = x.shape[0]
    return pl.pallas_call(
        _in1q_kernel,
        grid=(t // tm, D_MODEL // tn),
        in_specs=[
            pl.BlockSpec((tm, D_MODEL), lambda i, j: (i, 0)),
            pl.BlockSpec((1, D_MODEL), lambda i, j: (0, 0)),
            _mod_spec(0, row_of_tile),
            _mod_spec(1, row_of_tile),
            pl.BlockSpec((D_MODEL, tn), lambda i, j: (0, j)),
        ],
        out_specs=[pl.BlockSpec((tm, tn), lambda i, j: (i, j)),
                   pl.BlockSpec((tm, D_MODEL), lambda i, j: (i, 0))],
        out_shape=[jax.ShapeDtypeStruct((t, D_MODEL), BF16), jax.ShapeDtypeStruct((t, D_MODEL), BF16)],
        compiler_params=_cparams(2),
        name="in_proj1_q",
    )(x, norm_g, mods, mods, w)


def _in1kv_kernel(h_ref, w_ref, flat_ref, heads_ref):
    acc = jnp.dot(h_ref[...], w_ref[...], preferred_element_type=F32)
    flat_ref[...] = acc.astype(flat_ref.dtype)
    heads_ref[...] = acc.reshape(heads_ref.shape)


def _in_proj1_kv(h, w, part, tm, tn):
    t = h.shape[0]
    nb = D_MODEL // tn
    hb = tn // HEAD_DIM
    return pl.pallas_call(
        _in1kv_kernel,
        grid=(t // tm, nb),
        in_specs=[
            pl.BlockSpec((tm, D_MODEL), lambda i, j: (i, 0)),
            pl.BlockSpec((D_MODEL, tn), lambda i, j: (0, part * nb + j)),
        ],
        out_specs=[pl.BlockSpec((tm, tn), lambda i, j: (i, j)),
                   pl.BlockSpec((tm, hb, HEAD_DIM), lambda i, j: (i, j, 0))],
        out_shape=[jax.ShapeDtypeStruct((t, D_MODEL), BF16),
                   jax.ShapeDtypeStruct((t, NA_HEADS, HEAD_DIM), F32)],
        compiler_params=_cparams(2),
        name="in_proj1_kv",
    )(h, w)


def _gated_conv(zc, prev_row, next_row, cw):
    c = CONV_WIDTH
    s = zc.shape[0]
    gb = zc[:, 0:c].astype(F32)
    u = zc[:, c:2 * c].astype(F32) * zc[:, 2 * c:3 * c].astype(F32)
    row = lax.broadcasted_iota(jnp.int32, u.shape, 0)
    prev = jnp.where(row == 0, prev_row, pltpu.roll(u, 1, 0))
    nxt = jnp.where(row == s - 1, next_row, pltpu.roll(u, s - 1, 0))
    return gb * (prev * cw[0:1] + u * cw[1:2] + nxt * cw[2:3])


def _attn0_ctx_kernel(zc_ref, q_ref, kv_ref, cw_ref, *rest):
    n_cast = len(rest) // 2
    o_ref = rest[n_cast]
    _side_cast(rest[:n_cast], rest[n_cast + 1:])
    zero = jnp.zeros((1, CONV_WIDTH), F32)
    a = _gated_conv(zc_ref[...], zero, zero, cw_ref[...])
    o_ref[:, 0:CONV_WIDTH] = a.astype(o_ref.dtype)
    for g in range(GQA_KV_HEADS):
        k = kv_ref[:, g * HEAD_DIM:(g + 1) * HEAD_DIM]
        v = kv_ref[:, KV_W + g * HEAD_DIM:KV_W + (g + 1) * HEAD_DIM]
        for hh in range(GQA_GROUP):
            h = g * GQA_GROUP + hh
            q = q_ref[:, h * HEAD_DIM:(h + 1) * HEAD_DIM]
            o = _softmax_pv([_nt_dot(q, k)], [v])
            o_ref[:, CONV_WIDTH + h * HEAD_DIM:CONV_WIDTH + (h + 1) * HEAD_DIM] = o.astype(o_ref.dtype)


def _attn0_ctx(zc, q, kv, cw, cast_list):
    t = zc.shape[0]
    c_in, c_args, c_out, c_shape = _side_cast_specs(cast_list, t // SEQ, lambda b: b)
    return pl.pallas_call(
        _attn0_ctx_kernel,
        grid=(t // SEQ,),
        in_specs=[
            pl.BlockSpec((SEQ, 3 * CONV_WIDTH), lambda b: (b, 0)),
            pl.BlockSpec((SEQ, GQA_HEADS * HEAD_DIM), lambda b: (b, 0)),
            pl.BlockSpec((SEQ, 2 * KV_W), lambda b: (b, 0)),
            pl.BlockSpec((3, CONV_WIDTH), lambda b: (0, 0)),
        ] + c_in,
        out_specs=[pl.BlockSpec((SEQ, D_MODEL), lambda b: (b, 0))] + c_out,
        out_shape=[jax.ShapeDtypeStruct((t, D_MODEL), BF16)] + c_shape,
        compiler_params=_cparams(1),
        name="mixer0_ctx",
    )(zc, q, kv, cw, *c_args)


def _attn0_lat_kernel(zc_ref, zp_ref, zn_ref, q_ref, kv_ref, ck_ref, cv_ref, cw_ref, *rest, nqb):
    n_cast = len(rest) // 2
    o_ref = rest[n_cast]
    _side_cast(rest[:n_cast], rest[n_cast + 1:])
    qb = pl.program_id(1)
    c = CONV_WIDTH
    up = zp_ref[:, c:2 * c].astype(F32) * zp_ref[:, 2 * c:3 * c].astype(F32)
    un = zn_ref[:, c:2 * c].astype(F32) * zn_ref[:, 2 * c:3 * c].astype(F32)
    up = jnp.where(qb == 0, 0.0, up[_HALO - 1:_HALO, :])
    un = jnp.where(qb == nqb - 1, 0.0, un[0:1, :])
    a = _gated_conv(zc_ref[...], up, un, cw_ref[...])
    o_ref[:, 0:c] = a.astype(o_ref.dtype)
    for g in range(GQA_KV_HEADS):
        ks = slice(g * HEAD_DIM, (g + 1) * HEAD_DIM)
        vs = slice(KV_W + g * HEAD_DIM, KV_W + (g + 1) * HEAD_DIM)
        k = kv_ref[:, ks]
        v = kv_ref[:, vs]
        ck = ck_ref[:, ks].astype(BF16)
        cv = cv_ref[:, ks].astype(BF16)
        for hh in range(GQA_GROUP):
            h = g * GQA_GROUP + hh
            q = q_ref[:, h * HEAD_DIM:(h + 1) * HEAD_DIM]
            o = _softmax_pv([_nt_dot(q, ck), _nt_dot(q, k)], [cv, v])
            o_ref[:, c + h * HEAD_DIM:c + (h + 1) * HEAD_DIM] = o.astype(o_ref.dtype)


def _attn0_lat(zc, q, kv, ck, cv, cw, tq, cast_list):
    nb, t, _ = zc.shape
    nqb = t // tq
    hb = tq // _HALO
    c_in, c_args, c_out, c_shape = _side_cast_specs(cast_list, nb * nqb, lambda b, i: b * nqb + i)
    return pl.pallas_call(
        functools.partial(_attn0_lat_kernel, nqb=nqb),
        grid=(nb, nqb),
        in_specs=[
            pl.BlockSpec((None, tq, 3 * CONV_WIDTH), lambda b, i: (b, i, 0)),
            pl.BlockSpec((None, _HALO, 3 * CONV_WIDTH), lambda b, i: (b, jnp.maximum(i * hb - 1, 0), 0)),
            pl.BlockSpec((None, _HALO, 3 * CONV_WIDTH),
                         lambda b, i: (b, jnp.minimum((i + 1) * hb, t // _HALO - 1), 0)),
            pl.BlockSpec((None, tq, GQA_HEADS * HEAD_DIM), lambda b, i: (b, i, 0)),
            pl.BlockSpec((None, t, 2 * KV_W), lambda b, i: (b, 0, 0)),
            pl.BlockSpec((None, PAST_LEN, KV_W), lambda b, i: (b, 0, 0)),
            pl.BlockSpec((None, PAST_LEN, KV_W), lambda b, i: (b, 0, 0)),
            pl.BlockSpec((3, CONV_WIDTH), lambda b, i: (0, 0)),
        ] + c_in,
        out_specs=[pl.BlockSpec((None, tq, D_MODEL), lambda b, i: (b, i, 0))] + c_out,
        out_shape=[jax.ShapeDtypeStruct((nb, t, D_MODEL), BF16)] + c_shape,
        compiler_params=_cparams(2),
        name="mixer0_lat",
    )(zc, zc, zc, q, kv, ck, cv, cw, *c_args)


def _attn1_ctx_kernel(q_ref, k_ref, v_ref, o_ref):
    for h in range(NA_HEADS):
        sl = slice(h * HEAD_DIM, (h + 1) * HEAD_DIM)
        o = _softmax_pv([_nt_dot(q_ref[:, sl], k_ref[:, sl])], [v_ref[:, sl]])
        o_ref[:, sl] = o.astype(o_ref.dtype)


def _attn1_ctx(q, k, v):
    t = q.shape[0]
    blk = pl.BlockSpec((SEQ, D_MODEL), lambda b: (b, 0))
    return pl.pallas_call(
        _attn1_ctx_kernel,
        grid=(t // SEQ,),
        in_specs=[blk, blk, blk],
        out_specs=blk,
        out_shape=jax.ShapeDtypeStruct((t, D_MODEL), BF16),
        compiler_params=_cparams(1),
        name="mixer1_ctx",
    )(q, k, v)


def _na_slab_row(blk):
    return min(max(NA_QROWS * blk - NA_WIN_ROWS // 2, 0), GRID_H - NA_SLAB_ROWS)


NA_NROW = 2 * NA_WIN_ROWS - 1
NA_MASKED = NA_NROW
NA_FIRST_IN = NA_WIN_ROWS - 1 - NA_WIN_ROWS // 2
NA_LAST_IN = NA_FIRST_IN + NA_WIN_ROWS - 1
NA_PIECE_PAIRS = ([(m, m + 1) for m in range(NA_NROW - 1)]
                  + [(NA_MASKED, NA_FIRST_IN), (NA_LAST_IN, NA_MASKED)])


def _na_piece(blk, dr, jp):
    r = NA_QROWS * blk + dr
    ks = min(max(r - NA_WIN_ROWS // 2, 0), GRID_H - NA_WIN_ROWS)
    pair = []
    for j in (2 * jp, 2 * jp + 1):
        krow = _na_slab_row(blk) + j
        pair.append(krow - r + NA_WIN_ROWS - 1 if ks <= krow < ks + NA_WIN_ROWS else NA_MASKED)
    pair = tuple(pair)
    if pair == (NA_MASKED, NA_MASKED):
        return None
    return NA_PIECE_PAIRS.index(pair)


NA_PAD = GRID_W - NA_WIN_COLS


def _na_bias_table(rel_bias):
    return jnp.pad(rel_bias, ((0, 0), (0, 1), (NA_PAD, 2 * GRID_W - (2 * NA_WIN_COLS - 1) - NA_PAD)),
                   constant_values=MASK_NEG)


def _na_fill_pieces(table_ref, pieces_ref):
    shape = (GRID_W, 2 * GRID_W)
    c = lax.broadcasted_iota(jnp.int32, shape, 0)
    lane = lax.broadcasted_iota(jnp.int32, shape, 1)
    kc = lane & (GRID_W - 1)
    cs = jnp.clip(c - NA_WIN_COLS // 2, 0, GRID_W - NA_WIN_COLS)
    in_window = (kc >= cs) & (kc < cs + NA_WIN_COLS)

    def expand(a, shift):
        row = jnp.broadcast_to(table_ref[a:a + 1, :], shape)
        return pltpu.roll(row, shift, 1, stride=1, stride_axis=0)

    left_shift = 2 * GRID_W - (GRID_W - 1)
    right_shift = 1
    for m, (a_left, a_right) in enumerate(NA_PIECE_PAIRS):
        both = jnp.where(lane < GRID_W, expand(a_left, left_shift), expand(a_right, right_shift))
        pieces_ref[m] = jnp.where(in_window, both * LOG2E, MASK_NEG)


def _na_bias_block(pieces_ref, blk):
    rows = []
    for dr in range(NA_QROWS):
        cols = []
        for jp in range(NA_SLAB_ROWS // 2):
            m = _na_piece(blk, dr, jp)
            cols.append(jnp.full((GRID_W, 2 * GRID_W), MASK_NEG, F32) if m is None else pieces_ref[m])
        rows.append(jnp.concatenate(cols, axis=1))
    return jnp.concatenate(rows, axis=0)


def _attn1_lat_kernel(q_ref, k_ref, v_ref, ck_ref, cv_ref, table_ref, o_ref, pieces_ref):
    _na_fill_pieces(table_ref, pieces_ref)
    head_rows = pl.ds(pl.program_id(1), PAST_LEN, stride=NA_HEADS)
    ck = ck_ref[head_rows, :].astype(BF16)
    cv = cv_ref[head_rows, :].astype(BF16)
    for blk in range(NA_NBLK):
        rows = slice(blk * NA_QBLK, (blk + 1) * NA_QBLK)
        s0 = _na_slab_row(blk) * GRID_W
        slab = slice(s0, s0 + NA_SLAB)
        q = q_ref[rows, :]
        s_loc = _nt_dot(q, k_ref[slab, :]) + _na_bias_block(pieces_ref, blk)
        o = _softmax_pv([s_loc, _nt_dot(q, ck)], [v_ref[slab, :], cv])
        o_ref[rows, :] = o.astype(o_ref.dtype)


def _attn1_lat(qkv, ck, cv, table):
    _, nb, t, _ = qkv.shape
    part = lambda p: pl.BlockSpec((None, None, t, HEAD_DIM), lambda b, h: (p, b, 0, h))
    head = pl.BlockSpec((None, t, HEAD_DIM), lambda b, h: (b, 0, h))
    ctx = pl.BlockSpec((None, PAST_LEN * NA_HEADS, HEAD_DIM), lambda b, h: (b, 0, 0))
    return pl.pallas_call(
        _attn1_lat_kernel,
        grid=(nb, NA_HEADS),
        in_specs=[part(0), part(1), part(2), ctx, ctx,
                  pl.BlockSpec((None, NA_NROW + 1, 2 * GRID_W), lambda b, h: (h, 0, 0))],
        out_specs=head,
        out_shape=jax.ShapeDtypeStruct((nb, t, D_MODEL), BF16),
        scratch_shapes=[pltpu.VMEM((len(NA_PIECE_PAIRS), GRID_W, 2 * GRID_W), F32)],
        compiler_params=_cparams(2),
        name="mixer1_lat",
    )(qkv, qkv, qkv, ck, cv, table)


def _out_proj_kernel(m_ref, w_ref, x_ref, gate_ref, o_ref):
    acc = jnp.dot(m_ref[...], w_ref[...], preferred_element_type=F32)
    o_ref[...] = x_ref[...] + gate_ref[...] * acc


def _out_proj(mix, w, x, mods, row_of_tile, tm, tn):
    t, k = mix.shape
    return pl.pallas_call(
        _out_proj_kernel,
        grid=(t // tm, D_MODEL // tn),
        in_specs=[
            pl.BlockSpec((tm, k), lambda i, j: (i, 0)),
            pl.BlockSpec((k, tn), lambda i, j: (0, j)),
            pl.BlockSpec((tm, tn), lambda i, j: (i, j)),
            pl.BlockSpec((None, None, 1, tn), lambda i, j: (row_of_tile(i), 2, 0, j)),
        ],
        out_specs=pl.BlockSpec((tm, tn), lambda i, j: (i, j)),
        out_shape=jax.ShapeDtypeStruct((t, D_MODEL), F32),
        compiler_params=_cparams(2),
        name="out_proj",
    )(mix, w, x, mods)


def _mlp_kernel(x_ref, g_ref, shift_ref, scale_ref, gate_ref, w1_ref, w2_ref, fg_ref, *rest, nf, final):
    n_cast = (len(rest) - 2) // 2
    o_ref, h_scr = rest[n_cast], rest[-1]
    _side_cast(rest[:n_cast], rest[n_cast + 1:-1])
    f = pl.program_id(1)

    @pl.when(f == 0)
    def _():
        _ada_norm_rows(h_scr, x_ref, g_ref, shift_ref, scale_ref)
        o_ref[...] = jnp.zeros_like(o_ref)

    u = jnp.dot(h_scr[...], w1_ref[...], preferred_element_type=F32)
    u = jnp.square(jnp.maximum(u, 0.0)).astype(BF16)
    o_ref[...] += jnp.dot(u, w2_ref[...], preferred_element_type=F32)

    @pl.when(f == nf - 1)
    def _():
        y = x_ref[...] + gate_ref[...] * o_ref[...]
        if final:
            ms = jnp.mean(y * y, axis=-1, keepdims=True)
            y = y * lax.rsqrt(ms + NORM_EPS) * fg_ref[...]
        o_ref[...] = y


def _mlp(x, norm_g, mods, row_of_tile, w1, w2, final_g, final, tm, tf, cast_next=None):
    t = x.shape[0]
    nf = D_FF // tf
    nt = t // tm

    def mod(which):
        return pl.BlockSpec((None, None, 1, D_MODEL), lambda i, f: (row_of_tile(i), which, 0, 0))

    in_specs = [
        pl.BlockSpec((tm, D_MODEL), lambda i, f: (i, 0)),
        pl.BlockSpec((1, D_MODEL), lambda i, f: (0, 0)),
        mod(3), mod(4), mod(5),
        pl.BlockSpec((D_MODEL, tf), lambda i, f: (0, f)),
        pl.BlockSpec((tf, D_MODEL), lambda i, f: (f, 0)),
        pl.BlockSpec((1, D_MODEL), lambda i, f: (0, 0)),
    ]
    args = [x, norm_g, mods, mods, mods, w1, w2, final_g]
    out_specs = [pl.BlockSpec((tm, D_MODEL), lambda i, f: (i, 0))]
    out_shape = [jax.ShapeDtypeStruct((t, D_MODEL), F32)]
    c_in, c_args, c_out, c_shape = _side_cast_specs(cast_next or (), nt * nf, lambda i, f: i * nf + f)
    in_specs += c_in
    args += c_args
    out_specs += c_out
    out_shape += c_shape
    res = pl.pallas_call(
        functools.partial(_mlp_kernel, nf=nf, final=final),
        grid=(nt, nf),
        in_specs=in_specs,
        out_specs=out_specs,
        out_shape=out_shape,
        scratch_shapes=[pltpu.VMEM((tm, D_MODEL), BF16)],
        compiler_params=_cparams(2),
        name="mlp",
    )(*args)
    return res if cast_next else res[0]


def _rope_tables():
    t = np.arange(DEC_SEQ)
    half = HEAD_DIM // 4
    inv = ROPE_THETA ** (-np.arange(half, dtype=np.float32) / half)
    ang_r = (t // GRID_W).astype(np.float32)[:, None] * inv
    ang_c = (t % GRID_W).astype(np.float32)[:, None] * inv
    zero = np.zeros_like(ang_r)
    cos = np.concatenate([np.cos(ang_r)] * 2 + [np.cos(ang_c)] * 2, axis=-1)
    sin_lo = np.concatenate([-np.sin(ang_r), zero, -np.sin(ang_c), zero], axis=-1)
    sin_hi = np.concatenate([zero, np.sin(ang_r), zero, np.sin(ang_c)], axis=-1)
    return tuple(jnp.asarray(a, F32) for a in (cos, sin_lo, sin_hi))


def kernel(x_prompt, x_sample, cache_attn_k, cache_attn_v, cache_na_k, cache_na_v, c, c_ctx, mod_w, mod_b,
           norm1_g, norm2_g, ab_w_in, ab_conv_w, ab_q_norm, ab_k_norm, ab_w_out, na_w_qkv, na_rel_bias,
           na_w_out, mlp_w1, mlp_w2, final_norm_g):
    n_ctx = BATCH * SEQ
    xp = x_prompt.reshape(n_ctx, D_MODEL)
    xs = x_sample.reshape(DEC_BATCH * DEC_SEQ, D_MODEL)

    cond = jnp.concatenate([c_ctx[None, :], c, jnp.zeros((8 - 1 - DEC_BATCH, D_MODEL), F32)], axis=0)
    mods = _modulation(cond, mod_w, mod_b).reshape(2, 8, 6, 1, D_MODEL)

    def ctx_row(i):
        return 0

    def lat_row(tm):
        return lambda i: 1 + i // (DEC_SEQ // tm)

    rope_tabs = _rope_tables()
    fg = final_norm_g.reshape(1, D_MODEL)

    m0 = mods[0]
    g1 = norm1_g[0].reshape(1, D_MODEL)
    w_in = ab_w_in[0].astype(BF16)
    qn = ab_q_norm[0].reshape(1, HEAD_DIM)
    kn = ab_k_norm[0].reshape(1, HEAD_DIM)
    cw = ab_conv_w[0]

    zc_p, q_p, kv_p, k_cache, v_cache = _in_proj0(xp, g1, m0, ctx_row, w_in, qn, kn, None, IN0_TM)
    zc_s, q_s, kv_s = _in_proj0(xs, g1, m0, lat_row(IN0_TM), w_in, qn, kn, rope_tabs, IN0_TM)
    new_attn_k = k_cache.reshape(BATCH, 1, SEQ, GQA_KV_HEADS, HEAD_DIM)
    new_attn_v = v_cache.reshape(BATCH, 1, SEQ, GQA_KV_HEADS, HEAD_DIM)

    mix_p, w_out = _attn0_ctx(zc_p, q_p, kv_p, cw, [(ab_w_out, 0)])
    lat3 = lambda a: a.reshape(DEC_BATCH, DEC_SEQ, a.shape[-1])
    mix_s, w1, w2 = _attn0_lat(lat3(zc_s), lat3(q_s), lat3(kv_s),
                               cache_attn_k[:, 0].reshape(DEC_BATCH, PAST_LEN, KV_W),
                               cache_attn_v[:, 0].reshape(DEC_BATCH, PAST_LEN, KV_W), cw, LAT0_TQ,
                               [(mlp_w1, 0), (mlp_w2, 0)])
    mix_s = mix_s.reshape(DEC_BATCH * DEC_SEQ, D_MODEL)

    xp = _out_proj(mix_p, w_out, xp, m0, ctx_row, OUT_TM, D_MODEL)
    xs = _out_proj(mix_s, w_out, xs, m0, lat_row(OUT_TM), OUT_TM, D_MODEL)

    g2 = norm2_g[0].reshape(1, D_MODEL)
    xp, w1_next, w2_next = _mlp(xp, g2, m0, ctx_row, w1, w2, fg, False, MLP_TM, MLP_TF,
                                cast_next=[(mlp_w1, 1), (mlp_w2, 1)])
    xs, w_qkv, w_out = _mlp(xs, g2, m0, lat_row(MLP_TM), w1, w2, fg, False, MLP_TM, MLP_TF,
                            cast_next=[(na_w_qkv, 0), (na_w_out, 0)])

    m1 = mods[1]
    g1 = norm1_g[1].reshape(1, D_MODEL)

    q_p, h_p = _in_proj1_q(xp, g1, m1, ctx_row, w_qkv, IN1_CTX_TM, D_MODEL)
    k_p, k_heads = _in_proj1_kv(h_p, w_qkv, 1, IN1_CTX_TM, D_MODEL)
    v_p, v_heads = _in_proj1_kv(h_p, w_qkv, 2, IN1_CTX_TM, D_MODEL)
    qkv_s = _in_proj1(xs, g1, m1, lat_row(IN1_TM), w_qkv, BF16, IN1_TM, IN1_TN)
    new_na_k = k_heads.reshape(BATCH, 1, SEQ, NA_HEADS, HEAD_DIM)
    new_na_v = v_heads.reshape(BATCH, 1, SEQ, NA_HEADS, HEAD_DIM)

    mix_p = _attn1_ctx(q_p, k_p, v_p)
    mix_s = _attn1_lat(qkv_s.reshape(3, DEC_BATCH, DEC_SEQ, D_MODEL),
                       cache_na_k[:, 0].reshape(DEC_BATCH, PAST_LEN * NA_HEADS, HEAD_DIM),
                       cache_na_v[:, 0].reshape(DEC_BATCH, PAST_LEN * NA_HEADS, HEAD_DIM),
                       _na_bias_table(na_rel_bias[0]))
    mix_s = mix_s.reshape(DEC_BATCH * DEC_SEQ, D_MODEL)

    xp = _out_proj(mix_p, w_out, xp, m1, ctx_row, OUT_TM, D_MODEL)
    xs = _out_proj(mix_s, w_out, xs, m1, lat_row(OUT_TM), OUT_TM, D_MODEL)

    g2 = norm2_g[1].reshape(1, D_MODEL)
    yp = _mlp(xp, g2, m1, ctx_row, w1_next, w2_next, fg, True, MLP_TM, MLP_TF)
    ys = _mlp(xs, g2, m1, lat_row(MLP_TM), w1_next, w2_next, fg, True, MLP_TM, MLP_TF)

    return (yp.reshape(BATCH, SEQ, D_MODEL), ys.reshape(DEC_BATCH, DEC_SEQ, D_MODEL),
            new_attn_k, new_attn_v, new_na_k, new_na_v)
```

```python
import functools
import math

import numpy as np
import jax
import jax.numpy as jnp
from jax import lax
from jax.experimental import pallas as pl
from jax.experimental.pallas import tpu as pltpu

D_MODEL = 2048
BATCH = 32
SEQ = 256
DEC_BATCH = 2
DEC_SEQ = 2048
PAST_LEN = 512
GRID_W = 64
GRID_H = DEC_SEQ // GRID_W
HEAD_DIM = 128
CONV_WIDTH = D_MODEL // 2
GQA_HEADS = 8
GQA_KV_HEADS = 2
GQA_GROUP = GQA_HEADS // GQA_KV_HEADS
NA_HEADS = 16
NA_WIN_ROWS = 8
NA_WIN_COLS = 16
D_FF = 4 * D_MODEL
ROPE_THETA = 10000.0
NORM_EPS = 1e-6
AB_IN = 3 * CONV_WIDTH + (GQA_HEADS + 2 * GQA_KV_HEADS) * HEAD_DIM
KV_W = GQA_KV_HEADS * HEAD_DIM
LOG2E = math.log2(math.e)
ATTN_SCALE = HEAD_DIM ** -0.5 * LOG2E

NA_QROWS = 4
NA_SLAB_ROWS = NA_QROWS + NA_WIN_ROWS
NA_QBLK = NA_QROWS * GRID_W
NA_SLAB = NA_SLAB_ROWS * GRID_W
NA_NBLK = GRID_H // NA_QROWS
MASK_NEG = -1e30

V7X_VMEM_LIMIT = 56 * 1024 * 1024

MOD_TN = 1024
IN0_TM = 512
IN1_TM, IN1_TN = 1024, 1024
IN1_CTX_TM = 512
LAT0_TQ = 256
OUT_TM = 512
MLP_TM, MLP_TF = 512, 1024

BF16 = jnp.bfloat16
F32 = jnp.float32


def _cparams(n_axes):
    return pltpu.CompilerParams(
        dimension_semantics=("parallel",) + ("arbitrary",) * (n_axes - 1),
        vmem_limit_bytes=V7X_VMEM_LIMIT)


def _nt_dot(a, b):
    return lax.dot_general(a, b, (((1,), (1,)), ((), ())), preferred_element_type=F32)


NORM_ROWS = 16


def _ada_norm_rows(h_ref, x_ref, g_ref, shift_ref, scale_ref):
    gain = g_ref[...] * (1.0 + scale_ref[...])
    shift = shift_ref[...]

    def body(c, carry):
        rows = pl.ds(pl.multiple_of(c * NORM_ROWS, NORM_ROWS), NORM_ROWS)
        x = x_ref[rows, :]
        ms = jnp.mean(x * x, axis=-1, keepdims=True)
        h_ref[rows, :] = (x * lax.rsqrt(ms + NORM_EPS) * gain + shift).astype(h_ref.dtype)
        return carry

    lax.fori_loop(0, x_ref.shape[0] // NORM_ROWS, body, 0, unroll=8)


def _head_norm(x, g):
    ms = jnp.mean(x * x, axis=-1, keepdims=True)
    return x * lax.rsqrt(ms + NORM_EPS) * g


def _rope(x, cos, sin_lo, sin_hi):
    return x * cos + pltpu.roll(x, HEAD_DIM - 32, 1) * sin_lo + pltpu.roll(x, 32, 1) * sin_hi


def _softmax_pv(scores, values):
    m = scores[0].max(axis=-1, keepdims=True)
    for s in scores[1:]:
        m = jnp.maximum(m, s.max(axis=-1, keepdims=True))
    l = None
    o = None
    for s, v in zip(scores, values):
        p = jnp.exp2(s - m)
        ls = jnp.sum(p, axis=-1, keepdims=True)
        os_ = jnp.dot(p.astype(BF16), v, preferred_element_type=F32)
        l = ls if l is None else l + ls
        o = os_ if o is None else o + os_
    return o / l


_HALO = 16


def _side_cast_specs(cast_list, n_steps, step_of):
    in_specs, args, out_specs, out_shape = [], [], [], []
    for w_f32, layer in cast_list:
        _, rows, cols = w_f32.shape
        r = rows // n_steps
        in_specs.append(pl.BlockSpec((None, r, cols),
                                     functools.partial(lambda *g, l: (l, step_of(*g), 0), l=layer)))
        args.append(w_f32)
        out_specs.append(pl.BlockSpec((r, cols), lambda *g: (step_of(*g), 0)))
        out_shape.append(jax.ShapeDtypeStruct((rows, cols), BF16))
    return in_specs, args, out_specs, out_shape


def _side_cast(src_refs, dst_refs):
    for src_ref, dst_ref in zip(src_refs, dst_refs):
        dst_ref[...] = src_ref[...].astype(BF16)


def _mod_kernel(c_ref, w_ref, b_ref, o_ref):
    c = c_ref[...]
    s = (c * jax.nn.sigmoid(c)).astype(BF16)
    o_ref[...] = jnp.dot(s, w_ref[...].astype(BF16), preferred_element_type=F32) + b_ref[...]


def _modulation(cond, mod_w, mod_b):
    depth, d, n = mod_w.shape
    tn = MOD_TN
    rows = cond.shape[0]
    return pl.pallas_call(
        _mod_kernel,
        grid=(depth, n // tn),
        in_specs=[
            pl.BlockSpec((rows, d), lambda l, j: (0, 0)),
            pl.BlockSpec((None, d, tn), lambda l, j: (l, 0, j)),
            pl.BlockSpec((None, 1, tn), lambda l, j: (l, 0, j)),
        ],
        out_specs=pl.BlockSpec((None, rows, tn), lambda l, j: (l, 0, j)),
        out_shape=jax.ShapeDtypeStruct((depth, rows, n), F32),
        compiler_params=_cparams(2),
        name="modulation",
    )(cond, mod_w, mod_b.reshape(depth, 1, n))


def _mod_spec(which, row_of_tile):
    return pl.BlockSpec((None, None, 1, D_MODEL), lambda i, j: (row_of_tile(i), which, 0, 0))


Q_W = GQA_HEADS * HEAD_DIM
IN0_TN = Q_W + 2 * KV_W


def _in0_kernel(x_ref, g_ref, shift_ref, scale_ref, w_ref, qn_ref, kn_ref, *rest, rope):
    if rope:
        cos_ref, slo_ref, shi_ref, zc_ref, q_ref, kv_ref, h_scr = rest
    else:
        zc_ref, q_ref, kv_ref, kc_ref, vc_ref, h_scr = rest
    j = pl.program_id(1)
    n_zc = 3 * CONV_WIDTH // IN0_TN

    @pl.when(j == 0)
    def _():
        _ada_norm_rows(h_scr, x_ref, g_ref, shift_ref, scale_ref)

    acc = jnp.dot(h_scr[...], w_ref[...], preferred_element_type=F32)
    zc_ref[...] = acc.astype(zc_ref.dtype)

    def normed(blk, gain):
        y = _head_norm(blk, gain)
        if rope:
            y = _rope(y, cos_ref[...], slo_ref[...], shi_ref[...])
        return y

    @pl.when(j == n_zc)
    def _():
        for hh in range(GQA_HEADS):
            sl = slice(hh * HEAD_DIM, (hh + 1) * HEAD_DIM)
            q_ref[:, sl] = (normed(acc[:, sl], qn_ref[...]) * ATTN_SCALE).astype(q_ref.dtype)
        keys = []
        for hh in range(GQA_KV_HEADS):
            src = slice(Q_W + hh * HEAD_DIM, Q_W + (hh + 1) * HEAD_DIM)
            keys.append(normed(acc[:, src], kn_ref[...]))
            kv_ref[:, hh * HEAD_DIM:(hh + 1) * HEAD_DIM] = keys[hh].astype(kv_ref.dtype)
        kv_ref[:, KV_W:] = acc[:, Q_W + KV_W:].astype(kv_ref.dtype)
        if not rope:
            kc_ref[...] = jnp.stack(keys, axis=1)
            vc_ref[...] = acc[:, Q_W + KV_W:].reshape(vc_ref.shape)


def _in_proj0(x, norm_g, mods, row_of_tile, w, qn, kn, rope_tabs, tm):
    t = x.shape[0]
    tn = IN0_TN
    rope = rope_tabs is not None
    vec = pl.BlockSpec((1, HEAD_DIM), lambda i, j: (0, 0))
    in_specs = [
        pl.BlockSpec((tm, D_MODEL), lambda i, j: (i, 0)),
        pl.BlockSpec((1, D_MODEL), lambda i, j: (0, 0)),
        _mod_spec(0, row_of_tile),
        _mod_spec(1, row_of_tile),
        pl.BlockSpec((D_MODEL, tn), lambda i, j: (0, j)),
        vec, vec,
    ]
    args = [x, norm_g, mods, mods, w, qn, kn]
    if rope:
        nt = DEC_SEQ // tm
        tab = pl.BlockSpec((tm, HEAD_DIM), lambda i, j: (i % nt, 0))
        in_specs += [tab, tab, tab]
        args += list(rope_tabs)
    out_specs = [
        pl.BlockSpec((tm, tn), lambda i, j: (i, j)),
        pl.BlockSpec((tm, Q_W), lambda i, j: (i, 0)),
        pl.BlockSpec((tm, 2 * KV_W), lambda i, j: (i, 0)),
    ]
    out_shape = [
        jax.ShapeDtypeStruct((t, AB_IN), BF16),
        jax.ShapeDtypeStruct((t, GQA_HEADS * HEAD_DIM), BF16),
        jax.ShapeDtypeStruct((t, 2 * KV_W), BF16),
    ]
    if not rope:
        cache = pl.BlockSpec((tm, GQA_KV_HEADS, HEAD_DIM), lambda i, j: (i, 0, 0))
        out_specs += [cache, cache]
        out_shape += [jax.ShapeDtypeStruct((t, GQA_KV_HEADS, HEAD_DIM), F32)] * 2
    return pl.pallas_call(
        functools.partial(_in0_kernel, rope=rope),
        grid=(t // tm, AB_IN // tn),
        in_specs=in_specs,
        out_specs=out_specs,
        out_shape=out_shape,
        scratch_shapes=[pltpu.VMEM((tm, D_MODEL), BF16)],
        compiler_params=_cparams(2),
        name="in_proj0",
    )(*args)


def _in1_kernel(x_ref, g_ref, shift_ref, scale_ref, w_ref, o_ref, h_scr, *, nb):
    j = pl.program_id(1)

    @pl.when(j == 0)
    def _():
        _ada_norm_rows(h_scr, x_ref, g_ref, shift_ref, scale_ref)

    acc = jnp.dot(h_scr[...], w_ref[...], preferred_element_type=F32)
    o_ref[...] = (acc * jnp.where(j < nb, ATTN_SCALE, 1.0)).astype(o_ref.dtype)


def _in_proj1(x, norm_g, mods, row_of_tile, w, out_dtype, tm, tn):
    t = x.shape[0]
    nb = D_MODEL // tn
    return pl.pallas_call(
        functools.partial(_in1_kernel, nb=nb),
        grid=(t // tm, 3 * nb),
        in_specs=[
            pl.BlockSpec((tm, D_MODEL), lambda i, j: (i, 0)),
            pl.BlockSpec((1, D_MODEL), lambda i, j: (0, 0)),
            _mod_spec(0, row_of_tile),
            _mod_spec(1, row_of_tile),
            pl.BlockSpec((D_MODEL, tn), lambda i, j: (0, j)),
        ],
        out_specs=pl.BlockSpec((None, tm, tn), lambda i, j: (j // nb, i, j % nb)),
        out_shape=jax.ShapeDtypeStruct((3, t, D_MODEL), out_dtype),
        scratch_shapes=[pltpu.VMEM((tm, D_MODEL), BF16)],
        compiler_params=_cparams(2),
        name="in_proj1",
    )(x, norm_g, mods, mods, w)


def _in1q_kernel(x_ref, g_ref, shift_ref, scale_ref, w_ref, q_ref, h_ref):
    @pl.when(pl.program_id(1) == 0)
    def _():
        _ada_norm_rows(h_ref, x_ref, g_ref, shift_ref, scale_ref)

    acc = jnp.dot(h_ref[...], w_ref[...], preferred_element_type=F32)
    q_ref[...] = (acc * ATTN_SCALE).astype(q_ref.dtype)


def _in_proj1_q(x, norm_g, mods, row_of_tile, w, tm, tn):
    t = x.shape[0]
    return pl.pallas_call(
        _in1q_kernel,
        grid=(t // tm, D_MODEL // tn),
        in_specs=[
            pl.BlockSpec((tm, D_MODEL), lambda i, j: (i, 0)),
            pl.BlockSpec((1, D_MODEL), lambda i, j: (0, 0)),
            _mod_spec(0, row_of_tile),
            _mod_spec(1, row_of_tile),
            pl.BlockSpec((D_MODEL, tn), lambda i, j: (0, j)),
        ],
        out_specs=[pl.BlockSpec((tm, tn), lambda i, j: (i, j)),
                   pl.BlockSpec((tm, D_MODEL), lambda i, j: (i, 0))],
        out_shape=[jax.ShapeDtypeStruct((t, D_MODEL), BF16), jax.ShapeDtypeStruct((t, D_MODEL), BF16)],
        compiler_params=_cparams(2),
        name="in_proj1_q",
    )(x, norm_g, mods, mods, w)


def _in1kv_kernel(h_ref, w_ref, flat_ref, heads_ref):
    acc = jnp.dot(h_ref[...], w_ref[...], preferred_element_type=F32)
    flat_ref[...] = acc.astype(flat_ref.dtype)
    heads_ref[...] = acc.reshape(heads_ref.shape)


def _in_proj1_kv(h, w, part, tm, tn):
    t = h.shape[0]
    nb = D_MODEL // tn
    hb = tn // HEAD_DIM
    return pl.pallas_call(
        _in1kv_kernel,
        grid=(t // tm, nb),
        in_specs=[
            pl.BlockSpec((tm, D_MODEL), lambda i, j: (i, 0)),
            pl.BlockSpec((D_MODEL, tn), lambda i, j: (0, part * nb + j)),
        ],
        out_specs=[pl.BlockSpec((tm, tn), lambda i, j: (i, j)),
                   pl.BlockSpec((tm, hb, HEAD_DIM), lambda i, j: (i, j, 0))],
        out_shape=[jax.ShapeDtypeStruct((t, D_MODEL), BF16),
                   jax.ShapeDtypeStruct((t, NA_HEADS, HEAD_DIM), F32)],
        compiler_params=_cparams(2),
        name="in_proj1_kv",
    )(h, w)


def _gated_conv(zc, prev_row, next_row, cw):
    c = CONV_WIDTH
    s = zc.shape[0]
    gb = zc[:, 0:c].astype(F32)
    u = zc[:, c:2 * c].astype(F32) * zc[:, 2 * c:3 * c].astype(F32)
    row = lax.broadcasted_iota(jnp.int32, u.shape, 0)
    prev = jnp.where(row == 0, prev_row, pltpu.roll(u, 1, 0))
    nxt = jnp.where(row == s - 1, next_row, pltpu.roll(u, s - 1, 0))
    return gb * (prev * cw[0:1] + u * cw[1:2] + nxt * cw[2:3])


def _attn0_ctx_kernel(zc_ref, q_ref, kv_ref, cw_ref, *rest):
    n_cast = len(rest) // 2
    o_ref = rest[n_cast]
    _side_cast(rest[:n_cast], rest[n_cast + 1:])
    zero = jnp.zeros((1, CONV_WIDTH), F32)
    a = _gated_conv(zc_ref[...], zero, zero, cw_ref[...])
    o_ref[:, 0:CONV_WIDTH] = a.astype(o_ref.dtype)
    for g in range(GQA_KV_HEADS):
        k = kv_ref[:, g * HEAD_DIM:(g + 1) * HEAD_DIM]
        v = kv_ref[:, KV_W + g * HEAD_DIM:KV_W + (g + 1) * HEAD_DIM]
        for hh in range(GQA_GROUP):
            h = g * GQA_GROUP + hh
            q = q_ref[:, h * HEAD_DIM:(h + 1) * HEAD_DIM]
            o = _softmax_pv([_nt_dot(q, k)], [v])
            o_ref[:, CONV_WIDTH + h * HEAD_DIM:CONV_WIDTH + (h + 1) * HEAD_DIM] = o.astype(o_ref.dtype)


def _attn0_ctx(zc, q, kv, cw, cast_list):
    t = zc.shape[0]
    c_in, c_args, c_out, c_shape = _side_cast_specs(cast_list, t // SEQ, lambda b: b)
    return pl.pallas_call(
        _attn0_ctx_kernel,
        grid=(t // SEQ,),
        in_specs=[
            pl.BlockSpec((SEQ, 3 * CONV_WIDTH), lambda b: (b, 0)),
            pl.BlockSpec((SEQ, GQA_HEADS * HEAD_DIM), lambda b: (b, 0)),
            pl.BlockSpec((SEQ, 2 * KV_W), lambda b: (b, 0)),
            pl.BlockSpec((3, CONV_WIDTH), lambda b: (0, 0)),
        ] + c_in,
        out_specs=[pl.BlockSpec((SEQ, D_MODEL), lambda b: (b, 0))] + c_out,
        out_shape=[jax.ShapeDtypeStruct((t, D_MODEL), BF16)] + c_shape,
        compiler_params=_cparams(1),
        name="mixer0_ctx",
    )(zc, q, kv, cw, *c_args)


def _attn0_lat_kernel(zc_ref, zp_ref, zn_ref, q_ref, kv_ref, ck_ref, cv_ref, cw_ref, *rest, nqb):
    n_cast = len(rest) // 2
    o_ref = rest[n_cast]
    _side_cast(rest[:n_cast], rest[n_cast + 1:])
    qb = pl.program_id(1)
    c = CONV_WIDTH
    up = zp_ref[:, c:2 * c].astype(F32) * zp_ref[:, 2 * c:3 * c].astype(F32)
    un = zn_ref[:, c:2 * c].astype(F32) * zn_ref[:, 2 * c:3 * c].astype(F32)
    up = jnp.where(qb == 0, 0.0, up[_HALO - 1:_HALO, :])
    un = jnp.where(qb == nqb - 1, 0.0, un[0:1, :])
    a = _gated_conv(zc_ref[...], up, un, cw_ref[...])
    o_ref[:, 0:c] = a.astype(o_ref.dtype)
    for g in range(GQA_KV_HEADS):
        ks = slice(g * HEAD_DIM, (g + 1) * HEAD_DIM)
        vs = slice(KV_W + g * HEAD_DIM, KV_W + (g + 1) * HEAD_DIM)
        k = kv_ref[:, ks]
        v = kv_ref[:, vs]
        ck = ck_ref[:, ks].astype(BF16)
        cv = cv_ref[:, ks].astype(BF16)
        for hh in range(GQA_GROUP):
            h = g * GQA_GROUP + hh
            q = q_ref[:, h * HEAD_DIM:(h + 1) * HEAD_DIM]
            o = _softmax_pv([_nt_dot(q, ck), _nt_dot(q, k)], [cv, v])
            o_ref[:, c + h * HEAD_DIM:c + (h + 1) * HEAD_DIM] = o.astype(o_ref.dtype)


def _attn0_lat(zc, q, kv, ck, cv, cw, tq, cast_list):
    nb, t, _ = zc.shape
    nqb = t // tq
    hb = tq // _HALO
    c_in, c_args, c_out, c_shape = _side_cast_specs(cast_list, nb * nqb, lambda b, i: b * nqb + i)
    return pl.pallas_call(
        functools.partial(_attn0_lat_kernel, nqb=nqb),
        grid=(nb, nqb),
        in_specs=[
            pl.BlockSpec((None, tq, 3 * CONV_WIDTH), lambda b, i: (b, i, 0)),
            pl.BlockSpec((None, _HALO, 3 * CONV_WIDTH), lambda b, i: (b, jnp.maximum(i * hb - 1, 0), 0)),
            pl.BlockSpec((None, _HALO, 3 * CONV_WIDTH),
                         lambda b, i: (b, jnp.minimum((i + 1) * hb, t // _HALO - 1), 0)),
            pl.BlockSpec((None, tq, GQA_HEADS * HEAD_DIM), lambda b, i: (b, i, 0)),
            pl.BlockSpec((None, t, 2 * KV_W), lambda b, i: (b, 0, 0)),
            pl.BlockSpec((None, PAST_LEN, KV_W), lambda b, i: (b, 0, 0)),
            pl.BlockSpec((None, PAST_LEN, KV_W), lambda b, i: (b, 0, 0)),
            pl.BlockSpec((3, CONV_WIDTH), lambda b, i: (0, 0)),
        ] + c_in,
        out_specs=[pl.BlockSpec((None, tq, D_MODEL), lambda b, i: (b, i, 0))] + c_out,
        out_shape=[jax.ShapeDtypeStruct((nb, t, D_MODEL), BF16)] + c_shape,
        compiler_params=_cparams(2),
        name="mixer0_lat",
    )(zc, zc, zc, q, kv, ck, cv, cw, *c_args)


CTX1_SEQS = 2


def _attn1_ctx_kernel(q_ref, k_ref, v_ref, o_ref):
    for s in range(CTX1_SEQS):
        rows = slice(s * SEQ, (s + 1) * SEQ)
        for h in range(NA_HEADS):
            sl = slice(h * HEAD_DIM, (h + 1) * HEAD_DIM)
            o = _softmax_pv([_nt_dot(q_ref[rows, sl], k_ref[rows, sl])], [v_ref[rows, sl]])
            o_ref[rows, sl] = o.astype(o_ref.dtype)


def _attn1_ctx(q, k, v):
    t = q.shape[0]
    blk = pl.BlockSpec((CTX1_SEQS * SEQ, D_MODEL), lambda b: (b, 0))
    return pl.pallas_call(
        _attn1_ctx_kernel,
        grid=(t // (CTX1_SEQS * SEQ),),
        in_specs=[blk, blk, blk],
        out_specs=blk,
        out_shape=jax.ShapeDtypeStruct((t, D_MODEL), BF16),
        compiler_params=_cparams(1),
        name="mixer1_ctx",
    )(q, k, v)


def _na_slab_row(blk):
    return min(max(NA_QROWS * blk - NA_WIN_ROWS // 2, 0), GRID_H - NA_SLAB_ROWS)


NA_NROW = 2 * NA_WIN_ROWS - 1
NA_MASKED = NA_NROW
NA_FIRST_IN = NA_WIN_ROWS - 1 - NA_WIN_ROWS // 2
NA_LAST_IN = NA_FIRST_IN + NA_WIN_ROWS - 1
NA_PIECE_PAIRS = ([(m, m + 1) for m in range(NA_NROW - 1)]
                  + [(NA_MASKED, NA_FIRST_IN), (NA_LAST_IN, NA_MASKED)])


def _na_piece(blk, dr, jp):
    r = NA_QROWS * blk + dr
    ks = min(max(r - NA_WIN_ROWS // 2, 0), GRID_H - NA_WIN_ROWS)
    pair = []
    for j in (2 * jp, 2 * jp + 1):
        krow = _na_slab_row(blk) + j
        pair.append(krow - r + NA_WIN_ROWS - 1 if ks <= krow < ks + NA_WIN_ROWS else NA_MASKED)
    pair = tuple(pair)
    if pair == (NA_MASKED, NA_MASKED):
        return None
    return NA_PIECE_PAIRS.index(pair)


NA_PAD = GRID_W - NA_WIN_COLS


def _na_bias_table(rel_bias):
    return jnp.pad(rel_bias, ((0, 0), (0, 1), (NA_PAD, 2 * GRID_W - (2 * NA_WIN_COLS - 1) - NA_PAD)),
                   constant_values=MASK_NEG)


def _na_fill_pieces(table_ref, pieces_ref):
    shape = (GRID_W, 2 * GRID_W)
    c = lax.broadcasted_iota(jnp.int32, shape, 0)
    lane = lax.broadcasted_iota(jnp.int32, shape, 1)
    kc = lane & (GRID_W - 1)
    cs = jnp.clip(c - NA_WIN_COLS // 2, 0, GRID_W - NA_WIN_COLS)
    in_window = (kc >= cs) & (kc < cs + NA_WIN_COLS)

    def expand(a, shift):
        row = jnp.broadcast_to(table_ref[a:a + 1, :], shape)
        return pltpu.roll(row, shift, 1, stride=1, stride_axis=0)

    left_shift = 2 * GRID_W - (GRID_W - 1)
    right_shift = 1
    for m, (a_left, a_right) in enumerate(NA_PIECE_PAIRS):
        both = jnp.where(lane < GRID_W, expand(a_left, left_shift), expand(a_right, right_shift))
        pieces_ref[m] = jnp.where(in_window, both * LOG2E, MASK_NEG)


def _na_bias_block(pieces_ref, blk):
    rows = []
    for dr in range(NA_QROWS):
        cols = []
        for jp in range(NA_SLAB_ROWS // 2):
            m = _na_piece(blk, dr, jp)
            cols.append(jnp.full((GRID_W, 2 * GRID_W), MASK_NEG, F32) if m is None else pieces_ref[m])
        rows.append(jnp.concatenate(cols, axis=1))
    return jnp.concatenate(rows, axis=0)


def _attn1_lat_kernel(q_ref, k_ref, v_ref, ck_ref, cv_ref, table_ref, o_ref, pieces_ref):
    _na_fill_pieces(table_ref, pieces_ref)
    head_rows = pl.ds(pl.program_id(1), PAST_LEN, stride=NA_HEADS)
    ck = ck_ref[head_rows, :].astype(BF16)
    cv = cv_ref[head_rows, :].astype(BF16)
    for blk in range(NA_NBLK):
        rows = slice(blk * NA_QBLK, (blk + 1) * NA_QBLK)
        s0 = _na_slab_row(blk) * GRID_W
        slab = slice(s0, s0 + NA_SLAB)
        q = q_ref[rows, :]
        s_loc = _nt_dot(q, k_ref[slab, :]) + _na_bias_block(pieces_ref, blk)
        o = _softmax_pv([s_loc, _nt_dot(q, ck)], [v_ref[slab, :], cv])
        o_ref[rows, :] = o.astype(o_ref.dtype)


def _attn1_lat(qkv, ck, cv, table):
    _, nb, t, _ = qkv.shape
    part = lambda p: pl.BlockSpec((None, None, t, HEAD_DIM), lambda b, h: (p, b, 0, h))
    head = pl.BlockSpec((None, t, HEAD_DIM), lambda b, h: (b, 0, h))
    ctx = pl.BlockSpec((None, PAST_LEN * NA_HEADS, HEAD_DIM), lambda b, h: (b, 0, 0))
    return pl.pallas_call(
        _attn1_lat_kernel,
        grid=(nb, NA_HEADS),
        in_specs=[part(0), part(1), part(2), ctx, ctx,
                  pl.BlockSpec((None, NA_NROW + 1, 2 * GRID_W), lambda b, h: (h, 0, 0))],
        out_specs=head,
        out_shape=jax.ShapeDtypeStruct((nb, t, D_MODEL), BF16),
        scratch_shapes=[pltpu.VMEM((len(NA_PIECE_PAIRS), GRID_W, 2 * GRID_W), F32)],
        compiler_params=_cparams(2),
        name="mixer1_lat",
    )(qkv, qkv, qkv, ck, cv, table)


def _out_proj_kernel(m_ref, w_ref, x_ref, gate_ref, o_ref):
    acc = jnp.dot(m_ref[...], w_ref[...], preferred_element_type=F32)
    o_ref[...] = x_ref[...] + gate_ref[...] * acc


def _out_proj(mix, w, x, mods, row_of_tile, tm, tn):
    t, k = mix.shape
    return pl.pallas_call(
        _out_proj_kernel,
        grid=(t // tm, D_MODEL // tn),
        in_specs=[
            pl.BlockSpec((tm, k), lambda i, j: (i, 0)),
            pl.BlockSpec((k, tn), lambda i, j: (0, j)),
            pl.BlockSpec((tm, tn), lambda i, j: (i, j)),
            pl.BlockSpec((None, None, 1, tn), lambda i, j: (row_of_tile(i), 2, 0, j)),
        ],
        out_specs=pl.BlockSpec((tm, tn), lambda i, j: (i, j)),
        out_shape=jax.ShapeDtypeStruct((t, D_MODEL), F32),
        compiler_params=_cparams(2),
        name="out_proj",
    )(mix, w, x, mods)


def _mlp_kernel(x_ref, g_ref, shift_ref, scale_ref, gate_ref, w1_ref, w2_ref, fg_ref, *rest, nf, final):
    n_cast = (len(rest) - 2) // 2
    o_ref, h_scr = rest[n_cast], rest[-1]
    _side_cast(rest[:n_cast], rest[n_cast + 1:-1])
    f = pl.program_id(1)

    @pl.when(f == 0)
    def _():
        _ada_norm_rows(h_scr, x_ref, g_ref, shift_ref, scale_ref)
        o_ref[...] = jnp.zeros_like(o_ref)

    u = jnp.dot(h_scr[...], w1_ref[...], preferred_element_type=F32)
    u = jnp.square(jnp.maximum(u, 0.0)).astype(BF16)
    o_ref[...] += jnp.dot(u, w2_ref[...], preferred_element_type=F32)

    @pl.when(f == nf - 1)
    def _():
        y = x_ref[...] + gate_ref[...] * o_ref[...]
        if final:
            ms = jnp.mean(y * y, axis=-1, keepdims=True)
            y = y * lax.rsqrt(ms + NORM_EPS) * fg_ref[...]
        o_ref[...] = y


def _mlp(x, norm_g, mods, row_of_tile, w1, w2, final_g, final, tm, tf, cast_next=None):
    t = x.shape[0]
    nf = D_FF // tf
    nt = t // tm

    def mod(which):
        return pl.BlockSpec((None, None, 1, D_MODEL), lambda i, f: (row_of_tile(i), which, 0, 0))

    in_specs = [
        pl.BlockSpec((tm, D_MODEL), lambda i, f: (i, 0)),
        pl.BlockSpec((1, D_MODEL), lambda i, f: (0, 0)),
        mod(3), mod(4), mod(5),
        pl.BlockSpec((D_MODEL, tf), lambda i, f: (0, f)),
        pl.BlockSpec((tf, D_MODEL), lambda i, f: (f, 0)),
        pl.BlockSpec((1, D_MODEL), lambda i, f: (0, 0)),
    ]
    args = [x, norm_g, mods, mods, mods, w1, w2, final_g]
    out_specs = [pl.BlockSpec((tm, D_MODEL), lambda i, f: (i, 0))]
    out_shape = [jax.ShapeDtypeStruct((t, D_MODEL), F32)]
    c_in, c_args, c_out, c_shape = _side_cast_specs(cast_next or (), nt * nf, lambda i, f: i * nf + f)
    in_specs += c_in
    args += c_args
    out_specs += c_out
    out_shape += c_shape
    res = pl.pallas_call(
        functools.partial(_mlp_kernel, nf=nf, final=final),
        grid=(nt, nf),
        in_specs=in_specs,
        out_specs=out_specs,
        out_shape=out_shape,
        scratch_shapes=[pltpu.VMEM((tm, D_MODEL), BF16)],
        compiler_params=_cparams(2),
        name="mlp",
    )(*args)
    return res if cast_next else res[0]


def _rope_tables():
    t = np.arange(DEC_SEQ)
    half = HEAD_DIM // 4
    inv = ROPE_THETA ** (-np.arange(half, dtype=np.float32) / half)
    ang_r = (t // GRID_W).astype(np.float32)[:, None] * inv
    ang_c = (t % GRID_W).astype(np.float32)[:, None] * inv
    zero = np.zeros_like(ang_r)
    cos = np.concatenate([np.cos(ang_r)] * 2 + [np.cos(ang_c)] * 2, axis=-1)
    sin_lo = np.concatenate([-np.sin(ang_r), zero, -np.sin(ang_c), zero], axis=-1)
    sin_hi = np.concatenate([zero, np.sin(ang_r), zero, np.sin(ang_c)], axis=-1)
    return tuple(jnp.asarray(a, F32) for a in (cos, sin_lo, sin_hi))


def kernel(x_prompt, x_sample, cache_attn_k, cache_attn_v, cache_na_k, cache_na_v, c, c_ctx, mod_w, mod_b,
           norm1_g, norm2_g, ab_w_in, ab_conv_w, ab_q_norm, ab_k_norm, ab_w_out, na_w_qkv, na_rel_bias,
           na_w_out, mlp_w1, mlp_w2, final_norm_g):
    n_ctx = BATCH * SEQ
    xp = x_prompt.reshape(n_ctx, D_MODEL)
    xs = x_sample.reshape(DEC_BATCH * DEC_SEQ, D_MODEL)

    cond = jnp.concatenate([c_ctx[None, :], c, jnp.zeros((8 - 1 - DEC_BATCH, D_MODEL), F32)], axis=0)
    mods = _modulation(cond, mod_w, mod_b).reshape(2, 8, 6, 1, D_MODEL)

    def ctx_row(i):
        return 0

    def lat_row(tm):
        return lambda i: 1 + i // (DEC_SEQ // tm)

    rope_tabs = _rope_tables()
    fg = final_norm_g.reshape(1, D_MODEL)

    m0 = mods[0]
    g1 = norm1_g[0].reshape(1, D_MODEL)
    w_in = ab_w_in[0].astype(BF16)
    qn = ab_q_norm[0].reshape(1, HEAD_DIM)
    kn = ab_k_norm[0].reshape(1, HEAD_DIM)
    cw = ab_conv_w[0]

    zc_p, q_p, kv_p, k_cache, v_cache = _in_proj0(xp, g1, m0, ctx_row, w_in, qn, kn, None, IN0_TM)
    zc_s, q_s, kv_s = _in_proj0(xs, g1, m0, lat_row(IN0_TM), w_in, qn, kn, rope_tabs, IN0_TM)
    new_attn_k = k_cache.reshape(BATCH, 1, SEQ, GQA_KV_HEADS, HEAD_DIM)
    new_attn_v = v_cache.reshape(BATCH, 1, SEQ, GQA_KV_HEADS, HEAD_DIM)

    mix_p, w_out = _attn0_ctx(zc_p, q_p, kv_p, cw, [(ab_w_out, 0)])
    lat3 = lambda a: a.reshape(DEC_BATCH, DEC_SEQ, a.shape[-1])
    mix_s, w1, w2 = _attn0_lat(lat3(zc_s), lat3(q_s), lat3(kv_s),
                               cache_attn_k[:, 0].reshape(DEC_BATCH, PAST_LEN, KV_W),
                               cache_attn_v[:, 0].reshape(DEC_BATCH, PAST_LEN, KV_W), cw, LAT0_TQ,
                               [(mlp_w1, 0), (mlp_w2, 0)])
    mix_s = mix_s.reshape(DEC_BATCH * DEC_SEQ, D_MODEL)

    xp = _out_proj(mix_p, w_out, xp, m0, ctx_row, OUT_TM, D_MODEL)
    xs = _out_proj(mix_s, w_out, xs, m0, lat_row(OUT_TM), OUT_TM, D_MODEL)

    g2 = norm2_g[0].reshape(1, D_MODEL)
    xp, w1_next, w2_next = _mlp(xp, g2, m0, ctx_row, w1, w2, fg, False, MLP_TM, MLP_TF,
                                cast_next=[(mlp_w1, 1), (mlp_w2, 1)])
    xs, w_qkv, w_out = _mlp(xs, g2, m0, lat_row(MLP_TM), w1, w2, fg, False, MLP_TM, MLP_TF,
                            cast_next=[(na_w_qkv, 0), (na_w_out, 0)])

    m1 = mods[1]
    g1 = norm1_g[1].reshape(1, D_MODEL)

    q_p, h_p = _in_proj1_q(xp, g1, m1, ctx_row, w_qkv, IN1_CTX_TM, D_MODEL)
    k_p, k_heads = _in_proj1_kv(h_p, w_qkv, 1, IN1_CTX_TM, D_MODEL)
    v_p, v_heads = _in_proj1_kv(h_p, w_qkv, 2, IN1_CTX_TM, D_MODEL)
    qkv_s = _in_proj1(xs, g1, m1, lat_row(IN1_TM), w_qkv, BF16, IN1_TM, IN1_TN)
    new_na_k = k_heads.reshape(BATCH, 1, SEQ, NA_HEADS, HEAD_DIM)
    new_na_v = v_heads.reshape(BATCH, 1, SEQ, NA_HEADS, HEAD_DIM)

    mix_p = _attn1_ctx(q_p, k_p, v_p)
    mix_s = _attn1_lat(qkv_s.reshape(3, DEC_BATCH, DEC_SEQ, D_MODEL),
                       cache_na_k[:, 0].reshape(DEC_BATCH, PAST_LEN * NA_HEADS, HEAD_DIM),
                       cache_na_v[:, 0].reshape(DEC_BATCH, PAST_LEN * NA_HEADS, HEAD_DIM),
                       _na_bias_table(na_rel_bias[0]))
    mix_s = mix_s.reshape(DEC_BATCH * DEC_SEQ, D_MODEL)

    xp = _out_proj(mix_p, w_out, xp, m1, ctx_row, OUT_TM, D_MODEL)
    xs = _out_proj(mix_s, w_out, xs, m1, lat_row(OUT_TM), OUT_TM, D_MODEL)

    g2 = norm2_g[1].reshape(1, D_MODEL)
    yp = _mlp(xp, g2, m1, ctx_row, w1_next, w2_next, fg, True, MLP_TM, MLP_TF)
    ys = _mlp(xs, g2, m1, lat_row(MLP_TM), w1_next, w2_next, fg, True, MLP_TM, MLP_TF)

    return (yp.reshape(BATCH, SEQ, D_MODEL), ys.reshape(DEC_BATCH, DEC_SEQ, D_MODEL),
            new_attn_k, new_attn_v, new_na_k, new_na_v)
```
